```python
import math
import jax, jax.numpy as jnp
from jax import lax
import numpy as np

D_MODEL = 1024
BATCH = 8
SEQ = 2048
DEPTH = 2

N_MIXERS = 2
N_A_LAYERS = (DEPTH + 1) // 2
N_B_LAYERS = DEPTH // 2
RMS_EPS = 1e-6

D_FF = 2816
FFN_RES = 0.5

D_RNN = 1280
N_RNN_BLOCKS = 10
RNN_BLOCK = D_RNN // N_RNN_BLOCKS
CONV_WIDTH = 4
LRU_C = 8.0

HEAD_DIM = 64
HEADS_PER_GROUP = D_MODEL // HEAD_DIM
DILATION_GROUPS = ((128, 1), (512, 4), (2048, 16))
N_GROUPS = len(DILATION_GROUPS)
N_ATT_HEADS = N_GROUPS * HEADS_PER_GROUP
QKV_WIDTH = 3 * N_ATT_HEADS * HEAD_DIM
ATT_OUT_WIDTH = HEADS_PER_GROUP * HEAD_DIM
N_BUCKETS = 32
MAX_DISTANCE = 2048

kernel_name = 'hybrid_rglru_dilated_attn_macaron'


def rms_norm(x, g):
    xf = x.astype(jnp.float32)
    y = xf * lax.rsqrt(jnp.mean(xf * xf, axis=-1, keepdims=True) + RMS_EPS)
    return (y * g.astype(jnp.float32)).astype(x.dtype)


def swiglu(x, w_in, w_out):
    gate, up = jnp.split(x @ w_in, 2, axis=-1)
    return (jax.nn.silu(gate) * up) @ w_out


def rglru_mixer(x, w_in, conv_w, conv_b, w_a, b_a, w_x, b_x, lam, w_out):
    B, S, _ = x.shape
    gate, u = jnp.split(x @ w_in, 2, axis=-1)
    up = jnp.pad(u, ((0, 0), (CONV_WIDTH - 1, 0), (0, 0)))
    conv = conv_b
    for k in range(CONV_WIDTH):
        conv = conv + up[:, k:k + S] * conv_w[k]
    uf = conv.astype(jnp.float32)
    ub = uf.reshape(B, S, N_RNN_BLOCKS, RNN_BLOCK)
    r = jax.nn.sigmoid(jnp.einsum('bsnc,ncd->bsnd', ub, w_a.astype(jnp.float32)).reshape(B, S, D_RNN) + b_a.astype(jnp.float32))
    i = jax.nn.sigmoid(jnp.einsum('bsnc,ncd->bsnd', ub, w_x.astype(jnp.float32)).reshape(B, S, D_RNN) + b_x.astype(jnp.float32))
    log_a = -LRU_C * r * jax.nn.softplus(-lam.astype(jnp.float32))
    a = jnp.exp(log_a)
    b = jnp.sqrt(-jnp.expm1(2.0 * log_a)) * (i * uf)

    def combine(left, right):
        a1, b1 = left
        a2, b2 = right
        return a1 * a2, a2 * b1 + b2

    _, h = lax.associative_scan(combine, (a, b), axis=1)
    y = h.astype(x.dtype) * jax.nn.gelu(gate)
    return y @ w_out


def t5_causal_bucket(dist):
    max_exact = N_BUCKETS // 2
    n = jnp.maximum(dist, 0)
    nf = jnp.maximum(n, 1).astype(jnp.float32)
    large = max_exact + (jnp.log(nf / max_exact) / math.log(MAX_DISTANCE / max_exact) * (N_BUCKETS - max_exact)).astype(jnp.int32)
    large = jnp.minimum(large, N_BUCKETS - 1)
    return jnp.where(n < max_exact, n, large)


def dilated_group(q, k, v, bias_tbl, window, dilation):
    B, S, H, Dh = q.shape
    d = dilation
    blk = window // dilation
    span = d * blk
    Sp = ((S + span - 1) // span) * span
    pad = ((0, 0), (0, Sp - S), (0, 0), (0, 0))
    L = Sp // d
    nb = L // blk

    def to_sub(t):
        t = jnp.pad(t, pad).reshape(B, L, d, H, Dh).transpose(0, 2, 1, 3, 4)
        return t.reshape(B, d, nb, blk, H, Dh)

    def with_prev(t):
        prev = jnp.pad(t, ((0, 0), (0, 0), (1, 0), (0, 0), (0, 0), (0, 0)))[:, :, :-1]
        return jnp.concatenate([prev, t], axis=3)

    qs = to_sub(q).astype(jnp.float32)
    ks = with_prev(to_sub(k)).astype(jnp.float32)
    vs = with_prev(to_sub(v))
    scores = jnp.einsum('brnqhc,brnkhc->brnhqk', qs, ks) * (HEAD_DIM ** -0.5)

    qi = jnp.arange(blk)[:, None]
    kj = jnp.arange(2 * blk)[None, :]
    dist = qi - kj + blk
    band = (dist >= 0) & (dist <= blk)
    key_exists = (jnp.arange(nb)[:, None, None] > 0) | (kj[None] >= blk)
    mask = (band[None] & key_exists)[:, None]
    bias = bias_tbl.astype(jnp.float32)[t5_causal_bucket(dist * d)].transpose(2, 0, 1)
    scores = jnp.where(mask, scores + bias, -jnp.inf)
    lse = jax.nn.logsumexp(scores, axis=-1)
    p = jnp.exp(scores - lse[..., None])
    o = jnp.einsum('brnhqk,brnkhc->brnqhc', p, vs.astype(jnp.float32))
    o = o.reshape(B, d, L, H, Dh).transpose(0, 2, 1, 3, 4).reshape(B, Sp, H, Dh)[:, :S]
    lse = lse.transpose(0, 1, 2, 4, 3).reshape(B, d, L, H).transpose(0, 2, 1, 3).reshape(B, Sp, H)[:, :S]
    return o, lse


def dilated_attention(x, w_qkv, q_gain, k_gain, rel_bias, w_o):
    B, S, _ = x.shape
    qkv = (x @ w_qkv).reshape(B, S, 3, N_GROUPS, HEADS_PER_GROUP, HEAD_DIM)
    q = rms_norm(qkv[:, :, 0], q_gain)
    k = rms_norm(qkv[:, :, 1], k_gain)
    v = qkv[:, :, 2]
    outs, lses = [], []
    for g, (window, dilation) in enumerate(DILATION_GROUPS):
        tbl = rel_bias[:, g * HEADS_PER_GROUP:(g + 1) * HEADS_PER_GROUP]
        o, lse = dilated_group(q[:, :, g], k[:, :, g], v[:, :, g], tbl, window, dilation)
        outs.append(o)
        lses.append(lse)
    wts = jax.nn.softmax(jnp.stack(lses, axis=0), axis=0)
    o = jnp.einsum('gbsh,gbshc->bshc', wts, jnp.stack(outs, axis=0))
    return o.reshape(B, S, ATT_OUT_WIDTH).astype(x.dtype) @ w_o


def setup_inputs(seed: int = 0) -> dict:
    key = jax.random.key(seed)
    ks = jax.random.split(key, 24)
    nrm = jax.random.normal
    f32 = jnp.float32
    x = nrm(ks[0], (BATCH, SEQ, D_MODEL), f32)
    norm_g = 1.0 + 0.05 * nrm(ks[1], (DEPTH, 3, D_MODEL), f32)
    ffn_w_in = nrm(ks[2], (DEPTH, 2, D_MODEL, 2 * D_FF), f32) * D_MODEL ** -0.5
    ffn_w_out = nrm(ks[3], (DEPTH, 2, D_FF, D_MODEL), f32) * D_FF ** -0.5
    rnn_w_in = nrm(ks[4], (N_A_LAYERS, D_MODEL, 2 * D_RNN), f32) * D_MODEL ** -0.5
    rnn_conv_w = nrm(ks[5], (N_A_LAYERS, CONV_WIDTH, D_RNN), f32) * CONV_WIDTH ** -0.5
    rnn_conv_b = 0.02 * nrm(ks[6], (N_A_LAYERS, D_RNN), f32)
    rnn_w_a = nrm(ks[7], (N_A_LAYERS, N_RNN_BLOCKS, RNN_BLOCK, RNN_BLOCK), f32) * RNN_BLOCK ** -0.5
    rnn_b_a = 0.02 * nrm(ks[8], (N_A_LAYERS, D_RNN), f32)
    rnn_w_x = nrm(ks[9], (N_A_LAYERS, N_RNN_BLOCKS, RNN_BLOCK, RNN_BLOCK), f32) * RNN_BLOCK ** -0.5
    rnn_b_x = 0.02 * nrm(ks[10], (N_A_LAYERS, D_RNN), f32)
    a_c = jax.random.uniform(ks[11], (N_A_LAYERS, D_RNN), f32, 0.9, 0.999)
    a_base = a_c ** (1.0 / LRU_C)
    rnn_lambda = jnp.log(a_base) - jnp.log1p(-a_base)
    rnn_w_out = nrm(ks[12], (N_A_LAYERS, D_RNN, D_MODEL), f32) * D_RNN ** -0.5
    att_w_qkv = nrm(ks[13], (N_B_LAYERS, D_MODEL, QKV_WIDTH), f32) * D_MODEL ** -0.5
    att_q_gain = 1.0 + 0.05 * nrm(ks[14], (N_B_LAYERS, HEAD_DIM), f32)
    att_k_gain = 1.0 + 0.05 * nrm(ks[15], (N_B_LAYERS, HEAD_DIM), f32)
    att_w_o = nrm(ks[16], (N_B_LAYERS, ATT_OUT_WIDTH, D_MODEL), f32) * ATT_OUT_WIDTH ** -0.5
    rel_bias = 0.5 * nrm(ks[17], (N_BUCKETS, N_ATT_HEADS), f32)
    return {'x': x, 'norm_g': norm_g, 'ffn_w_in': ffn_w_in, 'ffn_w_out': ffn_w_out,
            'rnn_w_in': rnn_w_in, 'rnn_conv_w': rnn_conv_w, 'rnn_conv_b': rnn_conv_b,
            'rnn_w_a': rnn_w_a, 'rnn_b_a': rnn_b_a, 'rnn_w_x': rnn_w_x, 'rnn_b_x': rnn_b_x,
            'rnn_lambda': rnn_lambda, 'rnn_w_out': rnn_w_out,
            'att_w_qkv': att_w_qkv, 'att_q_gain': att_q_gain, 'att_k_gain': att_k_gain,
            'att_w_o': att_w_o, 'rel_bias': rel_bias}


def reference(x, norm_g, ffn_w_in, ffn_w_out, rnn_w_in, rnn_conv_w, rnn_conv_b,
              rnn_w_a, rnn_b_a, rnn_w_x, rnn_b_x, rnn_lambda, rnn_w_out,
              att_w_qkv, att_q_gain, att_k_gain, att_w_o, rel_bias):
    for layer in range(DEPTH):
        g = norm_g[layer]
        x = x + FFN_RES * swiglu(rms_norm(x, g[0]), ffn_w_in[layer, 0], ffn_w_out[layer, 0])
        h = rms_norm(x, g[1])
        j = layer // N_MIXERS
        if layer % N_MIXERS == 0:
            x = x + rglru_mixer(h, rnn_w_in[j], rnn_conv_w[j], rnn_conv_b[j], rnn_w_a[j], rnn_b_a[j],
                                rnn_w_x[j], rnn_b_x[j], rnn_lambda[j], rnn_w_out[j])
        else:
            x = x + dilated_attention(h, att_w_qkv[j], att_q_gain[j], att_k_gain[j], rel_bias, att_w_o[j])
        x = x + FFN_RES * swiglu(rms_norm(x, g[2]), ffn_w_in[layer, 1], ffn_w_out[layer, 1])
    return x
```

```python
import math

import numpy as np
import jax
import jax.numpy as jnp
from jax import lax
from jax.experimental import pallas as pl
from jax.experimental.pallas import tpu as pltpu

F32 = jnp.float32
BF16 = jnp.bfloat16

D_MODEL = 1024
RMS_EPS = 1e-6
D_FF = 2816
FFN_RES = 0.5
D_RNN = 1280
N_RNN_BLOCKS = 10
RNN_BLOCK = 128
CONV_WIDTH = 4
LRU_C = 8.0
HEAD_DIM = 64
N_HEADS = 16
DILATION_GROUPS = ((128, 1), (512, 4), (2048, 16))
N_GROUPS = 3
N_BUCKETS = 32
MAX_DISTANCE = 2048
BLK = 128

VMEM_LIMIT_BYTES = 56 * 1024 * 1024
FFN_TM = 512
FFN_CHUNK = 256
RNN_TS = 64
NEG = -1e30
PAD = 32
QUAD = 4
QW = QUAD * HEAD_DIM


def _rms(x, g):
    return x * lax.rsqrt(jnp.mean(x * x, axis=-1, keepdims=True) + RMS_EPS) * g


def _dot(a, b):
    return jnp.dot(a, b, preferred_element_type=F32)


def _ffn_body(*refs, emit_norm):
    if emit_norm:
        x_ref, g_ref, win_ref, wout_ref, g2_ref, o_ref, xn_ref, a_scr = refs
    else:
        x_ref, g_ref, win_ref, wout_ref, o_ref, a_scr = refs
    x = x_ref[...]
    xn = _rms(x, g_ref[...]).astype(BF16)
    for c in range(D_FF // FFN_CHUNK):
        lo = c * FFN_CHUNK
        gate = _dot(xn, win_ref[:, lo:lo + FFN_CHUNK])
        up = _dot(xn, win_ref[:, D_FF + lo:D_FF + lo + FFN_CHUNK])
        a_scr[:, lo:lo + FFN_CHUNK] = (gate * jax.nn.sigmoid(gate) * up).astype(BF16)
    y = x + FFN_RES * _dot(a_scr[...], wout_ref[...])
    o_ref[...] = y
    if emit_norm:
        xn_ref[...] = _rms(y, g2_ref[...]).astype(BF16)


def _ffn(x2d, g, w_in, w_out, g_next=None):
    m = x2d.shape[0]
    emit_norm = g_next is not None
    const = lambda i: (0, 0)
    row = lambda i: (i, 0)
    in_specs = [
        pl.BlockSpec((FFN_TM, D_MODEL), row),
        pl.BlockSpec((1, D_MODEL), const),
        pl.BlockSpec((D_MODEL, 2 * D_FF), const, pipeline_mode=pl.Buffered(1)),
        pl.BlockSpec((D_FF, D_MODEL), const, pipeline_mode=pl.Buffered(1)),
    ]
    args = [x2d, g.reshape(1, D_MODEL), w_in, w_out]
    out_shape = [jax.ShapeDtypeStruct((m, D_MODEL), F32)]
    out_specs = [pl.BlockSpec((FFN_TM, D_MODEL), row)]
    if emit_norm:
        in_specs.append(pl.BlockSpec((1, D_MODEL), const))
        args.append(g_next.reshape(1, D_MODEL))
        out_shape.append(jax.ShapeDtypeStruct((m, D_MODEL), BF16))
        out_specs.append(pl.BlockSpec((FFN_TM, D_MODEL), row))
    res = pl.pallas_call(
        lambda *refs: _ffn_body(*refs, emit_norm=emit_norm),
        grid=(m // FFN_TM,),
        in_specs=in_specs,
        out_specs=out_specs,
        out_shape=out_shape,
        scratch_shapes=[pltpu.VMEM((FFN_TM, D_FF), BF16)],
        compiler_params=pltpu.CompilerParams(
            dimension_semantics=("arbitrary",), vmem_limit_bytes=VMEM_LIMIT_BYTES),
        name="ffn_norm" if emit_norm else "ffn",
    )(*args)
    return res if emit_norm else res[0]


def _rglru_body(x_ref, g_ref, win_ref, cw_ref, cb_ref, wax_ref, ba_ref, bx_ref, lam_ref, wout_ref,
                o_ref, u_scr, a_scr, b_scr, h_scr):
    nb = x_ref.shape[0]
    m = RNN_TS * nb
    hist = (CONV_WIDTH - 1) * nb

    @pl.when(pl.program_id(0) == 0)
    def _():
        u_scr[0:hist, :] = jnp.zeros((hist, D_RNN), F32)
        h_scr[...] = jnp.zeros_like(h_scr)

    x = jnp.concatenate([x_ref[:, t * D_MODEL:(t + 1) * D_MODEL] for t in range(RNN_TS)], axis=0)
    xn = _rms(x, g_ref[...]).astype(BF16)
    gu = _dot(xn, win_ref[...])
    gate = gu[:, :D_RNN]
    u_scr[hist:hist + m, :] = gu[:, D_RNN:]
    conv = cb_ref[...]
    for k in range(CONV_WIDTH):
        conv = conv + u_scr[k * nb:k * nb + m, :] * cw_ref[k:k + 1, :]
    u_scr[0:hist, :] = u_scr[m:m + hist, :]

    sp = jax.nn.softplus(-lam_ref[...])
    for n in range(N_RNN_BLOCKS):
        sl = slice(n * RNN_BLOCK, (n + 1) * RNN_BLOCK)
        cn = conv[:, sl]
        ra = _dot(cn.astype(BF16), wax_ref[n])
        r = jax.nn.sigmoid(ra[:, :RNN_BLOCK] + ba_ref[:, sl])
        gi = jax.nn.sigmoid(ra[:, RNN_BLOCK:] + bx_ref[:, sl])
        log_a = -LRU_C * r * sp[:, sl]
        a = jnp.exp(log_a)
        a_scr[:, sl] = a
        b_scr[:, sl] = jnp.sqrt(-jnp.tanh(log_a) * (1.0 + a * a)) * (gi * cn)

    def step(t, h):
        r0 = pl.multiple_of(t * nb, nb)
        h = a_scr[pl.ds(r0, nb), :] * h + b_scr[pl.ds(r0, nb), :]
        b_scr[pl.ds(r0, nb), :] = h
        return h

    h_scr[...] = lax.fori_loop(0, RNN_TS, step, h_scr[...], unroll=8)
    y = (b_scr[...] * jax.nn.gelu(gate)).astype(BF16)
    out = x + _dot(y, wout_ref[...])
    for t in range(RNN_TS):
        o_ref[:, t * D_MODEL:(t + 1) * D_MODEL] = out[t * nb:(t + 1) * nb, :]


def _rglru(x, g, w_in, conv_w, conv_b, wax, b_a, b_x, lam, w_out):
    b, s, d = x.shape
    assert b == 8, "one f32 sublane group per time step"
    m = RNN_TS * b
    const2 = lambda i: (0, 0)
    out = pl.pallas_call(
        _rglru_body,
        grid=(s // RNN_TS,),
        in_specs=[
            pl.BlockSpec((b, RNN_TS * d), lambda i: (0, i)),
            pl.BlockSpec((1, d), const2),
            pl.BlockSpec((d, 2 * D_RNN), const2, pipeline_mode=pl.Buffered(1)),
            pl.BlockSpec((CONV_WIDTH, D_RNN), const2),
            pl.BlockSpec((1, D_RNN), const2),
            pl.BlockSpec((N_RNN_BLOCKS, RNN_BLOCK, 2 * RNN_BLOCK), lambda i: (0, 0, 0)),
            pl.BlockSpec((1, D_RNN), const2),
            pl.BlockSpec((1, D_RNN), const2),
            pl.BlockSpec((1, D_RNN), const2),
            pl.BlockSpec((D_RNN, d), const2, pipeline_mode=pl.Buffered(1)),
        ],
        out_specs=pl.BlockSpec((b, RNN_TS * d), lambda i: (0, i)),
        out_shape=jax.ShapeDtypeStruct((b, s * d), F32),
        scratch_shapes=[
            pltpu.VMEM(((CONV_WIDTH - 1) * b + m, D_RNN), F32),
            pltpu.VMEM((m, D_RNN), F32),
            pltpu.VMEM((m, D_RNN), F32),
            pltpu.VMEM((b, D_RNN), F32),
        ],
        compiler_params=pltpu.CompilerParams(
            dimension_semantics=("arbitrary",), vmem_limit_bytes=VMEM_LIMIT_BYTES),
        name="rglru",
    )(x.reshape(b, s * d), g.reshape(1, d), w_in, conv_w, conv_b.reshape(1, D_RNN), wax,
      b_a.reshape(1, D_RNN), b_x.reshape(1, D_RNN), lam.reshape(1, D_RNN), w_out)
    return out.reshape(b, s, d)


def _t5_bucket(n):
    max_exact = N_BUCKETS // 2
    nf = np.maximum(n, 1).astype(np.float64)
    large = max_exact + (np.log(nf / max_exact) / math.log(MAX_DISTANCE / max_exact)
                         * (N_BUCKETS - max_exact)).astype(np.int64)
    large = np.minimum(large, N_BUCKETS - 1)
    return np.where(n < max_exact, n, large)


def _bias_codes():
    qi = np.arange(BLK)
    q_off = 16 * (qi % 8) + qi // 8
    kj = np.arange(2 * BLK)
    k_off = 16 * (kj % 16 - 8) + kj // 16
    dist = q_off[:, None] - k_off[None, :]
    band = (dist >= 0) & (dist <= BLK)
    code = np.where(band, _t5_bucket(np.maximum(dist, 0) * 1), -1)
    c0 = np.stack([np.where(k_off[None, :] >= 0, code, -1), code])
    q_off = 4 * (qi % 32) + qi // 32
    k_off = 4 * (kj % 64 - 32) + kj // 64
    dist = q_off[:, None] - k_off[None, :]
    band = (dist >= 0) & (dist <= BLK)
    code = np.where(band, _t5_bucket(np.maximum(dist, 0) * 4), -1)
    c1 = np.stack([np.where(k_off[None, :] >= 0, code, -1), code])
    dist = qi[:, None] - qi[None, :]
    c2 = np.where(dist >= 0, _t5_bucket(np.maximum(dist, 0) * 16), -1)
    return c0.astype(np.int32), c1.astype(np.int32), c2.astype(np.int32)


def _bias_body(tbl_ref, c0_ref, c1_ref, c2_ref, b0_ref, b1_ref, b2_ref):
    h = pl.program_id(0)

    def tile(code, col):
        acc = jnp.full(code.shape, NEG, F32)
        for b in range(N_BUCKETS):
            acc = jnp.where(code == b, tbl_ref[b, col], acc)
        return acc

    for v in range(2):
        b0_ref[v, 0] = tile(c0_ref[v], h)
        b1_ref[v, 0] = tile(c1_ref[v], N_HEADS + h)
    b2_ref[0] = tile(c2_ref[...], 2 * N_HEADS + h)


def _bias_tiles(rel_bias):
    c0, c1, c2 = _bias_codes()
    full3 = lambda h: (0, 0, 0)
    return pl.pallas_call(
        _bias_body,
        grid=(N_HEADS,),
        in_specs=[
            pl.BlockSpec(memory_space=pltpu.SMEM),
            pl.BlockSpec((2, BLK, 2 * BLK), full3),
            pl.BlockSpec((2, BLK, 2 * BLK), full3),
            pl.BlockSpec((BLK, BLK), lambda h: (0, 0)),
        ],
        out_specs=[
            pl.BlockSpec((2, 1, BLK, 2 * BLK), lambda h: (0, h, 0, 0)),
            pl.BlockSpec((2, 1, BLK, 2 * BLK), lambda h: (0, h, 0, 0)),
            pl.BlockSpec((1, BLK, BLK), lambda h: (h, 0, 0)),
        ],
        out_shape=[
            jax.ShapeDtypeStruct((2, N_HEADS, BLK, 2 * BLK), F32),
            jax.ShapeDtypeStruct((2, N_HEADS, BLK, 2 * BLK), F32),
            jax.ShapeDtypeStruct((N_HEADS, BLK, BLK), F32),
        ],
        compiler_params=pltpu.CompilerParams(dimension_semantics=("arbitrary",)),
        name="attn_bias",
    )(rel_bias, jnp.asarray(c0), jnp.asarray(c1), jnp.asarray(c2))


def _attn_body(xn_ref, w_ref, gq_ref, bd_ref, b0_ref, b1_ref, b2_ref, o_ref,
               qs, ks, vs, m_st, l_st, acc_st):
    seq = o_ref.shape[2]
    n_chunk = seq // 512
    zpad = jnp.zeros((PAD, QW), F32)
    qs[0:PAD, :] = zpad
    ks[0:PAD, :] = zpad
    vs[0:PAD, :] = zpad
    is_a = lax.broadcasted_iota(jnp.int32, (BLK, 128), 1) < HEAD_DIM

    def project(g):
        wg = w_ref[0, :, g * 3 * QW:(g + 1) * 3 * QW]
        for c in range(n_chunk):
            xc = jnp.concatenate(
                [xn_ref[0, :, (4 * c + j) * D_MODEL:(4 * c + j + 1) * D_MODEL] for j in range(4)], axis=0)
            y = _dot(xc, wg)
            q, k, v = y[:, :QW], y[:, QW:2 * QW], y[:, 2 * QW:]
            ssq = _dot((q * q).astype(BF16), bd_ref[...])
            ssk = _dot((k * k).astype(BF16), bd_ref[...])
            rows = slice(PAD + 512 * c, PAD + 512 * (c + 1))
            qs[rows, :] = q * lax.rsqrt(ssq * (1.0 / HEAD_DIM) + RMS_EPS) * gq_ref[...]
            ks[rows, :] = k * lax.rsqrt(ssk * (1.0 / HEAD_DIM) + RMS_EPS)
            vs[rows, :] = v

    def gather(ref, starts, n, pad=PAD):
        return jnp.concatenate([ref[pl.ds(pad + st, n), :] for st in starts], axis=0)

    def attend(q_starts, q_n, k_starts, k_n, bias, init):
        q_blk = gather(qs, q_starts, q_n).astype(BF16)
        k_blk = gather(ks, k_starts, k_n).astype(BF16)
        v_blk = gather(vs, k_starts, k_n).astype(BF16)
        for sl in range(2):
            cols = slice(128 * sl, 128 * (sl + 1))
            q = q_blk[:, cols]
            q2 = jnp.concatenate([jnp.where(is_a, q, 0), jnp.where(is_a, 0, q)], axis=0)
            s = lax.dot_general(q2, k_blk[:, cols], (((1,), (1,)), ((), ())), preferred_element_type=F32)
            s = s + jnp.concatenate([bias(2 * sl), bias(2 * sl + 1)], axis=0)
            m = jnp.max(s, axis=-1, keepdims=True)
            p = jnp.exp(s - m)
            l = jnp.sum(p, axis=-1, keepdims=True)
            o2 = _dot(p.astype(BF16), v_blk[:, cols])
            mb = jnp.where(is_a, m[:BLK], m[BLK:])
            lb = jnp.where(is_a, l[:BLK], l[BLK:])
            ob = jnp.where(is_a, o2[:BLK], o2[BLK:])
            if not init:
                mo = gather(m_st, q_starts, q_n, 0)[:, cols]
                lo = gather(l_st, q_starts, q_n, 0)[:, cols]
                ao = gather(acc_st, q_starts, q_n, 0)[:, cols]
                mn = jnp.maximum(mo, mb)
                al = jnp.exp(mo - mn)
                be = jnp.exp(mb - mn)
                lb = al * lo + be * lb
                ob = al * ao + be * ob
                mb = mn
            for i, st in enumerate(q_starts):
                piece = slice(i * q_n, (i + 1) * q_n)
                m_st[pl.ds(st, q_n), cols] = mb[piece]
                l_st[pl.ds(st, q_n), cols] = lb[piece]
                acc_st[pl.ds(st, q_n), cols] = ob[piece]

    project(2)

    def g2_block(r, carry):
        st = pl.multiple_of(r * BLK, BLK)
        attend([st], BLK, [st], BLK, lambda h: b2_ref[h], init=True)
        return carry

    lax.fori_loop(0, seq // BLK, g2_block, 0)

    project(1)

    def g1_block(i, carry):
        r4 = i // 4
        n = i % 4
        var = jnp.minimum(n, 1)
        base = pl.multiple_of(r4 * BLK + n * 32, 32)
        q_starts = [base + 4 * j * BLK for j in range(4)]
        k_starts = [base + 4 * j * BLK - 32 for j in range(4)]
        attend(q_starts, 32, k_starts, 64, lambda h: b1_ref[var, h], init=False)
        return carry

    lax.fori_loop(0, seq // BLK, g1_block, 0)

    project(0)

    def g0_block(n, carry):
        var = jnp.minimum(n, 1)
        base = pl.multiple_of(n * 8, 8)
        q_starts = [base + r * BLK for r in range(16)]
        k_starts = [base + r * BLK - 8 for r in range(16)]
        attend(q_starts, 8, k_starts, 16, lambda h: b0_ref[var, h], init=False)
        return carry

    lax.fori_loop(0, seq // BLK, g0_block, 0)

    o_ref[0, 0] = (acc_st[...] / l_st[...]).astype(BF16)


def _attention(xn, w4, gq, b0, b1, b2):
    b, s, d = xn.shape
    assert s == 16 * BLK, "group 2 must be one block per residue"
    nq = N_HEADS // QUAD
    bd = np.kron(np.eye(QUAD), np.ones((HEAD_DIM, HEAD_DIM))).astype(np.float32)
    return pl.pallas_call(
        _attn_body,
        grid=(b, nq),
        in_specs=[
            pl.BlockSpec((1, BLK, 16 * d), lambda i, j: (i, 0, 0)),
            pl.BlockSpec((1, d, N_GROUPS * 3 * QW), lambda i, j: (j, 0, 0)),
            pl.BlockSpec((1, QW), lambda i, j: (0, 0)),
            pl.BlockSpec((QW, QW), lambda i, j: (0, 0)),
            pl.BlockSpec((2, QUAD, BLK, 2 * BLK), lambda i, j: (0, j, 0, 0)),
            pl.BlockSpec((2, QUAD, BLK, 2 * BLK), lambda i, j: (0, j, 0, 0)),
            pl.BlockSpec((QUAD, BLK, BLK), lambda i, j: (j, 0, 0)),
        ],
        out_specs=pl.BlockSpec((1, 1, s, QW), lambda i, j: (i, j, 0, 0)),
        out_shape=jax.ShapeDtypeStruct((b, nq, s, QW), BF16),
        scratch_shapes=[pltpu.VMEM((PAD + s, QW), F32)] * 3 + [pltpu.VMEM((s, QW), F32)] * 3,
        compiler_params=pltpu.CompilerParams(
            dimension_semantics=("arbitrary", "arbitrary"), vmem_limit_bytes=VMEM_LIMIT_BYTES),
        name="dilated_attn",
    )(xn.reshape(b, BLK, 16 * d), w4, gq, jnp.asarray(bd, BF16), b0, b1, b2)


def _wo_body(x_ref, o_ref, wo_ref, out_ref):
    acc = _dot(o_ref[0, 0], wo_ref[0:QW, :])
    for q in range(1, N_HEADS // QUAD):
        acc = acc + _dot(o_ref[0, q], wo_ref[q * QW:(q + 1) * QW, :])
    for j in range(4):
        cols = slice(j * D_MODEL, (j + 1) * D_MODEL)
        out_ref[0, :, cols] = x_ref[0, :, cols] + acc[j * BLK:(j + 1) * BLK, :]


def _attn_out(x, o4, wo):
    b, s, d = x.shape
    nq = N_HEADS // QUAD
    out = pl.pallas_call(
        _wo_body,
        grid=(b, 4),
        in_specs=[
            pl.BlockSpec((1, BLK, 4 * d), lambda i, c: (i, 0, c)),
            pl.BlockSpec((1, nq, 4 * BLK, QW), lambda i, c: (i, 0, c, 0)),
            pl.BlockSpec((d, d), lambda i, c: (0, 0)),
        ],
        out_specs=pl.BlockSpec((1, BLK, 4 * d), lambda i, c: (i, 0, c)),
        out_shape=jax.ShapeDtypeStruct((b, BLK, 16 * d), F32),
        compiler_params=pltpu.CompilerParams(
            dimension_semantics=("arbitrary", "arbitrary"), vmem_limit_bytes=VMEM_LIMIT_BYTES),
        name="attn_out",
    )(x.reshape(b, BLK, 16 * d), o4, wo)
    return out.reshape(b, s, d)


def kernel(x, norm_g, ffn_w_in, ffn_w_out, rnn_w_in, rnn_conv_w, rnn_conv_b, rnn_w_a, rnn_b_a, rnn_w_x,
           rnn_b_x, rnn_lambda, rnn_w_out, att_w_qkv, att_q_gain, att_k_gain, att_w_o, rel_bias):
    b, s, d = x.shape
    w_in = ffn_w_in.astype(BF16)
    w_out = ffn_w_out.astype(BF16)

    h = _ffn(x.reshape(b * s, d), norm_g[0, 0], w_in[0, 0], w_out[0, 0])
    wax = jnp.concatenate([rnn_w_a[0], rnn_w_x[0]], axis=-1).astype(BF16)
    h = _rglru(h.reshape(b, s, d), norm_g[0, 1], rnn_w_in[0].astype(BF16), rnn_conv_w[0], rnn_conv_b[0], wax,
               rnn_b_a[0], rnn_b_x[0], rnn_lambda[0], rnn_w_out[0].astype(BF16))
    h = _ffn(h.reshape(b * s, d), norm_g[0, 2], w_in[0, 1], w_out[0, 1])

    h, hn = _ffn(h, norm_g[1, 0], w_in[1, 0], w_out[1, 0], g_next=norm_g[1, 1])
    nq = N_HEADS // QUAD
    w4 = att_w_qkv[0].reshape(d, 3, N_GROUPS, nq, QW).transpose(3, 0, 2, 1, 4).reshape(nq, d, N_GROUPS * 3 * QW)
    gq = jnp.tile(att_q_gain[0] * att_k_gain[0] * HEAD_DIM ** -0.5, QUAD).reshape(1, QW)
    b0, b1, b2 = _bias_tiles(rel_bias)
    o4 = _attention(hn.reshape(b, s, d), w4.astype(BF16), gq, b0, b1, b2)
    h = _attn_out(h.reshape(b, s, d), o4, att_w_o[0].astype(BF16))
    h = _ffn(h.reshape(b * s, d), norm_g[1, 2], w_in[1, 1], w_out[1, 1])
    return h.reshape(b, s, d)
```

```python
import math

import numpy as np
import jax
import jax.numpy as jnp
from jax import lax
from jax.experimental import pallas as pl
from jax.experimental.pallas import tpu as pltpu

F32 = jnp.float32
BF16 = jnp.bfloat16

D_MODEL = 1024
RMS_EPS = 1e-6
D_FF = 2816
FFN_RES = 0.5
D_RNN = 1280
N_RNN_BLOCKS = 10
RNN_BLOCK = 128
CONV_WIDTH = 4
LRU_C = 8.0
HEAD_DIM = 64
N_HEADS = 16
DILATION_GROUPS = ((128, 1), (512, 4), (2048, 16))
N_GROUPS = 3
N_BUCKETS = 32
MAX_DISTANCE = 2048
BLK = 128

VMEM_LIMIT_BYTES = 56 * 1024 * 1024
FFN_TM = 512
FFN_CHUNK = 256
RNN_TS = 64
NEG = -1e30
PAD = 32
QUAD = 4
QW = QUAD * HEAD_DIM


def _rms(x, g):
    return x * lax.rsqrt(jnp.mean(x * x, axis=-1, keepdims=True) + RMS_EPS) * g


def _dot(a, b):
    return jnp.dot(a, b, preferred_element_type=F32)


def _ffn_body(*refs, emit_norm, attn_in):
    refs = list(refs)
    x_ref = refs.pop(0)
    if attn_in:
        ao_ref, wo_ref = refs.pop(0), refs.pop(0)
    g_ref, win_ref, wout_ref = refs.pop(0), refs.pop(0), refs.pop(0)
    if emit_norm:
        g2_ref, o_ref, xn_ref, a_scr = refs
    else:
        o_ref, a_scr = refs
    x = x_ref[...]
    if attn_in:
        x = x + _dot(ao_ref[...], wo_ref[...])
    xn = _rms(x, g_ref[...]).astype(BF16)
    for c in range(D_FF // FFN_CHUNK):
        lo = c * FFN_CHUNK
        gate = _dot(xn, win_ref[:, lo:lo + FFN_CHUNK])
        up = _dot(xn, win_ref[:, D_FF + lo:D_FF + lo + FFN_CHUNK])
        a_scr[:, lo:lo + FFN_CHUNK] = (gate * jax.nn.sigmoid(gate) * up).astype(BF16)
    y = x + FFN_RES * _dot(a_scr[...], wout_ref[...])
    o_ref[...] = y
    if emit_norm:
        yn = _rms(y, g2_ref[...]).reshape(FFN_TM // 16, 16, D_MODEL)
        xn_ref[0] = pltpu.einshape("lrd->rld", yn).astype(BF16)


def _ffn(x2d, g, w_in, w_out, sel, g_next=None, attn=None, seq=None):
    m = x2d.shape[0]
    emit_norm = g_next is not None
    attn_in = attn is not None
    const = lambda i: (0, 0)
    row = lambda i: (i, 0)
    wsel = lambda i: (sel[0], sel[1], 0, 0)
    in_specs = [pl.BlockSpec((FFN_TM, D_MODEL), row)]
    args = [x2d]
    if attn_in:
        in_specs += [pl.BlockSpec((FFN_TM, D_MODEL), row),
                     pl.BlockSpec((D_MODEL, D_MODEL), const, pipeline_mode=pl.Buffered(1))]
        args += list(attn)
    in_specs += [
        pl.BlockSpec((1, D_MODEL), const),
        pl.BlockSpec((None, None, D_MODEL, 2 * D_FF), wsel, pipeline_mode=pl.Buffered(1)),
        pl.BlockSpec((None, None, D_FF, D_MODEL), wsel, pipeline_mode=pl.Buffered(1)),
    ]
    args += [g.reshape(1, D_MODEL), w_in, w_out]
    out_shape = [jax.ShapeDtypeStruct((m, D_MODEL), F32)]
    out_specs = [pl.BlockSpec((FFN_TM, D_MODEL), row)]
    if emit_norm:
        tiles = seq // FFN_TM
        in_specs.append(pl.BlockSpec((1, D_MODEL), const))
        args.append(g_next.reshape(1, D_MODEL))
        out_shape.append(jax.ShapeDtypeStruct((m // seq, 16, seq // 16, D_MODEL), BF16))
        out_specs.append(pl.BlockSpec((1, 16, FFN_TM // 16, D_MODEL), lambda i: (i // tiles, 0, i % tiles, 0)))
    res = pl.pallas_call(
        lambda *refs: _ffn_body(*refs, emit_norm=emit_norm, attn_in=attn_in),
        grid=(m // FFN_TM,),
        in_specs=in_specs,
        out_specs=out_specs,
        out_shape=out_shape,
        scratch_shapes=[pltpu.VMEM((FFN_TM, D_FF), BF16)],
        compiler_params=pltpu.CompilerParams(
            dimension_semantics=("arbitrary",), vmem_limit_bytes=VMEM_LIMIT_BYTES),
        name="ffn_norm" if emit_norm else ("ffn_attn" if attn_in else "ffn"),
    )(*args)
    return res if emit_norm else res[0]


def _rglru_body(x_ref, g_ref, win_ref, cw_ref, cb_ref, wax_ref, ba_ref, bx_ref, lam_ref, wout_ref,
                o_ref, u_scr, a_scr, b_scr, h_scr):
    nb = x_ref.shape[0]
    m = RNN_TS * nb
    hist = (CONV_WIDTH - 1) * nb

    @pl.when(pl.program_id(0) == 0)
    def _():
        u_scr[0:hist, :] = jnp.zeros((hist, D_RNN), F32)
        h_scr[...] = jnp.zeros_like(h_scr)

    x = pltpu.einshape("btd->tbd", x_ref[...]).reshape(m, D_MODEL)
    xn = _rms(x, g_ref[...]).astype(BF16)
    gu = _dot(xn, win_ref[...])
    gate = gu[:, :D_RNN]
    u_scr[hist:hist + m, :] = gu[:, D_RNN:]
    conv = cb_ref[...]
    for k in range(CONV_WIDTH):
        conv = conv + u_scr[k * nb:k * nb + m, :] * cw_ref[k:k + 1, :]
    u_scr[0:hist, :] = u_scr[m:m + hist, :]

    sp = jax.nn.softplus(-lam_ref[...])
    for n in range(N_RNN_BLOCKS):
        sl = slice(n * RNN_BLOCK, (n + 1) * RNN_BLOCK)
        cn = conv[:, sl]
        ra = _dot(cn.astype(BF16), wax_ref[n])
        r = jax.nn.sigmoid(ra[:, :RNN_BLOCK] + ba_ref[:, sl])
        gi = jax.nn.sigmoid(ra[:, RNN_BLOCK:] + bx_ref[:, sl])
        log_a = -LRU_C * r * sp[:, sl]
        a = jnp.exp(log_a)
        a_scr[:, sl] = a
        b_scr[:, sl] = jnp.sqrt(-jnp.tanh(log_a) * (1.0 + a * a)) * (gi * cn)

    def step(t, h):
        r0 = pl.multiple_of(t * nb, nb)
        h = a_scr[pl.ds(r0, nb), :] * h + b_scr[pl.ds(r0, nb), :]
        b_scr[pl.ds(r0, nb), :] = h
        return h

    h_scr[...] = lax.fori_loop(0, RNN_TS, step, h_scr[...], unroll=8)
    y = (b_scr[...] * jax.nn.gelu(gate)).astype(BF16)
    delta = _dot(y, wout_ref[...]).reshape(RNN_TS, nb, D_MODEL)
    o_ref[...] = x_ref[...] + pltpu.einshape("tbd->btd", delta)


def _rglru(x, g, w_in, conv_w, conv_b, wax, b_a, b_x, lam, w_out):
    b, s, d = x.shape
    assert b == 8, "one f32 sublane group per time step"
    m = RNN_TS * b
    const2 = lambda i: (0, 0)
    out = pl.pallas_call(
        _rglru_body,
        grid=(s // RNN_TS,),
        in_specs=[
            pl.BlockSpec((b, RNN_TS, d), lambda i: (0, i, 0)),
            pl.BlockSpec((1, d), const2),
            pl.BlockSpec((d, 2 * D_RNN), const2, pipeline_mode=pl.Buffered(1)),
            pl.BlockSpec((CONV_WIDTH, D_RNN), const2),
            pl.BlockSpec((1, D_RNN), const2),
            pl.BlockSpec((N_RNN_BLOCKS, RNN_BLOCK, 2 * RNN_BLOCK), lambda i: (0, 0, 0)),
            pl.BlockSpec((1, D_RNN), const2),
            pl.BlockSpec((1, D_RNN), const2),
            pl.BlockSpec((1, D_RNN), const2),
            pl.BlockSpec((D_RNN, d), const2, pipeline_mode=pl.Buffered(1)),
        ],
        out_specs=pl.BlockSpec((b, RNN_TS, d), lambda i: (0, i, 0)),
        out_shape=jax.ShapeDtypeStruct((b, s, d), F32),
        scratch_shapes=[
            pltpu.VMEM(((CONV_WIDTH - 1) * b + m, D_RNN), F32),
            pltpu.VMEM((m, D_RNN), F32),
            pltpu.VMEM((m, D_RNN), F32),
            pltpu.VMEM((b, D_RNN), F32),
        ],
        compiler_params=pltpu.CompilerParams(
            dimension_semantics=("arbitrary",), vmem_limit_bytes=VMEM_LIMIT_BYTES),
        name="rglru",
    )(x, g.reshape(1, d), w_in, conv_w, conv_b.reshape(1, D_RNN), wax,
      b_a.reshape(1, D_RNN), b_x.reshape(1, D_RNN), lam.reshape(1, D_RNN), w_out)
    return out


def _t5_bucket(n):
    max_exact = N_BUCKETS // 2
    nf = np.maximum(n, 1).astype(np.float64)
    large = max_exact + (np.log(nf / max_exact) / math.log(MAX_DISTANCE / max_exact)
                         * (N_BUCKETS - max_exact)).astype(np.int64)
    large = np.minimum(large, N_BUCKETS - 1)
    return np.where(n < max_exact, n, large)


def _bias_codes():
    qi = np.arange(BLK)
    q_off = 16 * (qi % 8) + qi // 8
    kj = np.arange(2 * BLK)
    k_off = 16 * (kj % 16 - 8) + kj // 16
    dist = q_off[:, None] - k_off[None, :]
    band = (dist >= 0) & (dist <= BLK)
    code = np.where(band, _t5_bucket(np.maximum(dist, 0) * 1), -1)
    c0 = np.stack([np.where(k_off[None, :] >= 0, code, -1), code])
    q_off = 4 * (qi % 32) + qi // 32
    k_off = 4 * (kj % 64 - 32) + kj // 64
    dist = q_off[:, None] - k_off[None, :]
    band = (dist >= 0) & (dist <= BLK)
    code = np.where(band, _t5_bucket(np.maximum(dist, 0) * 4), -1)
    c1 = np.stack([np.where(k_off[None, :] >= 0, code, -1), code])
    dist = qi[:, None] - qi[None, :]
    c2 = np.where(dist >= 0, _t5_bucket(np.maximum(dist, 0) * 16), -1)
    return c0.astype(np.int32), c1.astype(np.int32), c2.astype(np.int32)


def _bias_body(tbl_ref, c0_ref, c1_ref, c2_ref, b0_ref, b1_ref, b2_ref):
    h = pl.program_id(0)

    def tile(code, col):
        acc = jnp.full(code.shape, NEG, F32)
        for b in range(N_BUCKETS):
            acc = jnp.where(code == b, tbl_ref[b, col], acc)
        return acc

    for v in range(2):
        b0_ref[v, 0] = tile(c0_ref[v], h)
        b1_ref[v, 0] = tile(c1_ref[v], N_HEADS + h)
    b2_ref[0] = tile(c2_ref[...], 2 * N_HEADS + h)


def _bias_tiles(rel_bias):
    c0, c1, c2 = _bias_codes()
    full3 = lambda h: (0, 0, 0)
    return pl.pallas_call(
        _bias_body,
        grid=(N_HEADS,),
        in_specs=[
            pl.BlockSpec(memory_space=pltpu.SMEM),
            pl.BlockSpec((2, BLK, 2 * BLK), full3),
            pl.BlockSpec((2, BLK, 2 * BLK), full3),
            pl.BlockSpec((BLK, BLK), lambda h: (0, 0)),
        ],
        out_specs=[
            pl.BlockSpec((2, 1, BLK, 2 * BLK), lambda h: (0, h, 0, 0)),
            pl.BlockSpec((2, 1, BLK, 2 * BLK), lambda h: (0, h, 0, 0)),
            pl.BlockSpec((1, BLK, BLK), lambda h: (h, 0, 0)),
        ],
        out_shape=[
            jax.ShapeDtypeStruct((2, N_HEADS, BLK, 2 * BLK), F32),
            jax.ShapeDtypeStruct((2, N_HEADS, BLK, 2 * BLK), F32),
            jax.ShapeDtypeStruct((N_HEADS, BLK, BLK), F32),
        ],
        compiler_params=pltpu.CompilerParams(dimension_semantics=("arbitrary",)),
        name="attn_bias",
    )(rel_bias, jnp.asarray(c0), jnp.asarray(c1), jnp.asarray(c2))


def _attn_body(xn_ref, w_ref, gq_ref, bd_ref, b0_ref, b1_ref, b2_ref, o_ref,
               qs, ks, vs, m_st, l_st, acc_st):
    seq = o_ref.shape[1]
    n_chunk = seq // 512
    zpad = jnp.zeros((PAD, QW), F32)
    qs[0:PAD, :] = zpad
    ks[0:PAD, :] = zpad
    vs[0:PAD, :] = zpad
    is_a = lax.broadcasted_iota(jnp.int32, (BLK, 128), 1) < HEAD_DIM

    def project(g):
        wg = w_ref[0, :, g * 3 * QW:(g + 1) * 3 * QW]
        for c in range(n_chunk):
            y = _dot(xn_ref[0, 512 * c:512 * (c + 1), :], wg)
            q, k, v = y[:, :QW], y[:, QW:2 * QW], y[:, 2 * QW:]
            ssq = _dot((q * q).astype(BF16), bd_ref[...])
            ssk = _dot((k * k).astype(BF16), bd_ref[...])
            rows = slice(PAD + 512 * c, PAD + 512 * (c + 1))
            qs[rows, :] = q * lax.rsqrt(ssq * (1.0 / HEAD_DIM) + RMS_EPS) * gq_ref[...]
            ks[rows, :] = k * lax.rsqrt(ssk * (1.0 / HEAD_DIM) + RMS_EPS)
            vs[rows, :] = v

    def gather(ref, starts, n, pad=PAD):
        return jnp.concatenate([ref[pl.ds(pad + st, n), :] for st in starts], axis=0)

    def attend(q_starts, q_n, k_starts, k_n, bias, init):
        q_blk = gather(qs, q_starts, q_n).astype(BF16)
        k_blk = gather(ks, k_starts, k_n).astype(BF16)
        v_blk = gather(vs, k_starts, k_n).astype(BF16)
        for sl in range(2):
            cols = slice(128 * sl, 128 * (sl + 1))
            q = q_blk[:, cols]
            q2 = jnp.concatenate([jnp.where(is_a, q, 0), jnp.where(is_a, 0, q)], axis=0)
            s = lax.dot_general(q2, k_blk[:, cols], (((1,), (1,)), ((), ())), preferred_element_type=F32)
            s = s + jnp.concatenate([bias(2 * sl), bias(2 * sl + 1)], axis=0)
            m = jnp.max(s, axis=-1, keepdims=True)
            p = jnp.exp(s - m)
            l = jnp.sum(p, axis=-1, keepdims=True)
            o2 = _dot(p.astype(BF16), v_blk[:, cols])
            mb = jnp.where(is_a, m[:BLK], m[BLK:])
            lb = jnp.where(is_a, l[:BLK], l[BLK:])
            ob = jnp.where(is_a, o2[:BLK], o2[BLK:])
            if not init:
                mo = gather(m_st, q_starts, q_n, 0)[:, cols]
                lo = gather(l_st, q_starts, q_n, 0)[:, cols]
                ao = gather(acc_st, q_starts, q_n, 0)[:, cols]
                mn = jnp.maximum(mo, mb)
                al = jnp.exp(mo - mn)
                be = jnp.exp(mb - mn)
                lb = al * lo + be * lb
                ob = al * ao + be * ob
                mb = mn
            for i, st in enumerate(q_starts):
                piece = slice(i * q_n, (i + 1) * q_n)
                m_st[pl.ds(st, q_n), cols] = mb[piece]
                l_st[pl.ds(st, q_n), cols] = lb[piece]
                acc_st[pl.ds(st, q_n), cols] = ob[piece]

    project(2)

    def g2_block(r, carry):
        st = pl.multiple_of(r * BLK, BLK)
        attend([st], BLK, [st], BLK, lambda h: b2_ref[h], init=True)
        return carry

    lax.fori_loop(0, seq // BLK, g2_block, 0)

    project(1)

    def g1_block(i, carry):
        r4 = i // 4
        n = i % 4
        var = jnp.minimum(n, 1)
        base = pl.multiple_of(r4 * BLK + n * 32, 32)
        q_starts = [base + 4 * j * BLK for j in range(4)]
        k_starts = [base + 4 * j * BLK - 32 for j in range(4)]
        attend(q_starts, 32, k_starts, 64, lambda h: b1_ref[var, h], init=False)
        return carry

    lax.fori_loop(0, seq // BLK, g1_block, 0)

    project(0)

    def g0_block(n, carry):
        var = jnp.minimum(n, 1)
        base = pl.multiple_of(n * 8, 8)
        q_starts = [base + r * BLK for r in range(16)]
        k_starts = [base + r * BLK - 8 for r in range(16)]
        attend(q_starts, 8, k_starts, 16, lambda h: b0_ref[var, h], init=False)
        return carry

    lax.fori_loop(0, seq // BLK, g0_block, 0)

    o = (acc_st[...] / l_st[...]).reshape(16, seq // 16, QW)
    o_ref[0] = pltpu.einshape("rlc->lrc", o).reshape(seq, QW).astype(BF16)


def _attention(xn, w4, gq, b0, b1, b2):
    b, s, d = xn.shape
    assert s == 16 * BLK, "group 2 must be one block per residue"
    nq = N_HEADS // QUAD
    bd = np.kron(np.eye(QUAD), np.ones((HEAD_DIM, HEAD_DIM))).astype(np.float32)
    return pl.pallas_call(
        _attn_body,
        grid=(b, nq),
        in_specs=[
            pl.BlockSpec((1, s, d), lambda i, j: (i, 0, 0)),
            pl.BlockSpec((1, d, N_GROUPS * 3 * QW), lambda i, j: (j, 0, 0)),
            pl.BlockSpec((1, QW), lambda i, j: (0, 0)),
            pl.BlockSpec((QW, QW), lambda i, j: (0, 0)),
            pl.BlockSpec((2, QUAD, BLK, 2 * BLK), lambda i, j: (0, j, 0, 0)),
            pl.BlockSpec((2, QUAD, BLK, 2 * BLK), lambda i, j: (0, j, 0, 0)),
            pl.BlockSpec((QUAD, BLK, BLK), lambda i, j: (j, 0, 0)),
        ],
        out_specs=pl.BlockSpec((1, s, QW), lambda i, j: (i, 0, j)),
        out_shape=jax.ShapeDtypeStruct((b, s, d), BF16),
        scratch_shapes=[pltpu.VMEM((PAD + s, QW), F32)] * 3 + [pltpu.VMEM((s, QW), F32)] * 3,
        compiler_params=pltpu.CompilerParams(
            dimension_semantics=("arbitrary", "arbitrary"), vmem_limit_bytes=VMEM_LIMIT_BYTES),
        name="dilated_attn",
    )(xn, w4, gq, jnp.asarray(bd, BF16), b0, b1, b2)


def kernel(x, norm_g, ffn_w_in, ffn_w_out, rnn_w_in, rnn_conv_w, rnn_conv_b, rnn_w_a, rnn_b_a, rnn_w_x,
           rnn_b_x, rnn_lambda, rnn_w_out, att_w_qkv, att_q_gain, att_k_gain, att_w_o, rel_bias):
    b, s, d = x.shape
    w_in = ffn_w_in.astype(BF16)
    w_out = ffn_w_out.astype(BF16)

    h = _ffn(x.reshape(b * s, d), norm_g[0, 0], w_in, w_out, (0, 0))
    wax = jnp.concatenate([rnn_w_a[0], rnn_w_x[0]], axis=-1).astype(BF16)
    h = _rglru(h.reshape(b, s, d), norm_g[0, 1], rnn_w_in[0].astype(BF16), rnn_conv_w[0], rnn_conv_b[0], wax,
               rnn_b_a[0], rnn_b_x[0], rnn_lambda[0], rnn_w_out[0].astype(BF16))
    h = _ffn(h.reshape(b * s, d), norm_g[0, 2], w_in, w_out, (0, 1))

    h, hn = _ffn(h, norm_g[1, 0], w_in, w_out, (1, 0), g_next=norm_g[1, 1], seq=s)
    nq = N_HEADS // QUAD
    w4 = att_w_qkv[0].reshape(d, 3, N_GROUPS, nq, QW).transpose(3, 0, 2, 1, 4).reshape(nq, d, N_GROUPS * 3 * QW)
    gq = jnp.tile(att_q_gain[0] * att_k_gain[0] * HEAD_DIM ** -0.5, QUAD).reshape(1, QW)
    b0, b1, b2 = _bias_tiles(rel_bias)
    o = _attention(hn.reshape(b, s, d), w4.astype(BF16), gq, b0, b1, b2)
    h = _ffn(h, norm_g[1, 2], w_in, w_out, (1, 1), attn=(o.reshape(b * s, d), att_w_o[0].astype(BF16)))
    return h.reshape(b, s, d)
```

```python
import math

import numpy as np
import jax
import jax.numpy as jnp
from jax import lax
from jax.experimental import pallas as pl
from jax.experimental.pallas import tpu as pltpu

F32 = jnp.float32
BF16 = jnp.bfloat16

D_MODEL = 1024
RMS_EPS = 1e-6
D_FF = 2816
FFN_RES = 0.5
D_RNN = 1280
N_RNN_BLOCKS = 10
RNN_BLOCK = 128
CONV_WIDTH = 4
LRU_C = 8.0
HEAD_DIM = 64
N_HEADS = 16
DILATION_GROUPS = ((128, 1), (512, 4), (2048, 16))
N_GROUPS = 3
N_BUCKETS = 32
MAX_DISTANCE = 2048
BLK = 128

VMEM_LIMIT_BYTES = 56 * 1024 * 1024
FFN_TM = 512
FFN_CHUNK = 256
RNN_TS = 64
NEG = -1e30
PAD = 32
QUAD = 4
QW = QUAD * HEAD_DIM


def _rms(x, g):
    return x * lax.rsqrt(jnp.mean(x * x, axis=-1, keepdims=True) + RMS_EPS) * g


def _dot(a, b):
    return jnp.dot(a, b, preferred_element_type=F32)


def _ffn_body(*refs, emit_norm, attn_in):
    refs = list(refs)
    x_ref = refs.pop(0)
    if attn_in:
        ao_ref, wo_ref = refs.pop(0), refs.pop(0)
    g_ref, win_ref, wout_ref = refs.pop(0), refs.pop(0), refs.pop(0)
    if emit_norm:
        g2_ref, o_ref, xn_ref, a_scr = refs
    else:
        o_ref, a_scr = refs
    x = x_ref[...]
    if attn_in:
        x = x + _dot(ao_ref[...], wo_ref[...])
    xn = _rms(x, g_ref[...]).astype(BF16)
    for c in range(D_FF // FFN_CHUNK):
        lo = c * FFN_CHUNK
        gate = _dot(xn, win_ref[:, lo:lo + FFN_CHUNK])
        up = _dot(xn, win_ref[:, D_FF + lo:D_FF + lo + FFN_CHUNK])
        a_scr[:, lo:lo + FFN_CHUNK] = (gate * jax.nn.sigmoid(gate) * up).astype(BF16)
    y = x + FFN_RES * _dot(a_scr[...], wout_ref[...])
    o_ref[...] = y
    if emit_norm:
        yn = _rms(y, g2_ref[...]).reshape(FFN_TM // 16, 16, D_MODEL)
        xn_ref[0] = pltpu.einshape("lrd->rld", yn).astype(BF16)


def _ffn(x2d, g, w_in, w_out, sel, g_next=None, attn=None, seq=None):
    m = x2d.shape[0]
    emit_norm = g_next is not None
    attn_in = attn is not None
    const = lambda i: (0, 0)
    row = lambda i: (i, 0)
    wsel = lambda i: (sel[0], sel[1], 0, 0)
    in_specs = [pl.BlockSpec((FFN_TM, D_MODEL), row)]
    args = [x2d]
    if attn_in:
        in_specs += [pl.BlockSpec((FFN_TM, D_MODEL), row),
                     pl.BlockSpec((D_MODEL, D_MODEL), const, pipeline_mode=pl.Buffered(1))]
        args += list(attn)
    in_specs += [
        pl.BlockSpec((1, D_MODEL), const),
        pl.BlockSpec((None, None, D_MODEL, 2 * D_FF), wsel, pipeline_mode=pl.Buffered(1)),
        pl.BlockSpec((None, None, D_FF, D_MODEL), wsel, pipeline_mode=pl.Buffered(1)),
    ]
    args += [g.reshape(1, D_MODEL), w_in, w_out]
    out_shape = [jax.ShapeDtypeStruct((m, D_MODEL), F32)]
    out_specs = [pl.BlockSpec((FFN_TM, D_MODEL), row)]
    if emit_norm:
        tiles = seq // FFN_TM
        in_specs.append(pl.BlockSpec((1, D_MODEL), const))
        args.append(g_next.reshape(1, D_MODEL))
        out_shape.append(jax.ShapeDtypeStruct((m // seq, 16, seq // 16, D_MODEL), BF16))
        out_specs.append(pl.BlockSpec((1, 16, FFN_TM // 16, D_MODEL), lambda i: (i // tiles, 0, i % tiles, 0)))
    res = pl.pallas_call(
        lambda *refs: _ffn_body(*refs, emit_norm=emit_norm, attn_in=attn_in),
        grid=(m // FFN_TM,),
        in_specs=in_specs,
        out_specs=out_specs,
        out_shape=out_shape,
        scratch_shapes=[pltpu.VMEM((FFN_TM, D_FF), BF16)],
        compiler_params=pltpu.CompilerParams(
            dimension_semantics=("arbitrary",), vmem_limit_bytes=VMEM_LIMIT_BYTES),
        name="ffn_norm" if emit_norm else ("ffn_attn" if attn_in else "ffn"),
    )(*args)
    return res if emit_norm else res[0]


def _rglru_body(x_ref, g_ref, win_ref, cw_ref, cb_ref, wax_ref, ba_ref, bx_ref, lam_ref, wout_ref,
                o_ref, u_scr, a_scr, b_scr, h_scr):
    nb = x_ref.shape[0]
    m = RNN_TS * nb
    hist = (CONV_WIDTH - 1) * nb

    @pl.when(pl.program_id(0) == 0)
    def _():
        u_scr[0:hist, :] = jnp.zeros((hist, D_RNN), F32)
        h_scr[...] = jnp.zeros_like(h_scr)

    x = pltpu.einshape("btd->tbd", x_ref[...]).reshape(m, D_MODEL)
    xn = _rms(x, g_ref[...]).astype(BF16)
    gu = _dot(xn, win_ref[...])
    gate = gu[:, :D_RNN]
    u_scr[hist:hist + m, :] = gu[:, D_RNN:]
    conv = cb_ref[...]
    for k in range(CONV_WIDTH):
        conv = conv + u_scr[k * nb:k * nb + m, :] * cw_ref[k:k + 1, :]
    u_scr[0:hist, :] = u_scr[m:m + hist, :]

    sp = jax.nn.softplus(-lam_ref[...])
    for n in range(N_RNN_BLOCKS):
        sl = slice(n * RNN_BLOCK, (n + 1) * RNN_BLOCK)
        cn = conv[:, sl]
        ra = _dot(cn.astype(BF16), wax_ref[n])
        r = jax.nn.sigmoid(ra[:, :RNN_BLOCK] + ba_ref[:, sl])
        gi = jax.nn.sigmoid(ra[:, RNN_BLOCK:] + bx_ref[:, sl])
        log_a = -LRU_C * r * sp[:, sl]
        a = jnp.exp(log_a)
        a_scr[:, sl] = a
        b_scr[:, sl] = jnp.sqrt(-jnp.tanh(log_a) * (1.0 + a * a)) * (gi * cn)

    def step(t, h):
        r0 = pl.multiple_of(t * nb, nb)
        h = a_scr[pl.ds(r0, nb), :] * h + b_scr[pl.ds(r0, nb), :]
        b_scr[pl.ds(r0, nb), :] = h
        return h

    h_scr[...] = lax.fori_loop(0, RNN_TS, step, h_scr[...], unroll=8)
    y = (b_scr[...] * jax.nn.gelu(gate)).astype(BF16)
    delta = _dot(y, wout_ref[...]).reshape(RNN_TS, nb, D_MODEL)
    o_ref[...] = x_ref[...] + pltpu.einshape("tbd->btd", delta)


def _rglru(x, g, w_in, conv_w, conv_b, wax, b_a, b_x, lam, w_out):
    b, s, d = x.shape
    assert b == 8, "one f32 sublane group per time step"
    m = RNN_TS * b
    const2 = lambda i: (0, 0)
    out = pl.pallas_call(
        _rglru_body,
        grid=(s // RNN_TS,),
        in_specs=[
            pl.BlockSpec((b, RNN_TS, d), lambda i: (0, i, 0)),
            pl.BlockSpec((1, d), const2),
            pl.BlockSpec((d, 2 * D_RNN), const2, pipeline_mode=pl.Buffered(1)),
            pl.BlockSpec((CONV_WIDTH, D_RNN), const2),
            pl.BlockSpec((1, D_RNN), const2),
            pl.BlockSpec((N_RNN_BLOCKS, RNN_BLOCK, 2 * RNN_BLOCK), lambda i: (0, 0, 0)),
            pl.BlockSpec((1, D_RNN), const2),
            pl.BlockSpec((1, D_RNN), const2),
            pl.BlockSpec((1, D_RNN), const2),
            pl.BlockSpec((D_RNN, d), const2, pipeline_mode=pl.Buffered(1)),
        ],
        out_specs=pl.BlockSpec((b, RNN_TS, d), lambda i: (0, i, 0)),
        out_shape=jax.ShapeDtypeStruct((b, s, d), F32),
        scratch_shapes=[
            pltpu.VMEM(((CONV_WIDTH - 1) * b + m, D_RNN), F32),
            pltpu.VMEM((m, D_RNN), F32),
            pltpu.VMEM((m, D_RNN), F32),
            pltpu.VMEM((b, D_RNN), F32),
        ],
        compiler_params=pltpu.CompilerParams(
            dimension_semantics=("arbitrary",), vmem_limit_bytes=VMEM_LIMIT_BYTES),
        name="rglru",
    )(x, g.reshape(1, d), w_in, conv_w, conv_b.reshape(1, D_RNN), wax,
      b_a.reshape(1, D_RNN), b_x.reshape(1, D_RNN), lam.reshape(1, D_RNN), w_out)
    return out


def _t5_bucket(n):
    max_exact = N_BUCKETS // 2
    nf = np.maximum(n, 1).astype(np.float64)
    large = max_exact + (np.log(nf / max_exact) / math.log(MAX_DISTANCE / max_exact)
                         * (N_BUCKETS - max_exact)).astype(np.int64)
    large = np.minimum(large, N_BUCKETS - 1)
    return np.where(n < max_exact, n, large)


def _bias_codes():
    qi = np.arange(BLK)
    q_off = 16 * (qi % 8) + qi // 8
    kj = np.arange(2 * BLK)
    k_off = 16 * (kj % 16 - 8) + kj // 16
    dist = q_off[:, None] - k_off[None, :]
    band = (dist >= 0) & (dist <= BLK)
    code = np.where(band, _t5_bucket(np.maximum(dist, 0) * 1), -1)
    c0 = np.stack([np.where(k_off[None, :] >= 0, code, -1), code])
    q_off = 4 * (qi % 32) + qi // 32
    k_off = 4 * (kj % 64 - 32) + kj // 64
    dist = q_off[:, None] - k_off[None, :]
    band = (dist >= 0) & (dist <= BLK)
    code = np.where(band, _t5_bucket(np.maximum(dist, 0) * 4), -1)
    c1 = np.stack([np.where(k_off[None, :] >= 0, code, -1), code])
    dist = qi[:, None] - qi[None, :]
    c2 = np.where(dist >= 0, _t5_bucket(np.maximum(dist, 0) * 16), -1)
    return c0.astype(np.int32), c1.astype(np.int32), c2.astype(np.int32)


def _bias_body(tbl_ref, c0_ref, c1_ref, c2_ref, b0_ref, b1_ref, b2_ref):
    h = pl.program_id(0)

    def tile(code, col):
        acc = jnp.full(code.shape, NEG, F32)
        for b in range(N_BUCKETS):
            acc = jnp.where(code == b, tbl_ref[b, col], acc)
        return acc

    for v in range(2):
        b0_ref[v, 0] = tile(c0_ref[v], h)
        b1_ref[v, 0] = tile(c1_ref[v], N_HEADS + h)
    b2_ref[0] = tile(c2_ref[...], 2 * N_HEADS + h)


def _bias_tiles(rel_bias):
    c0, c1, c2 = _bias_codes()
    full3 = lambda h: (0, 0, 0)
    return pl.pallas_call(
        _bias_body,
        grid=(N_HEADS,),
        in_specs=[
            pl.BlockSpec(memory_space=pltpu.SMEM),
            pl.BlockSpec((2, BLK, 2 * BLK), full3),
            pl.BlockSpec((2, BLK, 2 * BLK), full3),
            pl.BlockSpec((BLK, BLK), lambda h: (0, 0)),
        ],
        out_specs=[
            pl.BlockSpec((2, 1, BLK, 2 * BLK), lambda h: (0, h, 0, 0)),
            pl.BlockSpec((2, 1, BLK, 2 * BLK), lambda h: (0, h, 0, 0)),
            pl.BlockSpec((1, BLK, BLK), lambda h: (h, 0, 0)),
        ],
        out_shape=[
            jax.ShapeDtypeStruct((2, N_HEADS, BLK, 2 * BLK), F32),
            jax.ShapeDtypeStruct((2, N_HEADS, BLK, 2 * BLK), F32),
            jax.ShapeDtypeStruct((N_HEADS, BLK, BLK), F32),
        ],
        compiler_params=pltpu.CompilerParams(dimension_semantics=("arbitrary",)),
        name="attn_bias",
    )(rel_bias, jnp.asarray(c0), jnp.asarray(c1), jnp.asarray(c2))


def _attn_body(xn_ref, w_ref, gq_ref, bd_ref, b0_ref, b1_ref, b2_ref, o_ref,
               qs, ks, vs, m_st, l_st, acc_st):
    seq = o_ref.shape[1]
    n_chunk = seq // 512
    zpad = jnp.zeros((PAD, QW), F32)
    qs[0:PAD, :] = zpad
    ks[0:PAD, :] = zpad
    vs[0:PAD, :] = zpad
    is_a = lax.broadcasted_iota(jnp.int32, (BLK, 128), 1) < HEAD_DIM

    def project(g):
        wg = w_ref[0, :, g * 3 * QW:(g + 1) * 3 * QW]
        for c in range(n_chunk):
            y = _dot(xn_ref[0, 512 * c:512 * (c + 1), :], wg)
            q, k, v = y[:, :QW], y[:, QW:2 * QW], y[:, 2 * QW:]
            ssq = _dot((q * q).astype(BF16), bd_ref[...])
            ssk = _dot((k * k).astype(BF16), bd_ref[...])
            rows = slice(PAD + 512 * c, PAD + 512 * (c + 1))
            qs[rows, :] = q * lax.rsqrt(ssq * (1.0 / HEAD_DIM) + RMS_EPS) * gq_ref[...]
            ks[rows, :] = k * lax.rsqrt(ssk * (1.0 / HEAD_DIM) + RMS_EPS)
            vs[rows, :] = v

    def gather(ref, starts, n, pad=PAD):
        return jnp.concatenate([ref[pl.ds(pad + st, n), :] for st in starts], axis=0)

    def attend(q_starts, q_n, k_starts, k_n, bias, init):
        q_blk = gather(qs, q_starts, q_n).astype(BF16)
        k_blk = gather(ks, k_starts, k_n).astype(BF16)
        v_blk = gather(vs, k_starts, k_n).astype(BF16)
        for sl in range(2):
            cols = slice(128 * sl, 128 * (sl + 1))
            q = q_blk[:, cols]
            q2 = jnp.concatenate([jnp.where(is_a, q, 0), jnp.where(is_a, 0, q)], axis=0)
            s = lax.dot_general(q2, k_blk[:, cols], (((1,), (1,)), ((), ())), preferred_element_type=F32)
            s = s + jnp.concatenate([bias(2 * sl), bias(2 * sl + 1)], axis=0)
            m = jnp.max(s, axis=-1, keepdims=True)
            p = jnp.exp(s - m)
            l = jnp.sum(p, axis=-1, keepdims=True)
            o2 = _dot(p.astype(BF16), v_blk[:, cols])
            mb = jnp.where(is_a, m[:BLK], m[BLK:])
            lb = jnp.where(is_a, l[:BLK], l[BLK:])
            ob = jnp.where(is_a, o2[:BLK], o2[BLK:])
            if not init:
                mo = gather(m_st, q_starts, q_n, 0)[:, cols]
                lo = gather(l_st, q_starts, q_n, 0)[:, cols]
                ao = gather(acc_st, q_starts, q_n, 0)[:, cols]
                mn = jnp.maximum(mo, mb)
                al = jnp.exp(mo - mn)
                be = jnp.exp(mb - mn)
                lb = al * lo + be * lb
                ob = al * ao + be * ob
                mb = mn
            for i, st in enumerate(q_starts):
                piece = slice(i * q_n, (i + 1) * q_n)
                m_st[pl.ds(st, q_n), cols] = mb[piece]
                l_st[pl.ds(st, q_n), cols] = lb[piece]
                acc_st[pl.ds(st, q_n), cols] = ob[piece]

    project(2)

    for r in range(seq // BLK):
        attend([r * BLK], BLK, [r * BLK], BLK, lambda h: b2_ref[h], init=True)

    project(1)
    for r4 in range(4):
        for n in range(BLK // 32):
            base = r4 * BLK + n * 32
            q_starts = [base + 4 * j * BLK for j in range(4)]
            k_starts = [base + 4 * j * BLK - 32 for j in range(4)]
            attend(q_starts, 32, k_starts, 64, lambda h, var=min(n, 1): b1_ref[var, h], init=False)

    project(0)
    for n in range(BLK // 8):
        q_starts = [n * 8 + r * BLK for r in range(16)]
        k_starts = [n * 8 + r * BLK - 8 for r in range(16)]
        attend(q_starts, 8, k_starts, 16, lambda h, var=min(n, 1): b0_ref[var, h], init=False)

    o = (acc_st[...] / l_st[...]).reshape(16, seq // 16, QW)
    o_ref[0] = pltpu.einshape("rlc->lrc", o).reshape(seq, QW).astype(BF16)


def _attention(xn, w4, gq, b0, b1, b2):
    b, s, d = xn.shape
    assert s == 16 * BLK, "group 2 must be one block per residue"
    nq = N_HEADS // QUAD
    bd = np.kron(np.eye(QUAD), np.ones((HEAD_DIM, HEAD_DIM))).astype(np.float32)
    return pl.pallas_call(
        _attn_body,
        grid=(b, nq),
        in_specs=[
            pl.BlockSpec((1, s, d), lambda i, j: (i, 0, 0)),
            pl.BlockSpec((1, d, N_GROUPS * 3 * QW), lambda i, j: (j, 0, 0)),
            pl.BlockSpec((1, QW), lambda i, j: (0, 0)),
            pl.BlockSpec((QW, QW), lambda i, j: (0, 0)),
            pl.BlockSpec((2, QUAD, BLK, 2 * BLK), lambda i, j: (0, j, 0, 0)),
            pl.BlockSpec((2, QUAD, BLK, 2 * BLK), lambda i, j: (0, j, 0, 0)),
            pl.BlockSpec((QUAD, BLK, BLK), lambda i, j: (j, 0, 0)),
        ],
        out_specs=pl.BlockSpec((1, s, QW), lambda i, j: (i, 0, j)),
        out_shape=jax.ShapeDtypeStruct((b, s, d), BF16),
        scratch_shapes=[pltpu.VMEM((PAD + s, QW), F32)] * 3 + [pltpu.VMEM((s, QW), F32)] * 3,
        compiler_params=pltpu.CompilerParams(
            dimension_semantics=("arbitrary", "arbitrary"), vmem_limit_bytes=VMEM_LIMIT_BYTES),
        name="dilated_attn",
    )(xn, w4, gq, jnp.asarray(bd, BF16), b0, b1, b2)


def kernel(x, norm_g, ffn_w_in, ffn_w_out, rnn_w_in, rnn_conv_w, rnn_conv_b, rnn_w_a, rnn_b_a, rnn_w_x,
           rnn_b_x, rnn_lambda, rnn_w_out, att_w_qkv, att_q_gain, att_k_gain, att_w_o, rel_bias):
    b, s, d = x.shape
    w_in = ffn_w_in.astype(BF16)
    w_out = ffn_w_out.astype(BF16)

    h = _ffn(x.reshape(b * s, d), norm_g[0, 0], w_in, w_out, (0, 0))
    wax = jnp.concatenate([rnn_w_a[0], rnn_w_x[0]], axis=-1).astype(BF16)
    h = _rglru(h.reshape(b, s, d), norm_g[0, 1], rnn_w_in[0].astype(BF16), rnn_conv_w[0], rnn_conv_b[0], wax,
               rnn_b_a[0], rnn_b_x[0], rnn_lambda[0], rnn_w_out[0].astype(BF16))
    h = _ffn(h.reshape(b * s, d), norm_g[0, 2], w_in, w_out, (0, 1))

    h, hn = _ffn(h, norm_g[1, 0], w_in, w_out, (1, 0), g_next=norm_g[1, 1], seq=s)
    nq = N_HEADS // QUAD
    w4 = att_w_qkv[0].reshape(d, 3, N_GROUPS, nq, QW).transpose(3, 0, 2, 1, 4).reshape(nq, d, N_GROUPS * 3 * QW)
    gq = jnp.tile(att_q_gain[0] * att_k_gain[0] * HEAD_DIM ** -0.5, QUAD).reshape(1, QW)
    b0, b1, b2 = _bias_tiles(rel_bias)
    o = _attention(hn.reshape(b, s, d), w4.astype(BF16), gq, b0, b1, b2)
    h = _ffn(h, norm_g[1, 2], w_in, w_out, (1, 1), attn=(o.reshape(b * s, d), att_w_o[0].astype(BF16)))
    return h.reshape(b, s, d)
```

```python
import math

import numpy as np
import jax
import jax.numpy as jnp
from jax import lax
from jax.experimental import pallas as pl
from jax.experimental.pallas import tpu as pltpu

F32 = jnp.float32
BF16 = jnp.bfloat16

D_MODEL = 1024
RMS_EPS = 1e-6
D_FF = 2816
FFN_RES = 0.5
D_RNN = 1280
N_RNN_BLOCKS = 10
RNN_BLOCK = 128
CONV_WIDTH = 4
LRU_C = 8.0
HEAD_DIM = 64
N_HEADS = 16
DILATION_GROUPS = ((128, 1), (512, 4), (2048, 16))
N_GROUPS = 3
N_BUCKETS = 32
MAX_DISTANCE = 2048
BLK = 128

VMEM_LIMIT_BYTES = 56 * 1024 * 1024
FFN_TM = 512
FFN_CHUNK = 256
RNN_TS = 64
NEG = -1e30
PAD = 32
QUAD = 4
QW = QUAD * HEAD_DIM


def _rms(x, g):
    return x * lax.rsqrt(jnp.mean(x * x, axis=-1, keepdims=True) + RMS_EPS) * g


def _dot(a, b):
    return jnp.dot(a, b, preferred_element_type=F32)


def _ffn_body(*refs, pre, post):
    refs = list(refs)
    x_ref = refs.pop(0)
    if pre == "rnn":
        d_ref = refs.pop(0)
    if pre == "attn":
        ao_ref, wo_ref = refs.pop(0), refs.pop(0)
    g_ref, win_ref, wout_ref = refs.pop(0), refs.pop(0), refs.pop(0)
    if post:
        g2_ref, o_ref, xn_ref, a_scr = refs
    else:
        o_ref, a_scr = refs
    nb = x_ref.shape[0]
    x = x_ref[...].reshape(FFN_TM, D_MODEL)
    if pre == "rnn":
        x = x + pltpu.einshape("tbd->btd", d_ref[0].reshape(RNN_TS, nb, D_MODEL)).reshape(FFN_TM, D_MODEL)
    if pre == "attn":
        x = x + _dot(ao_ref[...], wo_ref[...])
    xn = _rms(x, g_ref[...]).astype(BF16)
    for c in range(D_FF // FFN_CHUNK):
        lo = c * FFN_CHUNK
        gate = _dot(xn, win_ref[:, lo:lo + FFN_CHUNK])
        up = _dot(xn, win_ref[:, D_FF + lo:D_FF + lo + FFN_CHUNK])
        a_scr[:, lo:lo + FFN_CHUNK] = (gate * jax.nn.sigmoid(gate) * up).astype(BF16)
    y = x + FFN_RES * _dot(a_scr[...], wout_ref[...])
    o_ref[...] = y.reshape(o_ref.shape)
    if post == "rnn":
        yn = _rms(y, g2_ref[...]).reshape(nb, RNN_TS, D_MODEL)
        xn_ref[0] = pltpu.einshape("btd->tbd", yn).reshape(FFN_TM, D_MODEL).astype(BF16)
    if post == "attn":
        yn = _rms(y, g2_ref[...]).reshape(FFN_TM // 16, 16, D_MODEL)
        xn_ref[0] = pltpu.einshape("lrd->rld", yn).astype(BF16)


def _ffn(x, g, w_in, w_out, sel, pre=None, pre_args=(), post=None, g_next=None):
    const = lambda i: (0, 0)
    row = lambda i: (i, 0)
    wsel = lambda i: (sel[0], sel[1], 0, 0)
    if x.ndim == 3:
        nb, seq, _ = x.shape
        assert nb * RNN_TS == FFN_TM
        steps = seq // RNN_TS
        x_spec = pl.BlockSpec((nb, RNN_TS, D_MODEL), lambda i: (0, i, 0))
    else:
        assert pre != "rnn" and post != "rnn"
        steps = x.shape[0] // FFN_TM
        x_spec = pl.BlockSpec((FFN_TM, D_MODEL), row)
    tm_spec = pl.BlockSpec((1, FFN_TM, D_MODEL), lambda i: (i, 0, 0))
    in_specs = [x_spec]
    if pre == "rnn":
        in_specs.append(tm_spec)
    if pre == "attn":
        in_specs += [pl.BlockSpec((FFN_TM, D_MODEL), row),
                     pl.BlockSpec((D_MODEL, D_MODEL), const, pipeline_mode=pl.Buffered(1))]
    in_specs += [
        pl.BlockSpec((1, D_MODEL), const),
        pl.BlockSpec((None, None, D_MODEL, 2 * D_FF), wsel, pipeline_mode=pl.Buffered(1)),
        pl.BlockSpec((None, None, D_FF, D_MODEL), wsel, pipeline_mode=pl.Buffered(1)),
    ]
    args = [x, *pre_args, g.reshape(1, D_MODEL), w_in, w_out]
    out_shape = [jax.ShapeDtypeStruct(x.shape, F32)]
    out_specs = [x_spec]
    if post:
        in_specs.append(pl.BlockSpec((1, D_MODEL), const))
        args.append(g_next.reshape(1, D_MODEL))
    if post == "rnn":
        out_shape.append(jax.ShapeDtypeStruct((steps, FFN_TM, D_MODEL), BF16))
        out_specs.append(tm_spec)
    if post == "attn":
        seq = 16 * BLK
        tiles = seq // FFN_TM
        out_shape.append(jax.ShapeDtypeStruct((x.shape[0] // seq, 16, seq // 16, D_MODEL), BF16))
        out_specs.append(pl.BlockSpec((1, 16, FFN_TM // 16, D_MODEL), lambda i: (i // tiles, 0, i % tiles, 0)))
    res = pl.pallas_call(
        lambda *refs: _ffn_body(*refs, pre=pre, post=post),
        grid=(steps,),
        in_specs=in_specs,
        out_specs=out_specs,
        out_shape=out_shape,
        scratch_shapes=[pltpu.VMEM((FFN_TM, D_FF), BF16)],
        compiler_params=pltpu.CompilerParams(
            dimension_semantics=("arbitrary",), vmem_limit_bytes=VMEM_LIMIT_BYTES),
        name="ffn" + ("_pre_" + pre if pre else "") + ("_post_" + post if post else ""),
    )(*args)
    return res if post else res[0]


def _rglru_body(xn_ref, win_ref, cw_ref, cb_ref, wax_ref, ba_ref, bx_ref, lam_ref, wout_ref,
                o_ref, u_scr, y_scr, h_scr):
    nb = h_scr.shape[0]
    m = RNN_TS * nb
    hist = (CONV_WIDTH - 1) * nb

    @pl.when(pl.program_id(0) == 0)
    def _():
        u_scr[0:hist, :] = jnp.zeros((hist, D_RNN), F32)
        h_scr[...] = jnp.zeros_like(h_scr)

    xn = xn_ref[0]
    sp = jax.nn.softplus(-lam_ref[...])

    def project(n):
        return _dot(xn, win_ref[:, 2 * n * RNN_BLOCK:2 * (n + 1) * RNN_BLOCK])

    def mix(n, gu):
        sl = slice(n * RNN_BLOCK, (n + 1) * RNN_BLOCK)
        u_scr[hist:hist + m, sl] = gu[:, RNN_BLOCK:]
        cn = cb_ref[:, sl]
        for k in range(CONV_WIDTH):
            cn = cn + u_scr[k * nb:k * nb + m, sl] * cw_ref[k:k + 1, sl]
        u_scr[0:hist, sl] = u_scr[m:m + hist, sl]
        ra = _dot(cn.astype(BF16), wax_ref[n])
        r = jax.nn.sigmoid(ra[:, :RNN_BLOCK] + ba_ref[:, sl])
        gi = jax.nn.sigmoid(ra[:, RNN_BLOCK:] + bx_ref[:, sl])
        log_a = -LRU_C * r * sp[:, sl]
        a = jnp.exp(log_a)
        b = jnp.sqrt(-jnp.tanh(log_a) * (1.0 + a * a)) * (gi * cn)
        h = h_scr[:, sl]
        hs = []
        for t in range(RNN_TS):
            h = a[t * nb:(t + 1) * nb] * h + b[t * nb:(t + 1) * nb]
            hs.append(h)
        h_scr[:, sl] = h
        y_scr[:, sl] = (jnp.concatenate(hs, axis=0) * jax.nn.gelu(gu[:, :RNN_BLOCK])).astype(BF16)

    gu = project(0)
    for n in range(N_RNN_BLOCKS):
        gu_next = project(n + 1) if n + 1 < N_RNN_BLOCKS else None
        mix(n, gu)
        gu = gu_next
    o_ref[0] = _dot(y_scr[...], wout_ref[...])


def _rglru(xn, nb, w_in, conv_w, conv_b, wax, b_a, b_x, lam, w_out):
    steps, m, d = xn.shape
    assert nb == 8 and m == nb * RNN_TS, "one f32 sublane group per time step"
    const2 = lambda i: (0, 0)
    tm_spec = pl.BlockSpec((1, m, d), lambda i: (i, 0, 0))
    out = pl.pallas_call(
        _rglru_body,
        grid=(steps,),
        in_specs=[
            tm_spec,
            pl.BlockSpec((d, 2 * D_RNN), const2, pipeline_mode=pl.Buffered(1)),
            pl.BlockSpec((CONV_WIDTH, D_RNN), const2),
            pl.BlockSpec((1, D_RNN), const2),
            pl.BlockSpec((N_RNN_BLOCKS, RNN_BLOCK, 2 * RNN_BLOCK), lambda i: (0, 0, 0)),
            pl.BlockSpec((1, D_RNN), const2),
            pl.BlockSpec((1, D_RNN), const2),
            pl.BlockSpec((1, D_RNN), const2),
            pl.BlockSpec((D_RNN, d), const2, pipeline_mode=pl.Buffered(1)),
        ],
        out_specs=tm_spec,
        out_shape=jax.ShapeDtypeStruct((steps, m, d), F32),
        scratch_shapes=[
            pltpu.VMEM(((CONV_WIDTH - 1) * nb + m, D_RNN), F32),
            pltpu.VMEM((m, D_RNN), BF16),
            pltpu.VMEM((nb, D_RNN), F32),
        ],
        compiler_params=pltpu.CompilerParams(
            dimension_semantics=("arbitrary",), vmem_limit_bytes=VMEM_LIMIT_BYTES),
        name="rglru",
    )(xn, w_in, conv_w, conv_b.reshape(1, D_RNN), wax,
      b_a.reshape(1, D_RNN), b_x.reshape(1, D_RNN), lam.reshape(1, D_RNN), w_out)
    return out


def _t5_bucket(n):
    max_exact = N_BUCKETS // 2
    nf = np.maximum(n, 1).astype(np.float64)
    large = max_exact + (np.log(nf / max_exact) / math.log(MAX_DISTANCE / max_exact)
                         * (N_BUCKETS - max_exact)).astype(np.int64)
    large = np.minimum(large, N_BUCKETS - 1)
    return np.where(n < max_exact, n, large)


def _bias_codes():
    qi = np.arange(BLK)
    q_off = 16 * (qi % 8) + qi // 8
    kj = np.arange(2 * BLK)
    k_off = 16 * (kj % 16 - 8) + kj // 16
    dist = q_off[:, None] - k_off[None, :]
    band = (dist >= 0) & (dist <= BLK)
    code = np.where(band, _t5_bucket(np.maximum(dist, 0) * 1), -1)
    c0 = np.stack([np.where(k_off[None, :] >= 0, code, -1), code])
    q_off = 4 * (qi % 32) + qi // 32
    k_off = 4 * (kj % 64 - 32) + kj // 64
    dist = q_off[:, None] - k_off[None, :]
    band = (dist >= 0) & (dist <= BLK)
    code = np.where(band, _t5_bucket(np.maximum(dist, 0) * 4), -1)
    c1 = np.stack([np.where(k_off[None, :] >= 0, code, -1), code])
    dist = qi[:, None] - qi[None, :]
    c2 = np.where(dist >= 0, _t5_bucket(np.maximum(dist, 0) * 16), -1)
    return c0.astype(np.int32), c1.astype(np.int32), c2.astype(np.int32)


def _bias_body(tbl_ref, c0_ref, c1_ref, c2_ref, b0_ref, b1_ref, b2_ref, *, buckets):
    h = pl.program_id(0)

    def tile(code, col, used):
        acc = jnp.full(code.shape, NEG, F32)
        for b in used:
            acc = jnp.where(code == b, tbl_ref[b, col], acc)
        return acc

    for v in range(2):
        b0_ref[v, 0] = tile(c0_ref[v], h, buckets[0])
        b1_ref[v, 0] = tile(c1_ref[v], N_HEADS + h, buckets[1])
    b2_ref[0] = tile(c2_ref[...], 2 * N_HEADS + h, buckets[2])


def _bias_tiles(rel_bias):
    c0, c1, c2 = _bias_codes()
    buckets = tuple(tuple(int(b) for b in np.unique(c) if b >= 0) for c in (c0, c1, c2))
    full3 = lambda h: (0, 0, 0)
    return pl.pallas_call(
        lambda *refs: _bias_body(*refs, buckets=buckets),
        grid=(N_HEADS,),
        in_specs=[
            pl.BlockSpec(memory_space=pltpu.SMEM),
            pl.BlockSpec((2, BLK, 2 * BLK), full3),
            pl.BlockSpec((2, BLK, 2 * BLK), full3),
            pl.BlockSpec((BLK, BLK), lambda h: (0, 0)),
        ],
        out_specs=[
            pl.BlockSpec((2, 1, BLK, 2 * BLK), lambda h: (0, h, 0, 0)),
            pl.BlockSpec((2, 1, BLK, 2 * BLK), lambda h: (0, h, 0, 0)),
            pl.BlockSpec((1, BLK, BLK), lambda h: (h, 0, 0)),
        ],
        out_shape=[
            jax.ShapeDtypeStruct((2, N_HEADS, BLK, 2 * BLK), F32),
            jax.ShapeDtypeStruct((2, N_HEADS, BLK, 2 * BLK), F32),
            jax.ShapeDtypeStruct((N_HEADS, BLK, BLK), F32),
        ],
        compiler_params=pltpu.CompilerParams(dimension_semantics=("arbitrary",)),
        name="attn_bias",
    )(rel_bias, jnp.asarray(c0), jnp.asarray(c1), jnp.asarray(c2))


def _attn_body(xn_ref, w_ref, gq_ref, bd_ref, b0_ref, b1_ref, b2_ref, o_ref,
               qs, ks, vs, m_st, l_st, acc_st):
    seq = o_ref.shape[1]
    n_chunk = seq // 512
    zpad = jnp.zeros((PAD, QW), F32)
    qs[0:PAD, :] = zpad
    ks[0:PAD, :] = zpad
    vs[0:PAD, :] = zpad
    is_a = lax.broadcasted_iota(jnp.int32, (BLK, 128), 1) < HEAD_DIM

    def project(g):
        wg = w_ref[0, :, g * 3 * QW:(g + 1) * 3 * QW]
        for c in range(n_chunk):
            y = _dot(xn_ref[0, 512 * c:512 * (c + 1), :], wg)
            q, k, v = y[:, :QW], y[:, QW:2 * QW], y[:, 2 * QW:]
            ssq = _dot((q * q).astype(BF16), bd_ref[...])
            ssk = _dot((k * k).astype(BF16), bd_ref[...])
            rows = slice(PAD + 512 * c, PAD + 512 * (c + 1))
            qs[rows, :] = q * lax.rsqrt(ssq * (1.0 / HEAD_DIM) + RMS_EPS) * gq_ref[...]
            ks[rows, :] = k * lax.rsqrt(ssk * (1.0 / HEAD_DIM) + RMS_EPS)
            vs[rows, :] = v

    def gather(ref, starts, n, pad=PAD):
        return jnp.concatenate([ref[pl.ds(pad + st, n), :] for st in starts], axis=0)

    def attend(q_starts, q_n, k_starts, k_n, bias, init):
        q_blk = gather(qs, q_starts, q_n).astype(BF16)
        k_blk = gather(ks, k_starts, k_n).astype(BF16)
        v_blk = gather(vs, k_starts, k_n).astype(BF16)
        for sl in range(2):
            cols = slice(128 * sl, 128 * (sl + 1))
            q = q_blk[:, cols]
            q2 = jnp.concatenate([jnp.where(is_a, q, 0), jnp.where(is_a, 0, q)], axis=0)
            s = lax.dot_general(q2, k_blk[:, cols], (((1,), (1,)), ((), ())), preferred_element_type=F32)
            s = s + jnp.concatenate([bias(2 * sl), bias(2 * sl + 1)], axis=0)
            m = jnp.max(s, axis=-1, keepdims=True)
            p = jnp.exp(s - m)
            l = jnp.sum(p, axis=-1, keepdims=True)
            o2 = _dot(p.astype(BF16), v_blk[:, cols])
            mb = jnp.where(is_a, m[:BLK], m[BLK:])
            lb = jnp.where(is_a, l[:BLK], l[BLK:])
            ob = jnp.where(is_a, o2[:BLK], o2[BLK:])
            if not init:
                mo = gather(m_st, q_starts, q_n, 0)[:, cols]
                lo = gather(l_st, q_starts, q_n, 0)[:, cols]
                ao = gather(acc_st, q_starts, q_n, 0)[:, cols]
                mn = jnp.maximum(mo, mb)
                al = jnp.exp(mo - mn)
                be = jnp.exp(mb - mn)
                lb = al * lo + be * lb
                ob = al * ao + be * ob
                mb = mn
            for i, st in enumerate(q_starts):
                piece = slice(i * q_n, (i + 1) * q_n)
                m_st[pl.ds(st, q_n), cols] = mb[piece]
                l_st[pl.ds(st, q_n), cols] = lb[piece]
                acc_st[pl.ds(st, q_n), cols] = ob[piece]

    project(2)

    for r in range(seq // BLK):
        attend([r * BLK], BLK, [r * BLK], BLK, lambda h: b2_ref[h], init=True)

    project(1)
    for r4 in range(4):
        for n in range(BLK // 32):
            base = r4 * BLK + n * 32
            q_starts = [base + 4 * j * BLK for j in range(4)]
            k_starts = [base + 4 * j * BLK - 32 for j in range(4)]
            attend(q_starts, 32, k_starts, 64, lambda h, var=min(n, 1): b1_ref[var, h], init=False)

    project(0)
    for n in range(BLK // 8):
        q_starts = [n * 8 + r * BLK for r in range(16)]
        k_starts = [n * 8 + r * BLK - 8 for r in range(16)]
        attend(q_starts, 8, k_starts, 16, lambda h, var=min(n, 1): b0_ref[var, h], init=False)

    o = (acc_st[...] / l_st[...]).reshape(16, seq // 16, QW)
    o_ref[0] = pltpu.einshape("rlc->lrc", o).reshape(seq, QW).astype(BF16)


def _attention(xn, w4, gq, b0, b1, b2):
    b, s, d = xn.shape
    assert s == 16 * BLK, "group 2 must be one block per residue"
    nq = N_HEADS // QUAD
    bd = np.kron(np.eye(QUAD), np.ones((HEAD_DIM, HEAD_DIM))).astype(np.float32)
    return pl.pallas_call(
        _attn_body,
        grid=(b, nq),
        in_specs=[
            pl.BlockSpec((1, s, d), lambda i, j: (i, 0, 0)),
            pl.BlockSpec((1, d, N_GROUPS * 3 * QW), lambda i, j: (j, 0, 0)),
            pl.BlockSpec((1, QW), lambda i, j: (0, 0)),
            pl.BlockSpec((QW, QW), lambda i, j: (0, 0)),
            pl.BlockSpec((2, QUAD, BLK, 2 * BLK), lambda i, j: (0, j, 0, 0)),
            pl.BlockSpec((2, QUAD, BLK, 2 * BLK), lambda i, j: (0, j, 0, 0)),
            pl.BlockSpec((QUAD, BLK, BLK), lambda i, j: (j, 0, 0)),
        ],
        out_specs=pl.BlockSpec((1, s, QW), lambda i, j: (i, 0, j)),
        out_shape=jax.ShapeDtypeStruct((b, s, d), BF16),
        scratch_shapes=[pltpu.VMEM((PAD + s, QW), F32)] * 3 + [pltpu.VMEM((s, QW), F32)] * 3,
        compiler_params=pltpu.CompilerParams(
            dimension_semantics=("arbitrary", "arbitrary"), vmem_limit_bytes=VMEM_LIMIT_BYTES),
        name="dilated_attn",
    )(xn, w4, gq, jnp.asarray(bd, BF16), b0, b1, b2)


def kernel(x, norm_g, ffn_w_in, ffn_w_out, rnn_w_in, rnn_conv_w, rnn_conv_b, rnn_w_a, rnn_b_a, rnn_w_x,
           rnn_b_x, rnn_lambda, rnn_w_out, att_w_qkv, att_q_gain, att_k_gain, att_w_o, rel_bias):
    b, s, d = x.shape
    w_in = ffn_w_in.astype(BF16)
    w_out = ffn_w_out.astype(BF16)

    h, hn = _ffn(x, norm_g[0, 0], w_in, w_out, (0, 0), post="rnn", g_next=norm_g[0, 1])
    wax = jnp.concatenate([rnn_w_a[0], rnn_w_x[0]], axis=-1).astype(BF16)
    w_rnn = rnn_w_in[0].reshape(d, 2, N_RNN_BLOCKS, RNN_BLOCK).transpose(0, 2, 1, 3).reshape(d, 2 * D_RNN)
    delta = _rglru(hn, b, w_rnn.astype(BF16), rnn_conv_w[0], rnn_conv_b[0], wax,
                   rnn_b_a[0], rnn_b_x[0], rnn_lambda[0], rnn_w_out[0].astype(BF16))
    h = _ffn(h, norm_g[0, 2], w_in, w_out, (0, 1), pre="rnn", pre_args=(delta,))

    h, hn = _ffn(h.reshape(b * s, d), norm_g[1, 0], w_in, w_out, (1, 0), post="attn", g_next=norm_g[1, 1])
    nq = N_HEADS // QUAD
    w4 = att_w_qkv[0].reshape(d, 3, N_GROUPS, nq, QW).transpose(3, 0, 2, 1, 4).reshape(nq, d, N_GROUPS * 3 * QW)
    gq = jnp.tile(att_q_gain[0] * att_k_gain[0] * HEAD_DIM ** -0.5, QUAD).reshape(1, QW)
    b0, b1, b2 = _bias_tiles(rel_bias)
    o = _attention(hn.reshape(b, s, d), w4.astype(BF16), gq, b0, b1, b2)
    h = _ffn(h, norm_g[1, 2], w_in, w_out, (1, 1), pre="attn",
             pre_args=(o.reshape(b * s, d), att_w_o[0].astype(BF16)))
    return h.reshape(b, s, d)
```

```python
import math

import numpy as np
import jax
import jax.numpy as jnp
from jax import lax
from jax.experimental import pallas as pl
from jax.experimental.pallas import tpu as pltpu

F32 = jnp.float32
BF16 = jnp.bfloat16

D_MODEL = 1024
RMS_EPS = 1e-6
D_FF = 2816
FFN_RES = 0.5
D_RNN = 1280
N_RNN_BLOCKS = 10
RNN_BLOCK = 128
CONV_WIDTH = 4
LRU_C = 8.0
HEAD_DIM = 64
N_HEADS = 16
DILATION_GROUPS = ((128, 1), (512, 4), (2048, 16))
N_GROUPS = 3
N_BUCKETS = 32
MAX_DISTANCE = 2048
BLK = 128

VMEM_LIMIT_BYTES = 56 * 1024 * 1024
FFN_TM = 1024
FFN_CHUNK = 256
RNN_TS = 128
NEG = -1e30
PAD = 32
QUAD = 4
QW = QUAD * HEAD_DIM


def _rms(x, g):
    return x * lax.rsqrt(jnp.mean(x * x, axis=-1, keepdims=True) + RMS_EPS) * g


def _dot(a, b):
    return jnp.dot(a, b, preferred_element_type=F32)


def _convert_specs(convert, steps):
    nw_in, nw_out, (layer, slot) = convert
    in_rows = D_MODEL // steps
    out_slabs = 16
    out_rows = D_FF // out_slabs
    assert in_rows % 16 == 0 and out_rows % 16 == 0 and out_slabs <= steps
    clamp = lambda i: jnp.minimum(i, out_slabs - 1)
    in_specs = [
        pl.BlockSpec((None, None, in_rows, 2 * D_FF), lambda i: (layer, slot, i, 0)),
        pl.BlockSpec((None, None, out_rows, D_MODEL), lambda i: (layer, slot, clamp(i), 0)),
    ]
    out_shape = [jax.ShapeDtypeStruct((D_MODEL, 2 * D_FF), BF16), jax.ShapeDtypeStruct((D_FF, D_MODEL), BF16)]
    out_specs = [
        pl.BlockSpec((in_rows, 2 * D_FF), lambda i: (i, 0)),
        pl.BlockSpec((out_rows, D_MODEL), lambda i: (clamp(i), 0)),
    ]
    return in_specs, [nw_in, nw_out], out_shape, out_specs


def _ffn_body(*refs, pre, post, convert):
    refs = list(refs)
    x_ref = refs.pop(0)
    if pre == "rnn":
        d_ref = refs.pop(0)
    if pre == "attn":
        ao_ref, wo_ref = refs.pop(0), refs.pop(0)
    g_ref, win_ref, wout_ref = refs.pop(0), refs.pop(0), refs.pop(0)
    if post:
        g2_ref = refs.pop(0)
    if convert:
        nwin_ref, nwout_ref = refs.pop(0), refs.pop(0)
    o_ref = refs.pop(0)
    if post:
        xn_ref = refs.pop(0)
    if convert:
        cwin_ref, cwout_ref = refs.pop(0), refs.pop(0)
        cwin_ref[...] = nwin_ref[...].astype(BF16)
        cwout_ref[...] = nwout_ref[...].astype(BF16)
    (a_scr,) = refs
    nb = x_ref.shape[0]
    x = x_ref[...].reshape(FFN_TM, D_MODEL)
    if pre == "rnn":
        x = x + pltpu.einshape("tbd->btd", d_ref[0].reshape(RNN_TS, nb, D_MODEL)).reshape(FFN_TM, D_MODEL)
    if pre == "attn":
        x = x + _dot(ao_ref[...], wo_ref[...])
    xn = _rms(x, g_ref[...]).astype(BF16)
    for c in range(D_FF // FFN_CHUNK):
        lo = c * FFN_CHUNK
        gate = _dot(xn, win_ref[:, lo:lo + FFN_CHUNK])
        up = _dot(xn, win_ref[:, D_FF + lo:D_FF + lo + FFN_CHUNK])
        a_scr[:, lo:lo + FFN_CHUNK] = (gate * jax.nn.sigmoid(gate) * up).astype(BF16)
    y = x + FFN_RES * _dot(a_scr[...], wout_ref[...])
    o_ref[...] = y.reshape(o_ref.shape)
    if post == "rnn":
        yn = _rms(y, g2_ref[...]).astype(BF16).reshape(nb, RNN_TS, D_MODEL)
        xn_ref[0] = pltpu.einshape("btd->tbd", yn).reshape(FFN_TM, D_MODEL)
    if post == "attn":
        yn = _rms(y, g2_ref[...]).astype(BF16).reshape(FFN_TM // 16, 16, D_MODEL)
        xn_ref[0] = pltpu.einshape("lrd->rld", yn)


def _ffn(x, g, w_in, w_out, pre=None, pre_args=(), post=None, g_next=None, convert=None):
    const = lambda i: (0, 0)
    row = lambda i: (i, 0)
    if x.ndim == 3:
        nb, seq, _ = x.shape
        assert nb * RNN_TS == FFN_TM
        steps = seq // RNN_TS
        x_spec = pl.BlockSpec((nb, RNN_TS, D_MODEL), lambda i: (0, i, 0))
    else:
        assert pre != "rnn" and post != "rnn"
        steps = x.shape[0] // FFN_TM
        x_spec = pl.BlockSpec((FFN_TM, D_MODEL), row)
    tm_spec = pl.BlockSpec((1, FFN_TM, D_MODEL), lambda i: (i, 0, 0))
    in_specs = [x_spec]
    if pre == "rnn":
        in_specs.append(tm_spec)
    if pre == "attn":
        in_specs += [pl.BlockSpec((FFN_TM, D_MODEL), row),
                     pl.BlockSpec((D_MODEL, D_MODEL), const, pipeline_mode=pl.Buffered(1))]
    in_specs += [
        pl.BlockSpec((1, D_MODEL), const),
        pl.BlockSpec((D_MODEL, 2 * D_FF), const, pipeline_mode=pl.Buffered(1)),
        pl.BlockSpec((D_FF, D_MODEL), const, pipeline_mode=pl.Buffered(1)),
    ]
    args = [x, *pre_args, g.reshape(1, D_MODEL), w_in, w_out]
    out_shape = [jax.ShapeDtypeStruct(x.shape, F32)]
    out_specs = [x_spec]
    if post:
        in_specs.append(pl.BlockSpec((1, D_MODEL), const))
        args.append(g_next.reshape(1, D_MODEL))
    if post == "rnn":
        out_shape.append(jax.ShapeDtypeStruct((steps, FFN_TM, D_MODEL), BF16))
        out_specs.append(tm_spec)
    if post == "attn":
        seq = 16 * BLK
        tiles = seq // FFN_TM
        out_shape.append(jax.ShapeDtypeStruct((x.shape[0] // seq, 16, seq // 16, D_MODEL), BF16))
        out_specs.append(pl.BlockSpec((1, 16, FFN_TM // 16, D_MODEL), lambda i: (i // tiles, 0, i % tiles, 0)))
    if convert:
        c_in, c_args, c_shape, c_out = _convert_specs(convert, steps)
        in_specs += c_in
        args += c_args
        out_shape += c_shape
        out_specs += c_out
    res = pl.pallas_call(
        lambda *refs: _ffn_body(*refs, pre=pre, post=post, convert=bool(convert)),
        grid=(steps,),
        in_specs=in_specs,
        out_specs=out_specs,
        out_shape=out_shape,
        scratch_shapes=[pltpu.VMEM((FFN_TM, D_FF), BF16)],
        compiler_params=pltpu.CompilerParams(
            dimension_semantics=("arbitrary",), vmem_limit_bytes=VMEM_LIMIT_BYTES),
        name="ffn" + ("_pre_" + pre if pre else "") + ("_post_" + post if post else ""),
    )(*args)
    return res


def _rglru_body(xn_ref, win_ref, cw_ref, cb_ref, wax_ref, ba_ref, bx_ref, lam_ref, wout_ref,
                nwin_ref, nwout_ref, o_ref, cwin_ref, cwout_ref, u_scr, y_scr, h_scr):
    cwin_ref[...] = nwin_ref[...].astype(BF16)
    cwout_ref[...] = nwout_ref[...].astype(BF16)
    nb = h_scr.shape[0]
    m = RNN_TS * nb
    hist = (CONV_WIDTH - 1) * nb

    @pl.when(pl.program_id(0) == 0)
    def _():
        u_scr[0:hist, :] = jnp.zeros((hist, D_RNN), F32)
        h_scr[...] = jnp.zeros_like(h_scr)

    xn = xn_ref[0]
    sp = jax.nn.softplus(-lam_ref[...])

    def project(n):
        w = jnp.concatenate([win_ref[:, n * RNN_BLOCK:(n + 1) * RNN_BLOCK],
                             win_ref[:, D_RNN + n * RNN_BLOCK:D_RNN + (n + 1) * RNN_BLOCK]], axis=1)
        return _dot(xn, w)

    def mix(n, gu):
        sl = slice(n * RNN_BLOCK, (n + 1) * RNN_BLOCK)
        u_scr[hist:hist + m, sl] = gu[:, RNN_BLOCK:]
        cn = cb_ref[:, sl]
        for k in range(CONV_WIDTH):
            cn = cn + u_scr[k * nb:k * nb + m, sl] * cw_ref[k:k + 1, sl]
        u_scr[0:hist, sl] = u_scr[m:m + hist, sl]
        ra = _dot(cn.astype(BF16), wax_ref[n])
        r = jax.nn.sigmoid(ra[:, :RNN_BLOCK] + ba_ref[:, sl])
        gi = jax.nn.sigmoid(ra[:, RNN_BLOCK:] + bx_ref[:, sl])
        log_a = -LRU_C * r * sp[:, sl]
        a = jnp.exp(log_a)
        b = jnp.sqrt(-jnp.tanh(log_a) * (1.0 + a * a)) * (gi * cn)
        h = h_scr[:, sl]
        hs = []
        for t in range(RNN_TS):
            h = a[t * nb:(t + 1) * nb] * h + b[t * nb:(t + 1) * nb]
            hs.append(h)
        h_scr[:, sl] = h
        y_scr[:, sl] = (jnp.concatenate(hs, axis=0) * jax.nn.gelu(gu[:, :RNN_BLOCK])).astype(BF16)

    gu = project(0)
    for n in range(N_RNN_BLOCKS):
        gu_next = project(n + 1) if n + 1 < N_RNN_BLOCKS else None
        mix(n, gu)
        gu = gu_next
    o_ref[0] = _dot(y_scr[...], wout_ref[...])


def _rglru(xn, nb, w_in, conv_w, conv_b, wax, b_a, b_x, lam, w_out, convert):
    steps, m, d = xn.shape
    assert nb == 8 and m == nb * RNN_TS, "one f32 sublane group per time step"
    const2 = lambda i: (0, 0)
    tm_spec = pl.BlockSpec((1, m, d), lambda i: (i, 0, 0))
    c_in, c_args, c_shape, c_out = _convert_specs(convert, steps)
    return pl.pallas_call(
        _rglru_body,
        grid=(steps,),
        in_specs=[
            tm_spec,
            pl.BlockSpec((d, 2 * D_RNN), const2, pipeline_mode=pl.Buffered(1)),
            pl.BlockSpec((CONV_WIDTH, D_RNN), const2),
            pl.BlockSpec((1, D_RNN), const2),
            pl.BlockSpec((N_RNN_BLOCKS, RNN_BLOCK, 2 * RNN_BLOCK), lambda i: (0, 0, 0)),
            pl.BlockSpec((1, D_RNN), const2),
            pl.BlockSpec((1, D_RNN), const2),
            pl.BlockSpec((1, D_RNN), const2),
            pl.BlockSpec((D_RNN, d), const2, pipeline_mode=pl.Buffered(1)),
        ] + c_in,
        out_specs=[tm_spec] + c_out,
        out_shape=[jax.ShapeDtypeStruct((steps, m, d), F32)] + c_shape,
        scratch_shapes=[
            pltpu.VMEM(((CONV_WIDTH - 1) * nb + m, D_RNN), F32),
            pltpu.VMEM((m, D_RNN), BF16),
            pltpu.VMEM((nb, D_RNN), F32),
        ],
        compiler_params=pltpu.CompilerParams(
            dimension_semantics=("arbitrary",), vmem_limit_bytes=VMEM_LIMIT_BYTES),
        name="rglru",
    )(xn, w_in, conv_w, conv_b.reshape(1, D_RNN), wax,
      b_a.reshape(1, D_RNN), b_x.reshape(1, D_RNN), lam.reshape(1, D_RNN), w_out, *c_args)


def _t5_bucket(n):
    max_exact = N_BUCKETS // 2
    nf = np.maximum(n, 1).astype(np.float64)
    large = max_exact + (np.log(nf / max_exact) / math.log(MAX_DISTANCE / max_exact)
                         * (N_BUCKETS - max_exact)).astype(np.int64)
    large = np.minimum(large, N_BUCKETS - 1)
    return np.where(n < max_exact, n, large)


def _bias_codes():
    qi = np.arange(BLK)
    q_off = 16 * (qi % 8) + qi // 8
    kj = np.arange(2 * BLK)
    k_off = 16 * (kj % 16 - 8) + kj // 16
    dist = q_off[:, None] - k_off[None, :]
    band = (dist >= 0) & (dist <= BLK)
    code = np.where(band, _t5_bucket(np.maximum(dist, 0) * 1), -1)
    c0 = np.stack([np.where(k_off[None, :] >= 0, code, -1), code])
    q_off = 4 * (qi % 32) + qi // 32
    k_off = 4 * (kj % 64 - 32) + kj // 64
    dist = q_off[:, None] - k_off[None, :]
    band = (dist >= 0) & (dist <= BLK)
    code = np.where(band, _t5_bucket(np.maximum(dist, 0) * 4), -1)
    c1 = np.stack([np.where(k_off[None, :] >= 0, code, -1), code])
    dist = qi[:, None] - qi[None, :]
    c2 = np.where(dist >= 0, _t5_bucket(np.maximum(dist, 0) * 16), -1)
    return c0.astype(np.int32), c1.astype(np.int32), c2.astype(np.int32)


def _bias_body(tbl_ref, c0_ref, c1_ref, c2_ref, b0_ref, b1_ref, b2_ref, *, buckets):
    h = pl.program_id(0)

    def tile(code, col, used):
        acc = jnp.full(code.shape, NEG, F32)
        for b in used:
            acc = jnp.where(code == b, tbl_ref[b, col], acc)
        return acc

    for v in range(2):
        b0_ref[v, 0] = tile(c0_ref[v], h, buckets[0])
        b1_ref[v, 0] = tile(c1_ref[v], N_HEADS + h, buckets[1])
    b2_ref[0] = tile(c2_ref[...], 2 * N_HEADS + h, buckets[2])


def _bias_tiles(rel_bias):
    c0, c1, c2 = _bias_codes()
    buckets = tuple(tuple(int(b) for b in np.unique(c) if b >= 0) for c in (c0, c1, c2))
    full3 = lambda h: (0, 0, 0)
    return pl.pallas_call(
        lambda *refs: _bias_body(*refs, buckets=buckets),
        grid=(N_HEADS,),
        in_specs=[
            pl.BlockSpec(memory_space=pltpu.SMEM),
            pl.BlockSpec((2, BLK, 2 * BLK), full3),
            pl.BlockSpec((2, BLK, 2 * BLK), full3),
            pl.BlockSpec((BLK, BLK), lambda h: (0, 0)),
        ],
        out_specs=[
            pl.BlockSpec((2, 1, BLK, 2 * BLK), lambda h: (0, h, 0, 0)),
            pl.BlockSpec((2, 1, BLK, 2 * BLK), lambda h: (0, h, 0, 0)),
            pl.BlockSpec((1, BLK, BLK), lambda h: (h, 0, 0)),
        ],
        out_shape=[
            jax.ShapeDtypeStruct((2, N_HEADS, BLK, 2 * BLK), F32),
            jax.ShapeDtypeStruct((2, N_HEADS, BLK, 2 * BLK), F32),
            jax.ShapeDtypeStruct((N_HEADS, BLK, BLK), F32),
        ],
        compiler_params=pltpu.CompilerParams(dimension_semantics=("arbitrary",)),
        name="attn_bias",
    )(rel_bias, jnp.asarray(c0), jnp.asarray(c1), jnp.asarray(c2))


def _attn_body(*refs):
    xn_ref = refs[0]
    w_refs = refs[1:1 + 3 * N_GROUPS]
    gq_ref, bd_ref, b0_ref, b1_ref, b2_ref, o_ref, qs, ks, vs, m_st, l_st, acc_st = refs[1 + 3 * N_GROUPS:]
    seq = o_ref.shape[1]
    n_chunk = seq // 512
    zpad = jnp.zeros((PAD, QW), F32)
    qs[0:PAD, :] = zpad
    ks[0:PAD, :] = zpad
    vs[0:PAD, :] = zpad
    is_a = lax.broadcasted_iota(jnp.int32, (BLK, 128), 1) < HEAD_DIM

    def project(g):
        wg = jnp.concatenate([w_refs[t * N_GROUPS + g][...] for t in range(3)], axis=1)
        for c in range(n_chunk):
            y = _dot(xn_ref[0, 512 * c:512 * (c + 1), :], wg)
            q, k, v = y[:, :QW], y[:, QW:2 * QW], y[:, 2 * QW:]
            ssq = _dot((q * q).astype(BF16), bd_ref[...])
            ssk = _dot((k * k).astype(BF16), bd_ref[...])
            rows = slice(PAD + 512 * c, PAD + 512 * (c + 1))
            qs[rows, :] = q * lax.rsqrt(ssq * (1.0 / HEAD_DIM) + RMS_EPS) * gq_ref[...]
            ks[rows, :] = k * lax.rsqrt(ssk * (1.0 / HEAD_DIM) + RMS_EPS)
            vs[rows, :] = v

    def gather(ref, starts, n, pad=PAD):
        return jnp.concatenate([ref[pl.ds(pad + st, n), :] for st in starts], axis=0)

    def attend(q_starts, q_n, k_starts, k_n, bias, init):
        q_blk = gather(qs, q_starts, q_n).astype(BF16)
        k_blk = gather(ks, k_starts, k_n).astype(BF16)
        v_blk = gather(vs, k_starts, k_n).astype(BF16)
        for sl in range(2):
            cols = slice(128 * sl, 128 * (sl + 1))
            q = q_blk[:, cols]
            q2 = jnp.concatenate([jnp.where(is_a, q, 0), jnp.where(is_a, 0, q)], axis=0)
            s = lax.dot_general(q2, k_blk[:, cols], (((1,), (1,)), ((), ())), preferred_element_type=F32)
            s = s + jnp.concatenate([bias(2 * sl), bias(2 * sl + 1)], axis=0)
            m = jnp.max(s, axis=-1, keepdims=True)
            p = jnp.exp(s - m)
            l = jnp.sum(p, axis=-1, keepdims=True)
            o2 = _dot(p.astype(BF16), v_blk[:, cols])
            mb = jnp.where(is_a, m[:BLK], m[BLK:])
            lb = jnp.where(is_a, l[:BLK], l[BLK:])
            ob = jnp.where(is_a, o2[:BLK], o2[BLK:])
            if not init:
                mo = gather(m_st, q_starts, q_n, 0)[:, cols]
                lo = gather(l_st, q_starts, q_n, 0)[:, cols]
                ao = gather(acc_st, q_starts, q_n, 0)[:, cols]
                mn = jnp.maximum(mo, mb)
                al = jnp.exp(mo - mn)
                be = jnp.exp(mb - mn)
                lb = al * lo + be * lb
                ob = al * ao + be * ob
                mb = mn
            for i, st in enumerate(q_starts):
                piece = slice(i * q_n, (i + 1) * q_n)
                m_st[pl.ds(st, q_n), cols] = mb[piece]
                l_st[pl.ds(st, q_n), cols] = lb[piece]
                acc_st[pl.ds(st, q_n), cols] = ob[piece]

    project(2)

    for r in range(seq // BLK):
        attend([r * BLK], BLK, [r * BLK], BLK, lambda h: b2_ref[h], init=True)

    project(1)
    for r4 in range(4):
        for n in range(BLK // 32):
            base = r4 * BLK + n * 32
            q_starts = [base + 4 * j * BLK for j in range(4)]
            k_starts = [base + 4 * j * BLK - 32 for j in range(4)]
            attend(q_starts, 32, k_starts, 64, lambda h, var=min(n, 1): b1_ref[var, h], init=False)

    project(0)
    for n in range(BLK // 8):
        q_starts = [n * 8 + r * BLK for r in range(16)]
        k_starts = [n * 8 + r * BLK - 8 for r in range(16)]
        attend(q_starts, 8, k_starts, 16, lambda h, var=min(n, 1): b0_ref[var, h], init=False)

    o = (acc_st[...] / l_st[...]).reshape(16, seq // 16, QW)
    o_ref[0] = pltpu.einshape("rlc->lrc", o).reshape(seq, QW).astype(BF16)


def _attention(xn, w_qkv, gq, b0, b1, b2):
    b, s, d = xn.shape
    assert s == 16 * BLK, "group 2 must be one block per residue"
    nq = N_HEADS // QUAD
    bd = np.kron(np.eye(QUAD), np.ones((HEAD_DIM, HEAD_DIM))).astype(np.float32)
    w_specs = [pl.BlockSpec((d, QW), lambda i, j, tg=tg: (0, tg * nq + j)) for tg in range(3 * N_GROUPS)]
    return pl.pallas_call(
        _attn_body,
        grid=(b, nq),
        in_specs=[pl.BlockSpec((1, s, d), lambda i, j: (i, 0, 0))] + w_specs + [
            pl.BlockSpec((1, QW), lambda i, j: (0, 0)),
            pl.BlockSpec((QW, QW), lambda i, j: (0, 0)),
            pl.BlockSpec((2, QUAD, BLK, 2 * BLK), lambda i, j: (0, j, 0, 0)),
            pl.BlockSpec((2, QUAD, BLK, 2 * BLK), lambda i, j: (0, j, 0, 0)),
            pl.BlockSpec((QUAD, BLK, BLK), lambda i, j: (j, 0, 0)),
        ],
        out_specs=pl.BlockSpec((1, s, QW), lambda i, j: (i, 0, j)),
        out_shape=jax.ShapeDtypeStruct((b, s, d), BF16),
        scratch_shapes=[pltpu.VMEM((PAD + s, QW), F32)] * 3 + [pltpu.VMEM((s, QW), F32)] * 3,
        compiler_params=pltpu.CompilerParams(
            dimension_semantics=("arbitrary", "arbitrary"), vmem_limit_bytes=VMEM_LIMIT_BYTES),
        name="dilated_attn",
    )(xn, *([w_qkv] * (3 * N_GROUPS)), gq, jnp.asarray(bd, BF16), b0, b1, b2)


def kernel(x, norm_g, ffn_w_in, ffn_w_out, rnn_w_in, rnn_conv_w, rnn_conv_b, rnn_w_a, rnn_b_a, rnn_w_x,
           rnn_b_x, rnn_lambda, rnn_w_out, att_w_qkv, att_q_gain, att_k_gain, att_w_o, rel_bias):
    b, s, d = x.shape
    nxt = lambda layer, slot: (ffn_w_in, ffn_w_out, (layer, slot))

    h, hn, w_in_a, w_out_a = _ffn(x, norm_g[0, 0], ffn_w_in[0, 0].astype(BF16), ffn_w_out[0, 0].astype(BF16),
                                  post="rnn", g_next=norm_g[0, 1], convert=nxt(0, 1))
    wax = jnp.concatenate([rnn_w_a[0], rnn_w_x[0]], axis=-1).astype(BF16)
    delta, w_in_b, w_out_b = _rglru(hn, b, rnn_w_in[0].astype(BF16), rnn_conv_w[0], rnn_conv_b[0], wax,
                                    rnn_b_a[0], rnn_b_x[0], rnn_lambda[0], rnn_w_out[0].astype(BF16),
                                    convert=nxt(1, 0))
    (h,) = _ffn(h, norm_g[0, 2], w_in_a, w_out_a, pre="rnn", pre_args=(delta,))

    h, hn, w_in, w_out = _ffn(h.reshape(b * s, d), norm_g[1, 0], w_in_b, w_out_b, post="attn",
                              g_next=norm_g[1, 1], convert=nxt(1, 1))
    gq = jnp.tile(att_q_gain[0] * att_k_gain[0] * HEAD_DIM ** -0.5, QUAD).reshape(1, QW)
    b0, b1, b2 = _bias_tiles(rel_bias)
    o = _attention(hn.reshape(b, s, d), att_w_qkv[0].astype(BF16), gq, b0, b1, b2)
    (h,) = _ffn(h, norm_g[1, 2], w_in, w_out, pre="attn", pre_args=(o.reshape(b * s, d), att_w_o[0].astype(BF16)))
    return h.reshape(b, s, d)
```

```python
import math

import numpy as np
import jax
import jax.numpy as jnp
from jax import lax
from jax.experimental import pallas as pl
from jax.experimental.pallas import tpu as pltpu

F32 = jnp.float32
BF16 = jnp.bfloat16

D_MODEL = 1024
RMS_EPS = 1e-6
D_FF = 2816
FFN_RES = 0.5
D_RNN = 1280
N_RNN_BLOCKS = 10
RNN_BLOCK = 128
CONV_WIDTH = 4
LRU_C = 8.0
GELU_C0 = math.sqrt(2.0 / math.pi)
GELU_C1 = 0.044715 * GELU_C0
HEAD_DIM = 64
N_HEADS = 16
DILATION_GROUPS = ((128, 1), (512, 4), (2048, 16))
N_GROUPS = 3
N_BUCKETS = 32
MAX_DISTANCE = 2048
BLK = 128

VMEM_LIMIT_BYTES = 56 * 1024 * 1024
FFN_TM = 1024
FFN_CHUNK = 256
RNN_TS = 128
NEG = -1e30
PAD = 32
QUAD = 4
QW = QUAD * HEAD_DIM


def _rms(x, g):
    return x * lax.rsqrt(jnp.mean(x * x, axis=-1, keepdims=True) + RMS_EPS) * g


def _dot(a, b):
    return jnp.dot(a, b, preferred_element_type=F32)


def _convert_specs(convert, steps):
    nw_in, nw_out, (layer, slot) = convert
    in_rows = D_MODEL // steps
    out_slabs = 16
    out_rows = D_FF // out_slabs
    assert in_rows % 16 == 0 and out_rows % 16 == 0 and out_slabs <= steps
    clamp = lambda i: jnp.minimum(i, out_slabs - 1)
    in_specs = [
        pl.BlockSpec((None, None, in_rows, 2 * D_FF), lambda i: (layer, slot, i, 0)),
        pl.BlockSpec((None, None, out_rows, D_MODEL), lambda i: (layer, slot, clamp(i), 0)),
    ]
    out_shape = [jax.ShapeDtypeStruct((D_MODEL, 2 * D_FF), BF16), jax.ShapeDtypeStruct((D_FF, D_MODEL), BF16)]
    out_specs = [
        pl.BlockSpec((in_rows, 2 * D_FF), lambda i: (i, 0)),
        pl.BlockSpec((out_rows, D_MODEL), lambda i: (clamp(i), 0)),
    ]
    return in_specs, [nw_in, nw_out], out_shape, out_specs


def _ffn_body(*refs, pre, post, convert):
    refs = list(refs)
    x_ref = refs.pop(0)
    if pre == "rnn":
        d_ref = refs.pop(0)
    if pre == "attn":
        ao_ref, wo_ref = refs.pop(0), refs.pop(0)
    g_ref, win_ref, wout_ref = refs.pop(0), refs.pop(0), refs.pop(0)
    if post:
        g2_ref = refs.pop(0)
    if convert:
        nwin_ref, nwout_ref = refs.pop(0), refs.pop(0)
    o_ref = refs.pop(0)
    if post:
        xn_ref = refs.pop(0)
    if convert:
        cwin_ref, cwout_ref = refs.pop(0), refs.pop(0)
        cwin_ref[...] = nwin_ref[...].astype(BF16)
        cwout_ref[...] = nwout_ref[...].astype(BF16)
    (a_scr,) = refs
    nb = x_ref.shape[0]
    x = x_ref[...].reshape(FFN_TM, D_MODEL)
    if pre == "rnn":
        x = x + pltpu.einshape("tbd->btd", d_ref[0].reshape(RNN_TS, nb, D_MODEL)).reshape(FFN_TM, D_MODEL)
    if pre == "attn":
        x = x + _dot(ao_ref[...], wo_ref[...])
    xn = _rms(x, g_ref[...]).astype(BF16)
    for c in range(D_FF // FFN_CHUNK):
        lo = c * FFN_CHUNK
        gate = _dot(xn, win_ref[:, lo:lo + FFN_CHUNK])
        up = _dot(xn, win_ref[:, D_FF + lo:D_FF + lo + FFN_CHUNK])
        a_scr[:, lo:lo + FFN_CHUNK] = (gate * jax.nn.sigmoid(gate) * up).astype(BF16)
    y = x + FFN_RES * _dot(a_scr[...], wout_ref[...])
    o_ref[...] = y.reshape(o_ref.shape)
    if post == "rnn":
        yn = _rms(y, g2_ref[...]).astype(BF16).reshape(nb, RNN_TS, D_MODEL)
        xn_ref[0] = pltpu.einshape("btd->tbd", yn).reshape(FFN_TM, D_MODEL)
    if post == "attn":
        yn = _rms(y, g2_ref[...]).astype(BF16).reshape(FFN_TM // 16, 16, D_MODEL)
        xn_ref[0] = pltpu.einshape("lrd->rld", yn)


def _ffn(x, g, w_in, w_out, pre=None, pre_args=(), post=None, g_next=None, convert=None):
    const = lambda i: (0, 0)
    row = lambda i: (i, 0)
    if x.ndim == 3:
        nb, seq, _ = x.shape
        assert nb * RNN_TS == FFN_TM
        steps = seq // RNN_TS
        x_spec = pl.BlockSpec((nb, RNN_TS, D_MODEL), lambda i: (0, i, 0))
    else:
        assert pre != "rnn" and post != "rnn"
        steps = x.shape[0] // FFN_TM
        x_spec = pl.BlockSpec((FFN_TM, D_MODEL), row)
    tm_spec = pl.BlockSpec((1, FFN_TM, D_MODEL), lambda i: (i, 0, 0))
    in_specs = [x_spec]
    if pre == "rnn":
        in_specs.append(tm_spec)
    if pre == "attn":
        in_specs += [pl.BlockSpec((FFN_TM, D_MODEL), row),
                     pl.BlockSpec((D_MODEL, D_MODEL), const, pipeline_mode=pl.Buffered(1))]
    in_specs += [
        pl.BlockSpec((1, D_MODEL), const),
        pl.BlockSpec((D_MODEL, 2 * D_FF), const, pipeline_mode=pl.Buffered(1)),
        pl.BlockSpec((D_FF, D_MODEL), const, pipeline_mode=pl.Buffered(1)),
    ]
    args = [x, *pre_args, g.reshape(1, D_MODEL), w_in, w_out]
    out_shape = [jax.ShapeDtypeStruct(x.shape, F32)]
    out_specs = [x_spec]
    if post:
        in_specs.append(pl.BlockSpec((1, D_MODEL), const))
        args.append(g_next.reshape(1, D_MODEL))
    if post == "rnn":
        out_shape.append(jax.ShapeDtypeStruct((steps, FFN_TM, D_MODEL), BF16))
        out_specs.append(tm_spec)
    if post == "attn":
        seq = 16 * BLK
        tiles = seq // FFN_TM
        out_shape.append(jax.ShapeDtypeStruct((x.shape[0] // seq, 16, seq // 16, D_MODEL), BF16))
        out_specs.append(pl.BlockSpec((1, 16, FFN_TM // 16, D_MODEL), lambda i: (i // tiles, 0, i % tiles, 0)))
    if convert:
        c_in, c_args, c_shape, c_out = _convert_specs(convert, steps)
        in_specs += c_in
        args += c_args
        out_shape += c_shape
        out_specs += c_out
    res = pl.pallas_call(
        lambda *refs: _ffn_body(*refs, pre=pre, post=post, convert=bool(convert)),
        grid=(steps,),
        in_specs=in_specs,
        out_specs=out_specs,
        out_shape=out_shape,
        scratch_shapes=[pltpu.VMEM((FFN_TM, D_FF), BF16)],
        compiler_params=pltpu.CompilerParams(
            dimension_semantics=("arbitrary",), vmem_limit_bytes=VMEM_LIMIT_BYTES),
        name="ffn" + ("_pre_" + pre if pre else "") + ("_post_" + post if post else ""),
    )(*args)
    return res


def _rglru_body(xn_ref, win_ref, cw_ref, cb_ref, wax_ref, ba_ref, bx_ref, lam_ref, wout_ref,
                nwin_ref, nwout_ref, o_ref, cwin_ref, cwout_ref, u_scr, y_scr, h_scr):
    cwin_ref[...] = nwin_ref[...].astype(BF16)
    cwout_ref[...] = nwout_ref[...].astype(BF16)
    nb = h_scr.shape[0]
    m = RNN_TS * nb
    hist = (CONV_WIDTH - 1) * nb

    @pl.when(pl.program_id(0) == 0)
    def _():
        u_scr[0:hist, :] = jnp.zeros((hist, D_RNN), F32)
        h_scr[...] = jnp.zeros_like(h_scr)

    xn = xn_ref[0]
    sp8 = LRU_C * jax.nn.softplus(-lam_ref[...])

    def project(n):
        w = jnp.concatenate([win_ref[:, n * RNN_BLOCK:(n + 1) * RNN_BLOCK],
                             win_ref[:, D_RNN + n * RNN_BLOCK:D_RNN + (n + 1) * RNN_BLOCK]], axis=1)
        return _dot(xn, w)

    def mix(n, gu):
        sl = slice(n * RNN_BLOCK, (n + 1) * RNN_BLOCK)
        u_scr[hist:hist + m, sl] = gu[:, RNN_BLOCK:]
        cn = cb_ref[:, sl]
        for k in range(CONV_WIDTH):
            cn = cn + u_scr[k * nb:k * nb + m, sl] * cw_ref[k:k + 1, sl]
        u_scr[0:hist, sl] = u_scr[m:m + hist, sl]
        ra = _dot(cn.astype(BF16), wax_ref[n])
        r = jax.nn.sigmoid(ra[:, :RNN_BLOCK] + ba_ref[:, sl])
        gi = jax.nn.sigmoid(ra[:, RNN_BLOCK:] + bx_ref[:, sl])
        nla = r * sp8[:, sl]
        a = jnp.exp(-nla)
        z = jnp.tanh(nla) * (1.0 + a * a)
        b = jnp.where(z > 0.0, z * lax.rsqrt(z), 0.0) * (gi * cn)
        h = h_scr[:, sl]
        hs = []
        for t in range(RNN_TS):
            h = a[t * nb:(t + 1) * nb] * h + b[t * nb:(t + 1) * nb]
            hs.append(h)
        h_scr[:, sl] = h
        gate = gu[:, :RNN_BLOCK]
        half = 0.5 * gate
        inner = gate * (GELU_C0 + GELU_C1 * (gate * gate))
        y_scr[:, sl] = (jnp.concatenate(hs, axis=0) * (half + half * jnp.tanh(inner))).astype(BF16)

    gu = project(0)
    for n in range(N_RNN_BLOCKS):
        gu_next = project(n + 1) if n + 1 < N_RNN_BLOCKS else None
        mix(n, gu)
        gu = gu_next
    o_ref[0] = _dot(y_scr[...], wout_ref[...])


def _rglru(xn, nb, w_in, conv_w, conv_b, wax, b_a, b_x, lam, w_out, convert):
    steps, m, d = xn.shape
    assert nb == 8 and m == nb * RNN_TS, "one f32 sublane group per time step"
    const2 = lambda i: (0, 0)
    tm_spec = pl.BlockSpec((1, m, d), lambda i: (i, 0, 0))
    c_in, c_args, c_shape, c_out = _convert_specs(convert, steps)
    return pl.pallas_call(
        _rglru_body,
        grid=(steps,),
        in_specs=[
            tm_spec,
            pl.BlockSpec((d, 2 * D_RNN), const2, pipeline_mode=pl.Buffered(1)),
            pl.BlockSpec((CONV_WIDTH, D_RNN), const2),
            pl.BlockSpec((1, D_RNN), const2),
            pl.BlockSpec((N_RNN_BLOCKS, RNN_BLOCK, 2 * RNN_BLOCK), lambda i: (0, 0, 0)),
            pl.BlockSpec((1, D_RNN), const2),
            pl.BlockSpec((1, D_RNN), const2),
            pl.BlockSpec((1, D_RNN), const2),
            pl.BlockSpec((D_RNN, d), const2, pipeline_mode=pl.Buffered(1)),
        ] + c_in,
        out_specs=[tm_spec] + c_out,
        out_shape=[jax.ShapeDtypeStruct((steps, m, d), F32)] + c_shape,
        scratch_shapes=[
            pltpu.VMEM(((CONV_WIDTH - 1) * nb + m, D_RNN), F32),
            pltpu.VMEM((m, D_RNN), BF16),
            pltpu.VMEM((nb, D_RNN), F32),
        ],
        compiler_params=pltpu.CompilerParams(
            dimension_semantics=("arbitrary",), vmem_limit_bytes=VMEM_LIMIT_BYTES),
        name="rglru",
    )(xn, w_in, conv_w, conv_b.reshape(1, D_RNN), wax,
      b_a.reshape(1, D_RNN), b_x.reshape(1, D_RNN), lam.reshape(1, D_RNN), w_out, *c_args)


def _t5_bucket(n):
    max_exact = N_BUCKETS // 2
    nf = np.maximum(n, 1).astype(np.float64)
    large = max_exact + (np.log(nf / max_exact) / math.log(MAX_DISTANCE / max_exact)
                         * (N_BUCKETS - max_exact)).astype(np.int64)
    large = np.minimum(large, N_BUCKETS - 1)
    return np.where(n < max_exact, n, large)


def _bias_codes():
    qi = np.arange(BLK)
    q_off = 16 * (qi % 8) + qi // 8
    kj = np.arange(2 * BLK)
    k_off = 16 * (kj % 16 - 8) + kj // 16
    dist = q_off[:, None] - k_off[None, :]
    band = (dist >= 0) & (dist <= BLK)
    code = np.where(band, _t5_bucket(np.maximum(dist, 0) * 1), -1)
    c0 = np.stack([np.where(k_off[None, :] >= 0, code, -1), code])
    q_off = 4 * (qi % 32) + qi // 32
    k_off = 4 * (kj % 64 - 32) + kj // 64
    dist = q_off[:, None] - k_off[None, :]
    band = (dist >= 0) & (dist <= BLK)
    code = np.where(band, _t5_bucket(np.maximum(dist, 0) * 4), -1)
    c1 = np.stack([np.where(k_off[None, :] >= 0, code, -1), code])
    dist = qi[:, None] - qi[None, :]
    c2 = np.where(dist >= 0, _t5_bucket(np.maximum(dist, 0) * 16), -1)
    return c0.astype(np.int32), c1.astype(np.int32), c2.astype(np.int32)


def _bias_body(tbl_ref, c0_ref, c1_ref, c2_ref, nwin_ref, nwout_ref, b0_ref, b1_ref, b2_ref, cwin_ref, cwout_ref,
               *, buckets):
    h = pl.program_id(0)
    cwin_ref[...] = nwin_ref[...].astype(BF16)
    cwout_ref[...] = nwout_ref[...].astype(BF16)

    def tile(code, col, used):
        acc = jnp.full(code.shape, NEG, F32)
        for b in used:
            acc = jnp.where(code == b, tbl_ref[b, col], acc)
        return acc

    for c_ref, b_ref, g in ((c0_ref, b0_ref, 0), (c1_ref, b1_ref, 1)):
        general = tile(c_ref[1], g * N_HEADS + h, buckets[g])
        b_ref[1, 0] = general
        b_ref[0, 0] = jnp.where(c_ref[0] >= 0, general, NEG)
    b2_ref[0] = tile(c2_ref[...], 2 * N_HEADS + h, buckets[2])


def _bias_tiles(rel_bias, convert):
    c0, c1, c2 = _bias_codes()
    assert ((c0[0] < 0) | (c0[0] == c0[1])).all() and ((c1[0] < 0) | (c1[0] == c1[1])).all()
    buckets = tuple(tuple(int(b) for b in np.unique(c) if b >= 0) for c in (c0, c1, c2))
    full3 = lambda h: (0, 0, 0)
    c_in, c_args, c_shape, c_out = _convert_specs(convert, N_HEADS)
    return pl.pallas_call(
        lambda *refs: _bias_body(*refs, buckets=buckets),
        grid=(N_HEADS,),
        in_specs=[
            pl.BlockSpec(memory_space=pltpu.SMEM),
            pl.BlockSpec((2, BLK, 2 * BLK), full3),
            pl.BlockSpec((2, BLK, 2 * BLK), full3),
            pl.BlockSpec((BLK, BLK), lambda h: (0, 0)),
        ] + c_in,
        out_specs=[
            pl.BlockSpec((2, 1, BLK, 2 * BLK), lambda h: (0, h, 0, 0)),
            pl.BlockSpec((2, 1, BLK, 2 * BLK), lambda h: (0, h, 0, 0)),
            pl.BlockSpec((1, BLK, BLK), lambda h: (h, 0, 0)),
        ] + c_out,
        out_shape=[
            jax.ShapeDtypeStruct((2, N_HEADS, BLK, 2 * BLK), F32),
            jax.ShapeDtypeStruct((2, N_HEADS, BLK, 2 * BLK), F32),
            jax.ShapeDtypeStruct((N_HEADS, BLK, BLK), F32),
        ] + c_shape,
        compiler_params=pltpu.CompilerParams(dimension_semantics=("arbitrary",)),
        name="attn_bias",
    )(rel_bias, jnp.asarray(c0), jnp.asarray(c1), jnp.asarray(c2), *c_args)


def _attn_body(*refs):
    xn_ref = refs[0]
    w_refs = refs[1:1 + 3 * N_GROUPS]
    gq_ref, bd_ref, b0_ref, b1_ref, b2_ref, o_ref, qs, ks, vs, m_st, l_st, acc_st = refs[1 + 3 * N_GROUPS:]
    seq = o_ref.shape[1]
    n_chunk = seq // 512
    zpad = jnp.zeros((PAD, QW), F32)
    qs[0:PAD, :] = zpad
    ks[0:PAD, :] = zpad
    vs[0:PAD, :] = zpad
    is_a = lax.broadcasted_iota(jnp.int32, (BLK, 128), 1) < HEAD_DIM

    def project(g):
        wg = jnp.concatenate([w_refs[t * N_GROUPS + g][...] for t in range(3)], axis=1)
        for c in range(n_chunk):
            y = _dot(xn_ref[0, 512 * c:512 * (c + 1), :], wg)
            q, k, v = y[:, :QW], y[:, QW:2 * QW], y[:, 2 * QW:]
            ssq = _dot((q * q).astype(BF16), bd_ref[...])
            ssk = _dot((k * k).astype(BF16), bd_ref[...])
            rows = slice(PAD + 512 * c, PAD + 512 * (c + 1))
            qs[rows, :] = q * lax.rsqrt(ssq * (1.0 / HEAD_DIM) + RMS_EPS) * gq_ref[...]
            ks[rows, :] = k * lax.rsqrt(ssk * (1.0 / HEAD_DIM) + RMS_EPS)
            vs[rows, :] = v

    def gather(ref, starts, n, pad=PAD):
        return jnp.concatenate([ref[pl.ds(pad + st, n), :] for st in starts], axis=0)

    def attend(q_starts, q_n, k_starts, k_n, bias, init):
        q_blk = gather(qs, q_starts, q_n).astype(BF16)
        k_blk = gather(ks, k_starts, k_n).astype(BF16)
        v_blk = gather(vs, k_starts, k_n).astype(BF16)
        for sl in range(2):
            cols = slice(128 * sl, 128 * (sl + 1))
            q = q_blk[:, cols]
            q2 = jnp.concatenate([jnp.where(is_a, q, 0), jnp.where(is_a, 0, q)], axis=0)
            s = lax.dot_general(q2, k_blk[:, cols], (((1,), (1,)), ((), ())), preferred_element_type=F32)
            s = s + jnp.concatenate([bias(2 * sl), bias(2 * sl + 1)], axis=0)
            m = jnp.max(s, axis=-1, keepdims=True)
            p = jnp.exp(s - m)
            l = jnp.sum(p, axis=-1, keepdims=True)
            o2 = _dot(p.astype(BF16), v_blk[:, cols])
            mb = jnp.where(is_a, m[:BLK], m[BLK:])
            lb = jnp.where(is_a, l[:BLK], l[BLK:])
            ob = jnp.where(is_a, o2[:BLK], o2[BLK:])
            if not init:
                mo = gather(m_st, q_starts, q_n, 0)[:, cols]
                lo = gather(l_st, q_starts, q_n, 0)[:, cols]
                ao = gather(acc_st, q_starts, q_n, 0)[:, cols]
                mn = jnp.maximum(mo, mb)
                al = jnp.exp(mo - mn)
                be = jnp.exp(mb - mn)
                lb = al * lo + be * lb
                ob = al * ao + be * ob
                mb = mn
            for i, st in enumerate(q_starts):
                piece = slice(i * q_n, (i + 1) * q_n)
                m_st[pl.ds(st, q_n), cols] = mb[piece]
                l_st[pl.ds(st, q_n), cols] = lb[piece]
                acc_st[pl.ds(st, q_n), cols] = ob[piece]

    project(2)

    for r in range(seq // BLK):
        attend([r * BLK], BLK, [r * BLK], BLK, lambda h: b2_ref[h], init=True)

    project(1)
    for r4 in range(4):
        for n in range(BLK // 32):
            base = r4 * BLK + n * 32
            q_starts = [base + 4 * j * BLK for j in range(4)]
            k_starts = [base + 4 * j * BLK - 32 for j in range(4)]
            attend(q_starts, 32, k_starts, 64, lambda h, var=min(n, 1): b1_ref[var, h], init=False)

    project(0)
    for n in range(BLK // 8):
        q_starts = [n * 8 + r * BLK for r in range(16)]
        k_starts = [n * 8 + r * BLK - 8 for r in range(16)]
        attend(q_starts, 8, k_starts, 16, lambda h, var=min(n, 1): b0_ref[var, h], init=False)

    o = (acc_st[...] / l_st[...]).reshape(16, seq // 16, QW)
    o_ref[0] = pltpu.einshape("rlc->lrc", o).reshape(seq, QW).astype(BF16)


def _attention(xn, w_qkv, gq, b0, b1, b2):
    b, s, d = xn.shape
    assert s == 16 * BLK, "group 2 must be one block per residue"
    nq = N_HEADS // QUAD
    bd = np.kron(np.eye(QUAD), np.ones((HEAD_DIM, HEAD_DIM))).astype(np.float32)
    w_specs = [pl.BlockSpec((d, QW), lambda i, j, tg=tg: (0, tg * nq + j)) for tg in range(3 * N_GROUPS)]
    return pl.pallas_call(
        _attn_body,
        grid=(b, nq),
        in_specs=[pl.BlockSpec((1, s, d), lambda i, j: (i, 0, 0))] + w_specs + [
            pl.BlockSpec((1, QW), lambda i, j: (0, 0)),
            pl.BlockSpec((QW, QW), lambda i, j: (0, 0)),
            pl.BlockSpec((2, QUAD, BLK, 2 * BLK), lambda i, j: (0, j, 0, 0)),
            pl.BlockSpec((2, QUAD, BLK, 2 * BLK), lambda i, j: (0, j, 0, 0)),
            pl.BlockSpec((QUAD, BLK, BLK), lambda i, j: (j, 0, 0)),
        ],
        out_specs=pl.BlockSpec((1, s, QW), lambda i, j: (i, 0, j)),
        out_shape=jax.ShapeDtypeStruct((b, s, d), BF16),
        scratch_shapes=[pltpu.VMEM((PAD + s, QW), F32)] * 3 + [pltpu.VMEM((s, QW), F32)] * 3,
        compiler_params=pltpu.CompilerParams(
            dimension_semantics=("arbitrary", "arbitrary"), vmem_limit_bytes=VMEM_LIMIT_BYTES),
        name="dilated_attn",
    )(xn, *([w_qkv] * (3 * N_GROUPS)), gq, jnp.asarray(bd, BF16), b0, b1, b2)


def kernel(x, norm_g, ffn_w_in, ffn_w_out, rnn_w_in, rnn_conv_w, rnn_conv_b, rnn_w_a, rnn_b_a, rnn_w_x,
           rnn_b_x, rnn_lambda, rnn_w_out, att_w_qkv, att_q_gain, att_k_gain, att_w_o, rel_bias):
    b, s, d = x.shape
    nxt = lambda layer, slot: (ffn_w_in, ffn_w_out, (layer, slot))
    b0, b1, b2, w_in, w_out = _bias_tiles(rel_bias, convert=nxt(0, 0))

    h, hn, w_in_a, w_out_a = _ffn(x, norm_g[0, 0], w_in, w_out, post="rnn", g_next=norm_g[0, 1], convert=nxt(0, 1))
    wax = jnp.concatenate([rnn_w_a[0], rnn_w_x[0]], axis=-1).astype(BF16)
    delta, w_in_b, w_out_b = _rglru(hn, b, rnn_w_in[0].astype(BF16), rnn_conv_w[0], rnn_conv_b[0], wax,
                                    rnn_b_a[0], rnn_b_x[0], rnn_lambda[0], rnn_w_out[0].astype(BF16),
                                    convert=nxt(1, 0))
    (h,) = _ffn(h, norm_g[0, 2], w_in_a, w_out_a, pre="rnn", pre_args=(delta,))

    h, hn, w_in, w_out = _ffn(h.reshape(b * s, d), norm_g[1, 0], w_in_b, w_out_b, post="attn",
                              g_next=norm_g[1, 1], convert=nxt(1, 1))
    gq = jnp.tile(att_q_gain[0] * att_k_gain[0] * HEAD_DIM ** -0.5, QUAD).reshape(1, QW)
    o = _attention(hn.reshape(b, s, d), att_w_qkv[0].astype(BF16), gq, b0, b1, b2)
    (h,) = _ffn(h, norm_g[1, 2], w_in, w_out, pre="attn", pre_args=(o.reshape(b * s, d), att_w_o[0].astype(BF16)))
    return h.reshape(b, s, d)
```

```python
import math

import numpy as np
import jax
import jax.numpy as jnp
from jax import lax
from jax.experimental import pallas as pl
from jax.experimental.pallas import tpu as pltpu

F32 = jnp.float32
BF16 = jnp.bfloat16

D_MODEL = 1024
RMS_EPS = 1e-6
D_FF = 2816
FFN_RES = 0.5
D_RNN = 1280
N_RNN_BLOCKS = 10
RNN_BLOCK = 128
CONV_WIDTH = 4
LRU_C = 8.0
GELU_C0 = math.sqrt(2.0 / math.pi)
GELU_C1 = 0.044715 * GELU_C0
HEAD_DIM = 64
N_HEADS = 16
DILATION_GROUPS = ((128, 1), (512, 4), (2048, 16))
N_GROUPS = 3
N_BUCKETS = 32
MAX_DISTANCE = 2048
BLK = 128
RESIDUES = 16

VMEM_LIMIT_BYTES = 56 * 1024 * 1024
FFN_TM = 1024
FFN_CHUNK = 256
RNN_TS = 128
NEG = -1e30
PAD = 32
QUAD = 4
QW = QUAD * HEAD_DIM


def _rms(x, g):
    return x * lax.rsqrt(jnp.mean(x * x, axis=-1, keepdims=True) + RMS_EPS) * g


def _dot(a, b):
    return jnp.dot(a, b, preferred_element_type=F32)


def _convert_specs(convert, steps):
    nw_in, nw_out, (layer, slot) = convert
    in_rows = D_MODEL // steps
    out_slabs = 16
    out_rows = D_FF // out_slabs
    assert in_rows % 16 == 0 and out_rows % 16 == 0 and out_slabs <= steps
    clamp = lambda i: jnp.minimum(i, out_slabs - 1)
    in_specs = [
        pl.BlockSpec((None, None, in_rows, 2 * D_FF), lambda i: (layer, slot, i, 0)),
        pl.BlockSpec((None, None, out_rows, D_MODEL), lambda i: (layer, slot, clamp(i), 0)),
    ]
    out_shape = [jax.ShapeDtypeStruct((D_MODEL, 2 * D_FF), BF16), jax.ShapeDtypeStruct((D_FF, D_MODEL), BF16)]
    out_specs = [
        pl.BlockSpec((in_rows, 2 * D_FF), lambda i: (i, 0)),
        pl.BlockSpec((out_rows, D_MODEL), lambda i: (clamp(i), 0)),
    ]
    return in_specs, [nw_in, nw_out], out_shape, out_specs


def _ffn_body(*refs, pre, post, convert):
    refs = list(refs)
    x_ref = refs.pop(0)
    if pre == "rnn":
        d_ref = refs.pop(0)
    if pre == "attn":
        ao_ref, wo_ref = refs.pop(0), refs.pop(0)
    g_ref, win_ref, wout_ref = refs.pop(0), refs.pop(0), refs.pop(0)
    if post:
        g2_ref = refs.pop(0)
    if convert:
        nwin_ref, nwout_ref = refs.pop(0), refs.pop(0)
    o_ref = refs.pop(0)
    if post:
        xn_ref = refs.pop(0)
    if convert:
        cwin_ref, cwout_ref = refs.pop(0), refs.pop(0)
        cwin_ref[...] = nwin_ref[...].astype(BF16)
        cwout_ref[...] = nwout_ref[...].astype(BF16)
    (a_scr,) = refs
    nb = x_ref.shape[0]
    x = x_ref[...].reshape(FFN_TM, D_MODEL)
    if pre == "rnn":
        x = x + pltpu.einshape("tbd->btd", d_ref[0].reshape(RNN_TS, nb, D_MODEL)).reshape(FFN_TM, D_MODEL)
    if pre == "attn":
        x = x + _dot(ao_ref[...], wo_ref[...])
    xn = _rms(x, g_ref[...]).astype(BF16)
    for c in range(D_FF // FFN_CHUNK):
        lo = c * FFN_CHUNK
        gate = _dot(xn, win_ref[:, lo:lo + FFN_CHUNK])
        up = _dot(xn, win_ref[:, D_FF + lo:D_FF + lo + FFN_CHUNK])
        a_scr[:, lo:lo + FFN_CHUNK] = (gate * jax.nn.sigmoid(gate) * up).astype(BF16)
    y = x + FFN_RES * _dot(a_scr[...], wout_ref[...])
    o_ref[...] = y.reshape(o_ref.shape)
    if post == "rnn":
        yn = _rms(y, g2_ref[...]).astype(BF16).reshape(nb, RNN_TS, D_MODEL)
        xn_ref[0] = pltpu.einshape("btd->tbd", yn).reshape(FFN_TM, D_MODEL)
    if post == "attn":
        yn = _rms(y, g2_ref[...]).astype(BF16).reshape(FFN_TM // 16, 16, D_MODEL)
        xn_ref[0] = pltpu.einshape("lrd->rld", yn)


def _ffn(x, g, w_in, w_out, pre=None, pre_args=(), post=None, g_next=None, convert=None):
    const = lambda i: (0, 0)
    row = lambda i: (i, 0)
    if x.ndim == 3:
        nb, seq, _ = x.shape
        assert nb * RNN_TS == FFN_TM
        steps = seq // RNN_TS
        x_spec = pl.BlockSpec((nb, RNN_TS, D_MODEL), lambda i: (0, i, 0))
    else:
        assert pre != "rnn" and post != "rnn"
        steps = x.shape[0] // FFN_TM
        x_spec = pl.BlockSpec((FFN_TM, D_MODEL), row)
    tm_spec = pl.BlockSpec((1, FFN_TM, D_MODEL), lambda i: (i, 0, 0))
    in_specs = [x_spec]
    if pre == "rnn":
        in_specs.append(tm_spec)
    if pre == "attn":
        in_specs += [pl.BlockSpec((FFN_TM, D_MODEL), row),
                     pl.BlockSpec((D_MODEL, D_MODEL), const, pipeline_mode=pl.Buffered(1))]
    in_specs += [
        pl.BlockSpec((1, D_MODEL), const),
        pl.BlockSpec((D_MODEL, 2 * D_FF), const, pipeline_mode=pl.Buffered(1)),
        pl.BlockSpec((D_FF, D_MODEL), const, pipeline_mode=pl.Buffered(1)),
    ]
    args = [x, *pre_args, g.reshape(1, D_MODEL), w_in, w_out]
    out_shape = [jax.ShapeDtypeStruct(x.shape, F32)]
    out_specs = [x_spec]
    if post:
        in_specs.append(pl.BlockSpec((1, D_MODEL), const))
        args.append(g_next.reshape(1, D_MODEL))
    if post == "rnn":
        out_shape.append(jax.ShapeDtypeStruct((steps, FFN_TM, D_MODEL), BF16))
        out_specs.append(tm_spec)
    if post == "attn":
        seq = 16 * BLK
        tiles = seq // FFN_TM
        out_shape.append(jax.ShapeDtypeStruct((x.shape[0] // seq, 16, seq // 16, D_MODEL), BF16))
        out_specs.append(pl.BlockSpec((1, 16, FFN_TM // 16, D_MODEL), lambda i: (i // tiles, 0, i % tiles, 0)))
    if convert:
        c_in, c_args, c_shape, c_out = _convert_specs(convert, steps)
        in_specs += c_in
        args += c_args
        out_shape += c_shape
        out_specs += c_out
    res = pl.pallas_call(
        lambda *refs: _ffn_body(*refs, pre=pre, post=post, convert=bool(convert)),
        grid=(steps,),
        in_specs=in_specs,
        out_specs=out_specs,
        out_shape=out_shape,
        scratch_shapes=[pltpu.VMEM((FFN_TM, D_FF), BF16)],
        compiler_params=pltpu.CompilerParams(
            dimension_semantics=("arbitrary",), vmem_limit_bytes=VMEM_LIMIT_BYTES),
        name="ffn" + ("_pre_" + pre if pre else "") + ("_post_" + post if post else ""),
    )(*args)
    return res


def _rglru_body(xn_ref, win_ref, cw_ref, cb_ref, wax_ref, ba_ref, bx_ref, lam_ref, wout_ref,
                nwin_ref, nwout_ref, o_ref, cwin_ref, cwout_ref, u_scr, y_scr, h_scr):
    cwin_ref[...] = nwin_ref[...].astype(BF16)
    cwout_ref[...] = nwout_ref[...].astype(BF16)
    nb = h_scr.shape[0]
    m = RNN_TS * nb
    hist = (CONV_WIDTH - 1) * nb

    @pl.when(pl.program_id(0) == 0)
    def _():
        u_scr[0:hist, :] = jnp.zeros((hist, D_RNN), F32)
        h_scr[...] = jnp.zeros_like(h_scr)

    xn = xn_ref[0]
    sp8 = LRU_C * jax.nn.softplus(-lam_ref[...])

    def project(n):
        w = jnp.concatenate([win_ref[:, n * RNN_BLOCK:(n + 1) * RNN_BLOCK],
                             win_ref[:, D_RNN + n * RNN_BLOCK:D_RNN + (n + 1) * RNN_BLOCK]], axis=1)
        return _dot(xn, w)

    def mix(n, gu):
        sl = slice(n * RNN_BLOCK, (n + 1) * RNN_BLOCK)
        u_scr[hist:hist + m, sl] = gu[:, RNN_BLOCK:]
        cn = cb_ref[:, sl]
        for k in range(CONV_WIDTH):
            cn = cn + u_scr[k * nb:k * nb + m, sl] * cw_ref[k:k + 1, sl]
        u_scr[0:hist, sl] = u_scr[m:m + hist, sl]
        ra = _dot(cn.astype(BF16), wax_ref[n])
        r = jax.nn.sigmoid(ra[:, :RNN_BLOCK] + ba_ref[:, sl])
        gi = jax.nn.sigmoid(ra[:, RNN_BLOCK:] + bx_ref[:, sl])
        nla = r * sp8[:, sl]
        a = jnp.exp(-nla)
        z = jnp.tanh(nla) * (1.0 + a * a)
        b = jnp.where(z > 0.0, z * lax.rsqrt(z), 0.0) * (gi * cn)
        h = h_scr[:, sl]
        hs = []
        for t in range(RNN_TS):
            h = a[t * nb:(t + 1) * nb] * h + b[t * nb:(t + 1) * nb]
            hs.append(h)
        h_scr[:, sl] = h
        gate = gu[:, :RNN_BLOCK]
        half = 0.5 * gate
        inner = gate * (GELU_C0 + GELU_C1 * (gate * gate))
        y_scr[:, sl] = (jnp.concatenate(hs, axis=0) * (half + half * jnp.tanh(inner))).astype(BF16)

    gu = project(0)
    for n in range(N_RNN_BLOCKS):
        gu_next = project(n + 1) if n + 1 < N_RNN_BLOCKS else None
        mix(n, gu)
        gu = gu_next
    o_ref[0] = _dot(y_scr[...], wout_ref[...])


def _rglru(xn, nb, w_in, conv_w, conv_b, wax, b_a, b_x, lam, w_out, convert):
    steps, m, d = xn.shape
    assert nb == 8 and m == nb * RNN_TS, "one f32 sublane group per time step"
    const2 = lambda i: (0, 0)
    tm_spec = pl.BlockSpec((1, m, d), lambda i: (i, 0, 0))
    c_in, c_args, c_shape, c_out = _convert_specs(convert, steps)
    return pl.pallas_call(
        _rglru_body,
        grid=(steps,),
        in_specs=[
            tm_spec,
            pl.BlockSpec((d, 2 * D_RNN), const2, pipeline_mode=pl.Buffered(1)),
            pl.BlockSpec((CONV_WIDTH, D_RNN), const2),
            pl.BlockSpec((1, D_RNN), const2),
            pl.BlockSpec((N_RNN_BLOCKS, RNN_BLOCK, 2 * RNN_BLOCK), lambda i: (0, 0, 0)),
            pl.BlockSpec((1, D_RNN), const2),
            pl.BlockSpec((1, D_RNN), const2),
            pl.BlockSpec((1, D_RNN), const2),
            pl.BlockSpec((D_RNN, d), const2, pipeline_mode=pl.Buffered(1)),
        ] + c_in,
        out_specs=[tm_spec] + c_out,
        out_shape=[jax.ShapeDtypeStruct((steps, m, d), F32)] + c_shape,
        scratch_shapes=[
            pltpu.VMEM(((CONV_WIDTH - 1) * nb + m, D_RNN), F32),
            pltpu.VMEM((m, D_RNN), BF16),
            pltpu.VMEM((nb, D_RNN), F32),
        ],
        compiler_params=pltpu.CompilerParams(
            dimension_semantics=("arbitrary",), vmem_limit_bytes=VMEM_LIMIT_BYTES),
        name="rglru",
    )(xn, w_in, conv_w, conv_b.reshape(1, D_RNN), wax,
      b_a.reshape(1, D_RNN), b_x.reshape(1, D_RNN), lam.reshape(1, D_RNN), w_out, *c_args)


def _t5_bucket(n):
    max_exact = N_BUCKETS // 2
    nf = np.maximum(n, 1).astype(np.float64)
    large = max_exact + (np.log(nf / max_exact) / math.log(MAX_DISTANCE / max_exact)
                         * (N_BUCKETS - max_exact)).astype(np.int64)
    large = np.minimum(large, N_BUCKETS - 1)
    return np.where(n < max_exact, n, large)


def _bias_codes():
    qi = np.arange(BLK)
    q_off = 16 * (qi % 8) + qi // 8
    kj = np.arange(2 * BLK)
    k_off = 16 * (kj % 16 - 8) + kj // 16
    dist = q_off[:, None] - k_off[None, :]
    band = (dist >= 0) & (dist <= BLK)
    code = np.where(band, _t5_bucket(np.maximum(dist, 0) * 1), -1)
    c0 = np.stack([np.where(k_off[None, :] >= 0, code, -1), code])
    q_off = 4 * (qi % 32) + qi // 32
    k_off = 4 * (kj % 64 - 32) + kj // 64
    dist = q_off[:, None] - k_off[None, :]
    band = (dist >= 0) & (dist <= BLK)
    code = np.where(band, _t5_bucket(np.maximum(dist, 0) * 4), -1)
    c1 = np.stack([np.where(k_off[None, :] >= 0, code, -1), code])
    dist = qi[:, None] - qi[None, :]
    c2 = np.where(dist >= 0, _t5_bucket(np.maximum(dist, 0) * 16), -1)
    return c0.astype(np.int32), c1.astype(np.int32), c2.astype(np.int32)


def _bias_body(tbl_ref, c0_ref, c1_ref, c2_ref, nwin_ref, nwout_ref, b0_ref, b1_ref, b2_ref, cwin_ref, cwout_ref,
               *, buckets):
    h = pl.program_id(0)
    cwin_ref[...] = nwin_ref[...].astype(BF16)
    cwout_ref[...] = nwout_ref[...].astype(BF16)

    def tile(code, col, used):
        acc = jnp.full(code.shape, NEG, F32)
        for b in used:
            acc = jnp.where(code == b, tbl_ref[b, col], acc)
        return acc

    for c_ref, b_ref, g in ((c0_ref, b0_ref, 0), (c1_ref, b1_ref, 1)):
        general = tile(c_ref[1], g * N_HEADS + h, buckets[g])
        b_ref[1, 0] = general
        b_ref[0, 0] = jnp.where(c_ref[0] >= 0, general, NEG)
    b2_ref[0] = tile(c2_ref[...], 2 * N_HEADS + h, buckets[2])


def _bias_tiles(rel_bias, convert):
    c0, c1, c2 = _bias_codes()
    assert ((c0[0] < 0) | (c0[0] == c0[1])).all() and ((c1[0] < 0) | (c1[0] == c1[1])).all()
    buckets = tuple(tuple(int(b) for b in np.unique(c) if b >= 0) for c in (c0, c1, c2))
    full3 = lambda h: (0, 0, 0)
    c_in, c_args, c_shape, c_out = _convert_specs(convert, N_HEADS)
    return pl.pallas_call(
        lambda *refs: _bias_body(*refs, buckets=buckets),
        grid=(N_HEADS,),
        in_specs=[
            pl.BlockSpec(memory_space=pltpu.SMEM),
            pl.BlockSpec((2, BLK, 2 * BLK), full3),
            pl.BlockSpec((2, BLK, 2 * BLK), full3),
            pl.BlockSpec((BLK, BLK), lambda h: (0, 0)),
        ] + c_in,
        out_specs=[
            pl.BlockSpec((2, 1, BLK, 2 * BLK), lambda h: (0, h, 0, 0)),
            pl.BlockSpec((2, 1, BLK, 2 * BLK), lambda h: (0, h, 0, 0)),
            pl.BlockSpec((1, BLK, BLK), lambda h: (h, 0, 0)),
        ] + c_out,
        out_shape=[
            jax.ShapeDtypeStruct((2, N_HEADS, BLK, 2 * BLK), F32),
            jax.ShapeDtypeStruct((2, N_HEADS, BLK, 2 * BLK), F32),
            jax.ShapeDtypeStruct((N_HEADS, BLK, BLK), F32),
        ] + c_shape,
        compiler_params=pltpu.CompilerParams(dimension_semantics=("arbitrary",)),
        name="attn_bias",
    )(rel_bias, jnp.asarray(c0), jnp.asarray(c1), jnp.asarray(c2), *c_args)


def _attn_body(*refs):
    xn_ref = refs[0]
    w_refs = refs[1:1 + 3 * N_GROUPS]
    gq_ref, bd_ref, b0_ref, b1_ref, b2_ref, o_ref, qs, ks, vs, m_st, l_st, acc_st = refs[1 + 3 * N_GROUPS:]
    seq = o_ref.shape[1]
    n_chunk = seq // 512
    zpad = jnp.zeros((PAD, QW), F32)
    qs[0:PAD, :] = zpad
    ks[0:PAD, :] = zpad
    vs[0:PAD, :] = zpad
    is_a = lax.broadcasted_iota(jnp.int32, (BLK, 128), 1) < HEAD_DIM

    def project(g):
        wg = jnp.concatenate([w_refs[t * N_GROUPS + g][...] for t in range(3)], axis=1)
        for c in range(n_chunk):
            y = _dot(xn_ref[0, 512 * c:512 * (c + 1), :], wg)
            q, k, v = y[:, :QW], y[:, QW:2 * QW], y[:, 2 * QW:]
            ssq = _dot((q * q).astype(BF16), bd_ref[...])
            ssk = _dot((k * k).astype(BF16), bd_ref[...])
            rows = slice(PAD + 512 * c, PAD + 512 * (c + 1))
            qs[rows, :] = q * lax.rsqrt(ssq * (1.0 / HEAD_DIM) + RMS_EPS) * gq_ref[...]
            ks[rows, :] = k * lax.rsqrt(ssk * (1.0 / HEAD_DIM) + RMS_EPS)
            vs[rows, :] = v

    def gather(ref, starts, n, pad=PAD):
        return jnp.concatenate([ref[pl.ds(pad + st, n), :] for st in starts], axis=0)

    def attend(q_starts, q_n, k_starts, k_n, bias, init):
        q_blk = gather(qs, q_starts, q_n).astype(BF16)
        k_blk = gather(ks, k_starts, k_n).astype(BF16)
        v_blk = gather(vs, k_starts, k_n).astype(BF16)
        for sl in range(2):
            cols = slice(128 * sl, 128 * (sl + 1))
            q = q_blk[:, cols]
            q2 = jnp.concatenate([jnp.where(is_a, q, 0), jnp.where(is_a, 0, q)], axis=0)
            s = lax.dot_general(q2, k_blk[:, cols], (((1,), (1,)), ((), ())), preferred_element_type=F32)
            s = s + jnp.concatenate([bias(2 * sl), bias(2 * sl + 1)], axis=0)
            m = jnp.max(s, axis=-1, keepdims=True)
            p = jnp.exp(s - m)
            l = jnp.sum(p, axis=-1, keepdims=True)
            o2 = _dot(p.astype(BF16), v_blk[:, cols])
            mb = jnp.where(is_a, m[:BLK], m[BLK:])
            lb = jnp.where(is_a, l[:BLK], l[BLK:])
            ob = jnp.where(is_a, o2[:BLK], o2[BLK:])
            if not init:
                mo = gather(m_st, q_starts, q_n, 0)[:, cols]
                lo = gather(l_st, q_starts, q_n, 0)[:, cols]
                ao = gather(acc_st, q_starts, q_n, 0)[:, cols]
                mn = jnp.maximum(mo, mb)
                al = jnp.exp(mo - mn)
                be = jnp.exp(mb - mn)
                lb = al * lo + be * lb
                ob = al * ao + be * ob
                mb = mn
            for i, st in enumerate(q_starts):
                piece = slice(i * q_n, (i + 1) * q_n)
                m_st[pl.ds(st, q_n), cols] = mb[piece]
                l_st[pl.ds(st, q_n), cols] = lb[piece]
                acc_st[pl.ds(st, q_n), cols] = ob[piece]


    project(0)
    for n in range(BLK // 8):
        q_starts = [n * 8 + r * BLK for r in range(RESIDUES)]
        k_starts = [n * 8 + r * BLK - 8 for r in range(RESIDUES)]
        attend(q_starts, 8, k_starts, 16, lambda h, var=min(n, 1): b0_ref[var, h], init=True)

    project(1)
    for r4 in range(4):
        for n in range(BLK // 32):
            base = r4 * BLK + n * 32
            q_starts = [base + 4 * j * BLK for j in range(4)]
            k_starts = [base + 4 * j * BLK - 32 for j in range(4)]
            attend(q_starts, 32, k_starts, 64, lambda h, var=min(n, 1): b1_ref[var, h], init=False)

    project(2)
    for r in range(seq // BLK):
        attend([r * BLK], BLK, [r * BLK], BLK, lambda h: b2_ref[h], init=False)

    o = (acc_st[...] / l_st[...]).reshape(16, seq // 16, QW)
    o_ref[0] = pltpu.einshape("rlc->lrc", o).reshape(seq, QW).astype(BF16)


def _attention(xn, w_qkv, gq, b0, b1, b2):
    b, s, d = xn.shape
    assert s == 16 * BLK, "group 2 must be one block per residue"
    nq = N_HEADS // QUAD
    bd = np.kron(np.eye(QUAD), np.ones((HEAD_DIM, HEAD_DIM))).astype(np.float32)
    w_specs = [pl.BlockSpec((d, QW), lambda i, j, tg=tg: (0, tg * nq + j)) for tg in range(3 * N_GROUPS)]
    return pl.pallas_call(
        _attn_body,
        grid=(b, nq),
        in_specs=[pl.BlockSpec((1, s, d), lambda i, j: (i, 0, 0))] + w_specs + [
            pl.BlockSpec((1, QW), lambda i, j: (0, 0)),
            pl.BlockSpec((QW, QW), lambda i, j: (0, 0)),
            pl.BlockSpec((2, QUAD, BLK, 2 * BLK), lambda i, j: (0, j, 0, 0)),
            pl.BlockSpec((2, QUAD, BLK, 2 * BLK), lambda i, j: (0, j, 0, 0)),
            pl.BlockSpec((QUAD, BLK, BLK), lambda i, j: (j, 0, 0)),
        ],
        out_specs=pl.BlockSpec((1, s, QW), lambda i, j: (i, 0, j)),
        out_shape=jax.ShapeDtypeStruct((b, s, d), BF16),
        scratch_shapes=[pltpu.VMEM((PAD + s, QW), F32)] * 3 + [pltpu.VMEM((s, QW), F32)] * 3,
        compiler_params=pltpu.CompilerParams(
            dimension_semantics=("arbitrary", "arbitrary"), vmem_limit_bytes=VMEM_LIMIT_BYTES),
        name="dilated_attn",
    )(xn, *([w_qkv] * (3 * N_GROUPS)), gq, jnp.asarray(bd, BF16), b0, b1, b2)


def kernel(x, norm_g, ffn_w_in, ffn_w_out, rnn_w_in, rnn_conv_w, rnn_conv_b, rnn_w_a, rnn_b_a, rnn_w_x,
           rnn_b_x, rnn_lambda, rnn_w_out, att_w_qkv, att_q_gain, att_k_gain, att_w_o, rel_bias):
    b, s, d = x.shape
    nxt = lambda layer, slot: (ffn_w_in, ffn_w_out, (layer, slot))
    b0, b1, b2, w_in, w_out = _bias_tiles(rel_bias, convert=nxt(0, 0))

    h, hn, w_in_a, w_out_a = _ffn(x, norm_g[0, 0], w_in, w_out, post="rnn", g_next=norm_g[0, 1], convert=nxt(0, 1))
    wax = jnp.concatenate([rnn_w_a[0], rnn_w_x[0]], axis=-1).astype(BF16)
    delta, w_in_b, w_out_b = _rglru(hn, b, rnn_w_in[0].astype(BF16), rnn_conv_w[0], rnn_conv_b[0], wax,
                                    rnn_b_a[0], rnn_b_x[0], rnn_lambda[0], rnn_w_out[0].astype(BF16),
                                    convert=nxt(1, 0))
    (h,) = _ffn(h, norm_g[0, 2], w_in_a, w_out_a, pre="rnn", pre_args=(delta,))

    h, hn, w_in, w_out = _ffn(h.reshape(b * s, d), norm_g[1, 0], w_in_b, w_out_b, post="attn",
                              g_next=norm_g[1, 1], convert=nxt(1, 1))
    gq = jnp.tile(att_q_gain[0] * att_k_gain[0] * HEAD_DIM ** -0.5, QUAD).reshape(1, QW)
    o = _attention(hn.reshape(b, s, d), att_w_qkv[0].astype(BF16), gq, b0, b1, b2)
    (h,) = _ffn(h, norm_g[1, 2], w_in, w_out, pre="attn", pre_args=(o.reshape(b * s, d), att_w_o[0].astype(BF16)))
    return h.reshape(b, s, d)
```

```python
import math

import numpy as np
import jax
import jax.numpy as jnp
from jax import lax
from jax.experimental import pallas as pl
from jax.experimental.pallas import tpu as pltpu

F32 = jnp.float32
BF16 = jnp.bfloat16

D_MODEL = 1024
RMS_EPS = 1e-6
D_FF = 2816
FFN_RES = 0.5
D_RNN = 1280
N_RNN_BLOCKS = 10
RNN_BLOCK = 128
CONV_WIDTH = 4
LRU_C = 8.0
GELU_C0 = math.sqrt(2.0 / math.pi)
GELU_C1 = 0.044715 * GELU_C0
HEAD_DIM = 64
N_HEADS = 16
DILATION_GROUPS = ((128, 1), (512, 4), (2048, 16))
N_GROUPS = 3
N_BUCKETS = 32
MAX_DISTANCE = 2048
BLK = 128
RESIDUES = 16

VMEM_LIMIT_BYTES = 56 * 1024 * 1024
FFN_TM = 1024
FFN_CHUNK = 256
RNN_TS = 128
NEG = -1e30
PAD = 32
BF16_ROWS = 16
CAST_SLABS = 16
QUAD = 4
QW = QUAD * HEAD_DIM


def _rms(x, g):
    return x * lax.rsqrt(jnp.mean(x * x, axis=-1, keepdims=True) + RMS_EPS) * g


def _dot(a, b):
    return jnp.dot(a, b, preferred_element_type=F32)


def _cast_specs(jobs, steps):
    in_specs, args, out_shape, out_specs = [], [], [], []
    assert CAST_SLABS <= steps
    clamp = lambda i: jnp.minimum(i, CAST_SLABS - 1)
    for arr, lead, col_block, n_cols in jobs:
        n_rows = arr.shape[-2]
        rows = n_rows // CAST_SLABS
        assert rows * CAST_SLABS == n_rows and rows % BF16_ROWS == 0 and arr.ndim == len(lead) + 2
        in_specs.append(pl.BlockSpec((None,) * len(lead) + (rows, n_cols),
                                     lambda i, lead=lead, cb=col_block: (*lead, clamp(i), cb)))
        args.append(arr)
        out_shape.append(jax.ShapeDtypeStruct((n_rows, n_cols), BF16))
        out_specs.append(pl.BlockSpec((rows, n_cols), lambda i: (clamp(i), 0)))
    return in_specs, args, out_shape, out_specs


def _cast_run(in_refs, out_refs):
    for src, dst in zip(in_refs, out_refs):
        dst[...] = src[...].astype(BF16)


def _ffn_body(*refs, pre, post, n_cast):
    refs = list(refs)
    x_ref = refs.pop(0)
    if pre == "rnn":
        d_ref = refs.pop(0)
    if pre == "attn":
        ao_ref, wo_ref = refs.pop(0), refs.pop(0)
    g_ref, win_ref, wout_ref = refs.pop(0), refs.pop(0), refs.pop(0)
    if post:
        g2_ref = refs.pop(0)
    cast_in = [refs.pop(0) for _ in range(n_cast)]
    o_ref = refs.pop(0)
    if post:
        xn_ref = refs.pop(0)
    _cast_run(cast_in, [refs.pop(0) for _ in range(n_cast)])
    (a_scr,) = refs
    nb = x_ref.shape[0]
    x = x_ref[...].reshape(FFN_TM, D_MODEL)
    if pre == "rnn":
        x = x + pltpu.einshape("tbd->btd", d_ref[0].reshape(RNN_TS, nb, D_MODEL)).reshape(FFN_TM, D_MODEL)
    if pre == "attn":
        x = x + _dot(ao_ref[...], wo_ref[...])
    xn = _rms(x, g_ref[...]).astype(BF16)
    for c in range(D_FF // FFN_CHUNK):
        lo = c * FFN_CHUNK
        gate = _dot(xn, win_ref[:, lo:lo + FFN_CHUNK])
        up = _dot(xn, win_ref[:, D_FF + lo:D_FF + lo + FFN_CHUNK])
        a_scr[:, lo:lo + FFN_CHUNK] = (gate * jax.nn.sigmoid(gate) * up).astype(BF16)
    y = x + FFN_RES * _dot(a_scr[...], wout_ref[...])
    o_ref[...] = y.reshape(o_ref.shape)
    if post == "rnn":
        yn = _rms(y, g2_ref[...]).astype(BF16).reshape(nb, RNN_TS, D_MODEL)
        xn_ref[0] = pltpu.einshape("btd->tbd", yn).reshape(FFN_TM, D_MODEL)
    if post == "attn":
        yn = _rms(y, g2_ref[...]).astype(BF16).reshape(FFN_TM // 16, 16, D_MODEL)
        xn_ref[0] = pltpu.einshape("lrd->rld", yn)


def _ffn(x, g, w_in, w_out, pre=None, pre_args=(), post=None, g_next=None, cast=()):
    const = lambda i: (0, 0)
    row = lambda i: (i, 0)
    if x.ndim == 3:
        nb, seq, _ = x.shape
        assert nb * RNN_TS == FFN_TM
        steps = seq // RNN_TS
        x_spec = pl.BlockSpec((nb, RNN_TS, D_MODEL), lambda i: (0, i, 0))
    else:
        assert pre != "rnn" and post != "rnn"
        steps = x.shape[0] // FFN_TM
        x_spec = pl.BlockSpec((FFN_TM, D_MODEL), row)
    tm_spec = pl.BlockSpec((1, FFN_TM, D_MODEL), lambda i: (i, 0, 0))
    in_specs = [x_spec]
    if pre == "rnn":
        in_specs.append(tm_spec)
    if pre == "attn":
        in_specs += [pl.BlockSpec((FFN_TM, D_MODEL), row),
                     pl.BlockSpec((D_MODEL, D_MODEL), const, pipeline_mode=pl.Buffered(1))]
    in_specs += [
        pl.BlockSpec((1, D_MODEL), const),
        pl.BlockSpec((D_MODEL, 2 * D_FF), const, pipeline_mode=pl.Buffered(1)),
        pl.BlockSpec((D_FF, D_MODEL), const, pipeline_mode=pl.Buffered(1)),
    ]
    args = [x, *pre_args, g.reshape(1, D_MODEL), w_in, w_out]
    out_shape = [jax.ShapeDtypeStruct(x.shape, F32)]
    out_specs = [x_spec]
    if post:
        in_specs.append(pl.BlockSpec((1, D_MODEL), const))
        args.append(g_next.reshape(1, D_MODEL))
    if post == "rnn":
        out_shape.append(jax.ShapeDtypeStruct((steps, FFN_TM, D_MODEL), BF16))
        out_specs.append(tm_spec)
    if post == "attn":
        seq = 16 * BLK
        tiles = seq // FFN_TM
        out_shape.append(jax.ShapeDtypeStruct((x.shape[0] // seq, 16, seq // 16, D_MODEL), BF16))
        out_specs.append(pl.BlockSpec((1, 16, FFN_TM // 16, D_MODEL), lambda i: (i // tiles, 0, i % tiles, 0)))
    c_in, c_args, c_shape, c_out = _cast_specs(cast, steps)
    in_specs += c_in
    args += c_args
    out_shape += c_shape
    out_specs += c_out
    res = pl.pallas_call(
        lambda *refs: _ffn_body(*refs, pre=pre, post=post, n_cast=len(cast)),
        grid=(steps,),
        in_specs=in_specs,
        out_specs=out_specs,
        out_shape=out_shape,
        scratch_shapes=[pltpu.VMEM((FFN_TM, D_FF), BF16)],
        compiler_params=pltpu.CompilerParams(
            dimension_semantics=("arbitrary",), vmem_limit_bytes=VMEM_LIMIT_BYTES),
        name="ffn" + ("_pre_" + pre if pre else "") + ("_post_" + post if post else ""),
    )(*args)
    return res


def _rglru_body(xn_ref, win_ref, cw_ref, cb_ref, wax_ref, ba_ref, bx_ref, lam_ref, wout_ref, *rest, n_cast):
    cast_in, o_ref, cast_out = rest[:n_cast], rest[n_cast], rest[n_cast + 1:2 * n_cast + 1]
    u_scr, y_scr, h_scr = rest[2 * n_cast + 1:]
    _cast_run(cast_in, cast_out)
    nb = h_scr.shape[0]
    m = RNN_TS * nb
    hist = (CONV_WIDTH - 1) * nb

    @pl.when(pl.program_id(0) == 0)
    def _():
        u_scr[0:hist, :] = jnp.zeros((hist, D_RNN), F32)
        h_scr[...] = jnp.zeros_like(h_scr)

    xn = xn_ref[0]
    sp8 = LRU_C * jax.nn.softplus(-lam_ref[...])

    def project(n):
        w = jnp.concatenate([win_ref[:, n * RNN_BLOCK:(n + 1) * RNN_BLOCK],
                             win_ref[:, D_RNN + n * RNN_BLOCK:D_RNN + (n + 1) * RNN_BLOCK]], axis=1)
        return _dot(xn, w)

    def mix(n, gu):
        sl = slice(n * RNN_BLOCK, (n + 1) * RNN_BLOCK)
        u_scr[hist:hist + m, sl] = gu[:, RNN_BLOCK:]
        cn = cb_ref[:, sl]
        for k in range(CONV_WIDTH):
            cn = cn + u_scr[k * nb:k * nb + m, sl] * cw_ref[k:k + 1, sl]
        u_scr[0:hist, sl] = u_scr[m:m + hist, sl]
        ra = _dot(cn.astype(BF16), wax_ref[n])
        r = jax.nn.sigmoid(ra[:, :RNN_BLOCK] + ba_ref[:, sl])
        gi = jax.nn.sigmoid(ra[:, RNN_BLOCK:] + bx_ref[:, sl])
        nla = r * sp8[:, sl]
        a = jnp.exp(-nla)
        z = jnp.tanh(nla) * (1.0 + a * a)
        b = jnp.where(z > 0.0, z * lax.rsqrt(z), 0.0) * (gi * cn)
        h = h_scr[:, sl]
        hs = []
        for t in range(RNN_TS):
            h = a[t * nb:(t + 1) * nb] * h + b[t * nb:(t + 1) * nb]
            hs.append(h)
        h_scr[:, sl] = h
        gate = gu[:, :RNN_BLOCK]
        half = 0.5 * gate
        inner = gate * (GELU_C0 + GELU_C1 * (gate * gate))
        y_scr[:, sl] = (jnp.concatenate(hs, axis=0) * (half + half * jnp.tanh(inner))).astype(BF16)

    gu = project(0)
    for n in range(N_RNN_BLOCKS):
        gu_next = project(n + 1) if n + 1 < N_RNN_BLOCKS else None
        mix(n, gu)
        gu = gu_next
    o_ref[0] = _dot(y_scr[...], wout_ref[...])


def _rglru(xn, nb, w_in, conv_w, conv_b, wax, b_a, b_x, lam, w_out, cast=()):
    steps, m, d = xn.shape
    assert nb == 8 and m == nb * RNN_TS, "one f32 sublane group per time step"
    const2 = lambda i: (0, 0)
    tm_spec = pl.BlockSpec((1, m, d), lambda i: (i, 0, 0))
    c_in, c_args, c_shape, c_out = _cast_specs(cast, steps)
    return pl.pallas_call(
        lambda *refs: _rglru_body(*refs, n_cast=len(cast)),
        grid=(steps,),
        in_specs=[
            tm_spec,
            pl.BlockSpec((d, 2 * D_RNN), const2, pipeline_mode=pl.Buffered(1)),
            pl.BlockSpec((CONV_WIDTH, D_RNN), const2),
            pl.BlockSpec((1, D_RNN), const2),
            pl.BlockSpec((N_RNN_BLOCKS, RNN_BLOCK, 2 * RNN_BLOCK), lambda i: (0, 0, 0)),
            pl.BlockSpec((1, D_RNN), const2),
            pl.BlockSpec((1, D_RNN), const2),
            pl.BlockSpec((1, D_RNN), const2),
            pl.BlockSpec((D_RNN, d), const2, pipeline_mode=pl.Buffered(1)),
        ] + c_in,
        out_specs=[tm_spec] + c_out,
        out_shape=[jax.ShapeDtypeStruct((steps, m, d), F32)] + c_shape,
        scratch_shapes=[
            pltpu.VMEM(((CONV_WIDTH - 1) * nb + m, D_RNN), F32),
            pltpu.VMEM((m, D_RNN), BF16),
            pltpu.VMEM((nb, D_RNN), F32),
        ],
        compiler_params=pltpu.CompilerParams(
            dimension_semantics=("arbitrary",), vmem_limit_bytes=VMEM_LIMIT_BYTES),
        name="rglru",
    )(xn, w_in, conv_w, conv_b.reshape(1, D_RNN), wax,
      b_a.reshape(1, D_RNN), b_x.reshape(1, D_RNN), lam.reshape(1, D_RNN), w_out, *c_args)


def _t5_bucket(n):
    max_exact = N_BUCKETS // 2
    nf = np.maximum(n, 1).astype(np.float64)
    large = max_exact + (np.log(nf / max_exact) / math.log(MAX_DISTANCE / max_exact)
                         * (N_BUCKETS - max_exact)).astype(np.int64)
    large = np.minimum(large, N_BUCKETS - 1)
    return np.where(n < max_exact, n, large)


def _bias_codes():
    qi = np.arange(BLK)
    q_off = 16 * (qi % 8) + qi // 8
    kj = np.arange(2 * BLK)
    k_off = 16 * (kj % 16 - 8) + kj // 16
    dist = q_off[:, None] - k_off[None, :]
    band = (dist >= 0) & (dist <= BLK)
    code = np.where(band, _t5_bucket(np.maximum(dist, 0) * 1), -1)
    c0 = np.stack([np.where(k_off[None, :] >= 0, code, -1), code])
    q_off = 4 * (qi % 32) + qi // 32
    k_off = 4 * (kj % 64 - 32) + kj // 64
    dist = q_off[:, None] - k_off[None, :]
    band = (dist >= 0) & (dist <= BLK)
    code = np.where(band, _t5_bucket(np.maximum(dist, 0) * 4), -1)
    c1 = np.stack([np.where(k_off[None, :] >= 0, code, -1), code])
    dist = qi[:, None] - qi[None, :]
    c2 = np.where(dist >= 0, _t5_bucket(np.maximum(dist, 0) * 16), -1)
    return c0.astype(np.int32), c1.astype(np.int32), c2.astype(np.int32)


def _bias_body(tbl_ref, c0_ref, c1_ref, c2_ref, *rest, buckets, n_cast):
    cast_in, (b0_ref, b1_ref, b2_ref), cast_out = rest[:n_cast], rest[n_cast:n_cast + 3], rest[n_cast + 3:]
    h = pl.program_id(0)
    _cast_run(cast_in, cast_out)

    def tile(code, col, used):
        acc = jnp.full(code.shape, NEG, F32)
        for b in used:
            acc = jnp.where(code == b, tbl_ref[b, col], acc)
        return acc

    for c_ref, b_ref, g in ((c0_ref, b0_ref, 0), (c1_ref, b1_ref, 1)):
        general = tile(c_ref[1], g * N_HEADS + h, buckets[g])
        b_ref[1, 0] = general
        b_ref[0, 0] = jnp.where(c_ref[0] >= 0, general, NEG)
    b2_ref[0] = tile(c2_ref[...], 2 * N_HEADS + h, buckets[2])


def _bias_tiles(rel_bias, cast=()):
    c0, c1, c2 = _bias_codes()
    assert ((c0[0] < 0) | (c0[0] == c0[1])).all() and ((c1[0] < 0) | (c1[0] == c1[1])).all()
    buckets = tuple(tuple(int(b) for b in np.unique(c) if b >= 0) for c in (c0, c1, c2))
    full3 = lambda h: (0, 0, 0)
    c_in, c_args, c_shape, c_out = _cast_specs(cast, N_HEADS)
    return pl.pallas_call(
        lambda *refs: _bias_body(*refs, buckets=buckets, n_cast=len(cast)),
        grid=(N_HEADS,),
        in_specs=[
            pl.BlockSpec(memory_space=pltpu.SMEM),
            pl.BlockSpec((2, BLK, 2 * BLK), full3),
            pl.BlockSpec((2, BLK, 2 * BLK), full3),
            pl.BlockSpec((BLK, BLK), lambda h: (0, 0)),
        ] + c_in,
        out_specs=[
            pl.BlockSpec((2, 1, BLK, 2 * BLK), lambda h: (0, h, 0, 0)),
            pl.BlockSpec((2, 1, BLK, 2 * BLK), lambda h: (0, h, 0, 0)),
            pl.BlockSpec((1, BLK, BLK), lambda h: (h, 0, 0)),
        ] + c_out,
        out_shape=[
            jax.ShapeDtypeStruct((2, N_HEADS, BLK, 2 * BLK), F32),
            jax.ShapeDtypeStruct((2, N_HEADS, BLK, 2 * BLK), F32),
            jax.ShapeDtypeStruct((N_HEADS, BLK, BLK), F32),
        ] + c_shape,
        compiler_params=pltpu.CompilerParams(dimension_semantics=("arbitrary",)),
        name="attn_bias",
    )(rel_bias, jnp.asarray(c0), jnp.asarray(c1), jnp.asarray(c2), *c_args)


def _attn_body(*refs):
    xn_ref = refs[0]
    w_refs = refs[1:1 + 3 * N_GROUPS]
    gq_ref, bd_ref, b0_ref, b1_ref, b2_ref, o_ref, qs, ks, vs, m_st, l_st, acc_st = refs[1 + 3 * N_GROUPS:]
    seq = o_ref.shape[1]
    n_chunk = seq // 512
    zpad = jnp.zeros((PAD, QW), F32)
    qs[0:PAD, :] = zpad
    ks[0:PAD, :] = zpad
    vs[0:PAD, :] = zpad
    is_a = lax.broadcasted_iota(jnp.int32, (BLK, 128), 1) < HEAD_DIM

    def project(g):
        wg = jnp.concatenate([w_refs[t * N_GROUPS + g][...] for t in range(3)], axis=1)
        for c in range(n_chunk):
            y = _dot(xn_ref[0, 512 * c:512 * (c + 1), :], wg)
            q, k, v = y[:, :QW], y[:, QW:2 * QW], y[:, 2 * QW:]
            ssq = _dot((q * q).astype(BF16), bd_ref[...])
            ssk = _dot((k * k).astype(BF16), bd_ref[...])
            rows = slice(PAD + 512 * c, PAD + 512 * (c + 1))
            qs[rows, :] = q * lax.rsqrt(ssq * (1.0 / HEAD_DIM) + RMS_EPS) * gq_ref[...]
            ks[rows, :] = k * lax.rsqrt(ssk * (1.0 / HEAD_DIM) + RMS_EPS)
            vs[rows, :] = v

    def gather(ref, starts, n, pad=PAD):
        return jnp.concatenate([ref[pl.ds(pad + st, n), :] for st in starts], axis=0)

    def attend(q_starts, q_n, k_starts, k_n, bias, init):
        q_blk = gather(qs, q_starts, q_n).astype(BF16)
        k_blk = gather(ks, k_starts, k_n).astype(BF16)
        v_blk = gather(vs, k_starts, k_n).astype(BF16)
        for sl in range(2):
            cols = slice(128 * sl, 128 * (sl + 1))
            q = q_blk[:, cols]
            q2 = jnp.concatenate([jnp.where(is_a, q, 0), jnp.where(is_a, 0, q)], axis=0)
            s = lax.dot_general(q2, k_blk[:, cols], (((1,), (1,)), ((), ())), preferred_element_type=F32)
            s = s + jnp.concatenate([bias(2 * sl), bias(2 * sl + 1)], axis=0)
            m = jnp.max(s, axis=-1, keepdims=True)
            p = jnp.exp(s - m)
            l = jnp.sum(p, axis=-1, keepdims=True)
            o2 = _dot(p.astype(BF16), v_blk[:, cols])
            mb = jnp.where(is_a, m[:BLK], m[BLK:])
            lb = jnp.where(is_a, l[:BLK], l[BLK:])
            ob = jnp.where(is_a, o2[:BLK], o2[BLK:])
            if not init:
                mo = gather(m_st, q_starts, q_n, 0)[:, cols]
                lo = gather(l_st, q_starts, q_n, 0)[:, cols]
                ao = gather(acc_st, q_starts, q_n, 0)[:, cols]
                mn = jnp.maximum(mo, mb)
                al = jnp.exp(mo - mn)
                be = jnp.exp(mb - mn)
                lb = al * lo + be * lb
                ob = al * ao + be * ob
                mb = mn
            for i, st in enumerate(q_starts):
                piece = slice(i * q_n, (i + 1) * q_n)
                m_st[pl.ds(st, q_n), cols] = mb[piece]
                l_st[pl.ds(st, q_n), cols] = lb[piece]
                acc_st[pl.ds(st, q_n), cols] = ob[piece]


    project(0)
    for n in range(BLK // 8):
        q_starts = [n * 8 + r * BLK for r in range(RESIDUES)]
        k_starts = [n * 8 + r * BLK - 8 for r in range(RESIDUES)]
        attend(q_starts, 8, k_starts, 16, lambda h, var=min(n, 1): b0_ref[var, h], init=True)

    project(1)
    for r4 in range(4):
        for n in range(BLK // 32):
            base = r4 * BLK + n * 32
            q_starts = [base + 4 * j * BLK for j in range(4)]
            k_starts = [base + 4 * j * BLK - 32 for j in range(4)]
            attend(q_starts, 32, k_starts, 64, lambda h, var=min(n, 1): b1_ref[var, h], init=False)

    project(2)
    for r in range(seq // BLK):
        attend([r * BLK], BLK, [r * BLK], BLK, lambda h: b2_ref[h], init=False)

    o = (acc_st[...] / l_st[...]).reshape(16, seq // 16, QW)
    o_ref[0] = pltpu.einshape("rlc->lrc", o).reshape(seq, QW).astype(BF16)


def _attention(xn, w_q, w_k, w_v, gq, b0, b1, b2):
    b, s, d = xn.shape
    assert s == RESIDUES * BLK, "group 2 must be one block per residue"
    nq = N_HEADS // QUAD
    bd = np.kron(np.eye(QUAD), np.ones((HEAD_DIM, HEAD_DIM))).astype(np.float32)
    w_specs = [pl.BlockSpec((d, QW), lambda i, j, g=g: (0, g * nq + j)) for _ in range(3) for g in range(N_GROUPS)]
    w_args = [w for w in (w_q, w_k, w_v) for _ in range(N_GROUPS)]
    return pl.pallas_call(
        _attn_body,
        grid=(b, nq),
        in_specs=[pl.BlockSpec((1, s, d), lambda i, j: (i, 0, 0))] + w_specs + [
            pl.BlockSpec((1, QW), lambda i, j: (0, 0)),
            pl.BlockSpec((QW, QW), lambda i, j: (0, 0)),
            pl.BlockSpec((2, QUAD, BLK, 2 * BLK), lambda i, j: (0, j, 0, 0)),
            pl.BlockSpec((2, QUAD, BLK, 2 * BLK), lambda i, j: (0, j, 0, 0)),
            pl.BlockSpec((QUAD, BLK, BLK), lambda i, j: (j, 0, 0)),
        ],
        out_specs=pl.BlockSpec((1, s, QW), lambda i, j: (i, 0, j)),
        out_shape=jax.ShapeDtypeStruct((b, s, d), BF16),
        scratch_shapes=[pltpu.VMEM((PAD + s, QW), F32)] * 3 + [pltpu.VMEM((s, QW), F32)] * 3,
        compiler_params=pltpu.CompilerParams(
            dimension_semantics=("arbitrary", "arbitrary"), vmem_limit_bytes=VMEM_LIMIT_BYTES),
        name="dilated_attn",
    )(xn, *w_args, gq, jnp.asarray(bd, BF16), b0, b1, b2)


def kernel(x, norm_g, ffn_w_in, ffn_w_out, rnn_w_in, rnn_conv_w, rnn_conv_b, rnn_w_a, rnn_b_a, rnn_w_x,
           rnn_b_x, rnn_lambda, rnn_w_out, att_w_qkv, att_q_gain, att_k_gain, att_w_o, rel_bias):
    b, s, d = x.shape
    ffn_cast = lambda layer, slot: [(ffn_w_in, (layer, slot), 0, 2 * D_FF), (ffn_w_out, (layer, slot), 0, d)]
    qkv_cast = lambda third: [(att_w_qkv, (0,), third, att_w_qkv.shape[-1] // 3)]
    b0, b1, b2, w_in, w_out = _bias_tiles(rel_bias, cast=ffn_cast(0, 0))

    rnn_cast = [(rnn_w_in, (0,), 0, 2 * D_RNN), (rnn_w_out, (0,), 0, d)]
    h, hn, w_in_a, w_out_a, w_rnn_in, w_rnn_out = _ffn(x, norm_g[0, 0], w_in, w_out, post="rnn", g_next=norm_g[0, 1],
                                                       cast=ffn_cast(0, 1) + rnn_cast)
    wax = jnp.concatenate([rnn_w_a[0], rnn_w_x[0]], axis=-1).astype(BF16)
    delta, w_in_b, w_out_b, w_q = _rglru(hn, b, w_rnn_in, rnn_conv_w[0], rnn_conv_b[0], wax, rnn_b_a[0], rnn_b_x[0],
                                         rnn_lambda[0], w_rnn_out, cast=ffn_cast(1, 0) + qkv_cast(0))
    h, w_k = _ffn(h, norm_g[0, 2], w_in_a, w_out_a, pre="rnn", pre_args=(delta,), cast=qkv_cast(1))

    h, hn, w_in, w_out, w_v = _ffn(h.reshape(b * s, d), norm_g[1, 0], w_in_b, w_out_b, post="attn",
                                   g_next=norm_g[1, 1], cast=ffn_cast(1, 1) + qkv_cast(2))
    gq = jnp.tile(att_q_gain[0] * att_k_gain[0] * HEAD_DIM ** -0.5, QUAD).reshape(1, QW)
    o = _attention(hn.reshape(b, s, d), w_q, w_k, w_v, gq, b0, b1, b2)
    (h,) = _ffn(h, norm_g[1, 2], w_in, w_out, pre="attn", pre_args=(o.reshape(b * s, d), att_w_o[0].astype(BF16)))
    return h.reshape(b, s, d)
```

```python
import math

import numpy as np
import jax
import jax.numpy as jnp
from jax import lax
from jax.experimental import pallas as pl
from jax.experimental.pallas import tpu as pltpu

F32 = jnp.float32
BF16 = jnp.bfloat16

D_MODEL = 1024
RMS_EPS = 1e-6
D_FF = 2816
FFN_RES = 0.5
D_RNN = 1280
N_RNN_BLOCKS = 10
RNN_BLOCK = 128
CONV_WIDTH = 4
LRU_C = 8.0
GELU_C0 = math.sqrt(2.0 / math.pi)
GELU_C1 = 0.044715 * GELU_C0
HEAD_DIM = 64
N_HEADS = 16
DILATION_GROUPS = ((128, 1), (512, 4), (2048, 16))
N_GROUPS = 3
N_BUCKETS = 32
MAX_DISTANCE = 2048
BLK = 128
RESIDUES = 16

VMEM_LIMIT_BYTES = 56 * 1024 * 1024
FFN_TM = 1024
FFN_CHUNK = 256
RNN_TS = 128
NEG = -1e30
PAD = 32
BF16_ROWS = 16
CAST_SLABS = 16
QUAD = 4
QW = QUAD * HEAD_DIM


def _rms(x, g):
    return x * lax.rsqrt(jnp.mean(x * x, axis=-1, keepdims=True) + RMS_EPS) * g


def _dot(a, b):
    return jnp.dot(a, b, preferred_element_type=F32)


def _cast_specs(jobs, steps):
    in_specs, args, out_shape, out_specs = [], [], [], []
    assert CAST_SLABS <= steps
    clamp = lambda i: jnp.minimum(i, CAST_SLABS - 1)
    for arr, lead, col_block, n_cols in jobs:
        n_rows = arr.shape[-2]
        rows = n_rows // CAST_SLABS
        assert rows * CAST_SLABS == n_rows and rows % BF16_ROWS == 0 and arr.ndim == len(lead) + 2
        in_specs.append(pl.BlockSpec((None,) * len(lead) + (rows, n_cols),
                                     lambda i, lead=lead, cb=col_block: (*lead, clamp(i), cb)))
        args.append(arr)
        out_shape.append(jax.ShapeDtypeStruct((n_rows, n_cols), BF16))
        out_specs.append(pl.BlockSpec((rows, n_cols), lambda i: (clamp(i), 0)))
    return in_specs, args, out_shape, out_specs


def _cast_run(in_refs, out_refs):
    for src, dst in zip(in_refs, out_refs):
        dst[...] = src[...].astype(BF16)


def _ffn_body(*refs, pre, post, n_cast):
    refs = list(refs)
    x_ref = refs.pop(0)
    if pre == "rnn":
        d_ref = refs.pop(0)
    if pre == "attn":
        ao_ref, wo_ref = refs.pop(0), refs.pop(0)
    g_ref, win_ref, wout_ref = refs.pop(0), refs.pop(0), refs.pop(0)
    if post:
        g2_ref = refs.pop(0)
    cast_in = [refs.pop(0) for _ in range(n_cast)]
    o_ref = refs.pop(0)
    if post:
        xn_ref = refs.pop(0)
    _cast_run(cast_in, [refs.pop(0) for _ in range(n_cast)])
    (a_scr,) = refs
    nb = x_ref.shape[0]
    x = x_ref[...].reshape(FFN_TM, D_MODEL)
    if pre == "rnn":
        x = x + pltpu.einshape("tbd->btd", d_ref[0].reshape(RNN_TS, nb, D_MODEL)).reshape(FFN_TM, D_MODEL)
    if pre == "attn":
        x = x + _dot(ao_ref[...], wo_ref[...])
    xn = _rms(x, g_ref[...]).astype(BF16)
    for c in range(D_FF // FFN_CHUNK):
        lo = c * FFN_CHUNK
        gate = _dot(xn, win_ref[:, lo:lo + FFN_CHUNK])
        up = _dot(xn, win_ref[:, D_FF + lo:D_FF + lo + FFN_CHUNK])
        a_scr[:, lo:lo + FFN_CHUNK] = (gate * jax.nn.sigmoid(gate) * up).astype(BF16)
    y = x + FFN_RES * _dot(a_scr[...], wout_ref[...])
    o_ref[...] = y.reshape(o_ref.shape)
    if post == "rnn":
        yn = _rms(y, g2_ref[...]).astype(BF16).reshape(nb, RNN_TS, D_MODEL)
        xn_ref[0] = pltpu.einshape("btd->tbd", yn).reshape(FFN_TM, D_MODEL)
    if post == "attn":
        yn = _rms(y, g2_ref[...]).astype(BF16).reshape(FFN_TM // 16, 16, D_MODEL)
        xn_ref[0] = pltpu.einshape("lrd->rld", yn)


def _ffn(x, g, w_in, w_out, pre=None, pre_args=(), post=None, g_next=None, cast=()):
    const = lambda i: (0, 0)
    row = lambda i: (i, 0)
    if x.ndim == 3:
        nb, seq, _ = x.shape
        assert nb * RNN_TS == FFN_TM
        steps = seq // RNN_TS
        x_spec = pl.BlockSpec((nb, RNN_TS, D_MODEL), lambda i: (0, i, 0))
    else:
        assert pre != "rnn" and post != "rnn"
        steps = x.shape[0] // FFN_TM
        x_spec = pl.BlockSpec((FFN_TM, D_MODEL), row)
    tm_spec = pl.BlockSpec((1, FFN_TM, D_MODEL), lambda i: (i, 0, 0))
    in_specs = [x_spec]
    if pre == "rnn":
        in_specs.append(tm_spec)
    if pre == "attn":
        in_specs += [pl.BlockSpec((FFN_TM, D_MODEL), row),
                     pl.BlockSpec((D_MODEL, D_MODEL), const, pipeline_mode=pl.Buffered(1))]
    in_specs += [
        pl.BlockSpec((1, D_MODEL), const),
        pl.BlockSpec((D_MODEL, 2 * D_FF), const, pipeline_mode=pl.Buffered(1)),
        pl.BlockSpec((D_FF, D_MODEL), const, pipeline_mode=pl.Buffered(1)),
    ]
    args = [x, *pre_args, g.reshape(1, D_MODEL), w_in, w_out]
    out_shape = [jax.ShapeDtypeStruct(x.shape, F32)]
    out_specs = [x_spec]
    if post:
        in_specs.append(pl.BlockSpec((1, D_MODEL), const))
        args.append(g_next.reshape(1, D_MODEL))
    if post == "rnn":
        out_shape.append(jax.ShapeDtypeStruct((steps, FFN_TM, D_MODEL), BF16))
        out_specs.append(tm_spec)
    if post == "attn":
        seq = 16 * BLK
        tiles = seq // FFN_TM
        out_shape.append(jax.ShapeDtypeStruct((x.shape[0] // seq, 16, seq // 16, D_MODEL), BF16))
        out_specs.append(pl.BlockSpec((1, 16, FFN_TM // 16, D_MODEL), lambda i: (i // tiles, 0, i % tiles, 0)))
    c_in, c_args, c_shape, c_out = _cast_specs(cast, steps)
    in_specs += c_in
    args += c_args
    out_shape += c_shape
    out_specs += c_out
    res = pl.pallas_call(
        lambda *refs: _ffn_body(*refs, pre=pre, post=post, n_cast=len(cast)),
        grid=(steps,),
        in_specs=in_specs,
        out_specs=out_specs,
        out_shape=out_shape,
        scratch_shapes=[pltpu.VMEM((FFN_TM, D_FF), BF16)],
        compiler_params=pltpu.CompilerParams(
            dimension_semantics=("arbitrary",), vmem_limit_bytes=VMEM_LIMIT_BYTES),
        name="ffn" + ("_pre_" + pre if pre else "") + ("_post_" + post if post else ""),
    )(*args)
    return res


def _rglru_body(xn_ref, win_ref, cw_ref, cb_ref, wax_ref, ba_ref, bx_ref, lam_ref, wout_ref, *rest, n_cast):
    cast_in, o_ref, cast_out = rest[:n_cast], rest[n_cast], rest[n_cast + 1:2 * n_cast + 1]
    u_scr, y_scr, h_scr = rest[2 * n_cast + 1:]
    _cast_run(cast_in, cast_out)
    nb = h_scr.shape[0]
    m = RNN_TS * nb
    hist = (CONV_WIDTH - 1) * nb

    @pl.when(pl.program_id(0) == 0)
    def _():
        u_scr[...] = jnp.zeros_like(u_scr)
        h_scr[...] = jnp.zeros_like(h_scr)

    xn = xn_ref[0]
    sp8 = LRU_C * jax.nn.softplus(-lam_ref[...])

    def project(n):
        w = jnp.concatenate([win_ref[:, n * RNN_BLOCK:(n + 1) * RNN_BLOCK],
                             win_ref[:, D_RNN + n * RNN_BLOCK:D_RNN + (n + 1) * RNN_BLOCK]], axis=1)
        return _dot(xn, w)

    def mix(n, gu):
        sl = slice(n * RNN_BLOCK, (n + 1) * RNN_BLOCK)
        u_ext = jnp.concatenate([u_scr[:, sl], gu[:, RNN_BLOCK:]], axis=0)
        cn = cb_ref[:, sl]
        for k in range(CONV_WIDTH):
            cn = cn + u_ext[k * nb:k * nb + m] * cw_ref[k:k + 1, sl]
        u_scr[:, sl] = u_ext[m:m + hist]
        ra = _dot(cn.astype(BF16), wax_ref[n])
        r = jax.nn.sigmoid(ra[:, :RNN_BLOCK] + ba_ref[:, sl])
        gi = jax.nn.sigmoid(ra[:, RNN_BLOCK:] + bx_ref[:, sl])
        nla = r * sp8[:, sl]
        a = jnp.exp(-nla)
        z = jnp.tanh(nla) * (1.0 + a * a)
        b = jnp.where(z > 0.0, z * lax.rsqrt(z), 0.0) * (gi * cn)
        h = h_scr[:, sl]
        hs = []
        for t in range(RNN_TS):
            h = a[t * nb:(t + 1) * nb] * h + b[t * nb:(t + 1) * nb]
            hs.append(h)
        h_scr[:, sl] = h
        gate = gu[:, :RNN_BLOCK]
        half = 0.5 * gate
        inner = gate * (GELU_C0 + GELU_C1 * (gate * gate))
        y_scr[:, sl] = (jnp.concatenate(hs, axis=0) * (half + half * jnp.tanh(inner))).astype(BF16)

    gu = project(0)
    for n in range(N_RNN_BLOCKS):
        gu_next = project(n + 1) if n + 1 < N_RNN_BLOCKS else None
        mix(n, gu)
        gu = gu_next
    o_ref[0] = _dot(y_scr[...], wout_ref[...])


def _rglru(xn, nb, w_in, conv_w, conv_b, wax, b_a, b_x, lam, w_out, cast=()):
    steps, m, d = xn.shape
    assert nb == 8 and m == nb * RNN_TS, "one f32 sublane group per time step"
    const2 = lambda i: (0, 0)
    tm_spec = pl.BlockSpec((1, m, d), lambda i: (i, 0, 0))
    c_in, c_args, c_shape, c_out = _cast_specs(cast, steps)
    return pl.pallas_call(
        lambda *refs: _rglru_body(*refs, n_cast=len(cast)),
        grid=(steps,),
        in_specs=[
            tm_spec,
            pl.BlockSpec((d, 2 * D_RNN), const2, pipeline_mode=pl.Buffered(1)),
            pl.BlockSpec((CONV_WIDTH, D_RNN), const2),
            pl.BlockSpec((1, D_RNN), const2),
            pl.BlockSpec((N_RNN_BLOCKS, RNN_BLOCK, 2 * RNN_BLOCK), lambda i: (0, 0, 0)),
            pl.BlockSpec((1, D_RNN), const2),
            pl.BlockSpec((1, D_RNN), const2),
            pl.BlockSpec((1, D_RNN), const2),
            pl.BlockSpec((D_RNN, d), const2, pipeline_mode=pl.Buffered(1)),
        ] + c_in,
        out_specs=[tm_spec] + c_out,
        out_shape=[jax.ShapeDtypeStruct((steps, m, d), F32)] + c_shape,
        scratch_shapes=[
            pltpu.VMEM(((CONV_WIDTH - 1) * nb, D_RNN), F32),
            pltpu.VMEM((m, D_RNN), BF16),
            pltpu.VMEM((nb, D_RNN), F32),
        ],
        compiler_params=pltpu.CompilerParams(
            dimension_semantics=("arbitrary",), vmem_limit_bytes=VMEM_LIMIT_BYTES),
        name="rglru",
    )(xn, w_in, conv_w, conv_b.reshape(1, D_RNN), wax,
      b_a.reshape(1, D_RNN), b_x.reshape(1, D_RNN), lam.reshape(1, D_RNN), w_out, *c_args)


def _t5_bucket(n):
    max_exact = N_BUCKETS // 2
    nf = np.maximum(n, 1).astype(np.float64)
    large = max_exact + (np.log(nf / max_exact) / math.log(MAX_DISTANCE / max_exact)
                         * (N_BUCKETS - max_exact)).astype(np.int64)
    large = np.minimum(large, N_BUCKETS - 1)
    return np.where(n < max_exact, n, large)


def _bias_codes():
    qi = np.arange(BLK)
    q_off = 16 * (qi % 8) + qi // 8
    kj = np.arange(2 * BLK)
    k_off = 16 * (kj % 16 - 8) + kj // 16
    dist = q_off[:, None] - k_off[None, :]
    band = (dist >= 0) & (dist <= BLK)
    code = np.where(band, _t5_bucket(np.maximum(dist, 0) * 1), -1)
    c0 = np.stack([np.where(k_off[None, :] >= 0, code, -1), code])
    q_off = 4 * (qi % 32) + qi // 32
    k_off = 4 * (kj % 64 - 32) + kj // 64
    dist = q_off[:, None] - k_off[None, :]
    band = (dist >= 0) & (dist <= BLK)
    code = np.where(band, _t5_bucket(np.maximum(dist, 0) * 4), -1)
    c1 = np.stack([np.where(k_off[None, :] >= 0, code, -1), code])
    dist = qi[:, None] - qi[None, :]
    c2 = np.where(dist >= 0, _t5_bucket(np.maximum(dist, 0) * 16), -1)
    return c0.astype(np.int32), c1.astype(np.int32), c2.astype(np.int32)


def _bias_body(tbl_ref, c0_ref, c1_ref, c2_ref, *rest, buckets, n_cast):
    cast_in, (b0_ref, b1_ref, b2_ref), cast_out = rest[:n_cast], rest[n_cast:n_cast + 3], rest[n_cast + 3:]
    h = pl.program_id(0)
    _cast_run(cast_in, cast_out)

    def tile(code, col, used):
        acc = jnp.full(code.shape, NEG, F32)
        for b in used:
            acc = jnp.where(code == b, tbl_ref[b, col], acc)
        return acc

    for c_ref, b_ref, g in ((c0_ref, b0_ref, 0), (c1_ref, b1_ref, 1)):
        general = tile(c_ref[1], g * N_HEADS + h, buckets[g])
        b_ref[1, 0] = general
        b_ref[0, 0] = jnp.where(c_ref[0] >= 0, general, NEG)
    b2_ref[0] = tile(c2_ref[...], 2 * N_HEADS + h, buckets[2])


def _bias_tiles(rel_bias, cast=()):
    c0, c1, c2 = _bias_codes()
    assert ((c0[0] < 0) | (c0[0] == c0[1])).all() and ((c1[0] < 0) | (c1[0] == c1[1])).all()
    buckets = tuple(tuple(int(b) for b in np.unique(c) if b >= 0) for c in (c0, c1, c2))
    full3 = lambda h: (0, 0, 0)
    c_in, c_args, c_shape, c_out = _cast_specs(cast, N_HEADS)
    return pl.pallas_call(
        lambda *refs: _bias_body(*refs, buckets=buckets, n_cast=len(cast)),
        grid=(N_HEADS,),
        in_specs=[
            pl.BlockSpec(memory_space=pltpu.SMEM),
            pl.BlockSpec((2, BLK, 2 * BLK), full3),
            pl.BlockSpec((2, BLK, 2 * BLK), full3),
            pl.BlockSpec((BLK, BLK), lambda h: (0, 0)),
        ] + c_in,
        out_specs=[
            pl.BlockSpec((2, 1, BLK, 2 * BLK), lambda h: (0, h, 0, 0)),
            pl.BlockSpec((2, 1, BLK, 2 * BLK), lambda h: (0, h, 0, 0)),
            pl.BlockSpec((1, BLK, BLK), lambda h: (h, 0, 0)),
        ] + c_out,
        out_shape=[
            jax.ShapeDtypeStruct((2, N_HEADS, BLK, 2 * BLK), F32),
            jax.ShapeDtypeStruct((2, N_HEADS, BLK, 2 * BLK), F32),
            jax.ShapeDtypeStruct((N_HEADS, BLK, BLK), F32),
        ] + c_shape,
        compiler_params=pltpu.CompilerParams(dimension_semantics=("arbitrary",)),
        name="attn_bias",
    )(rel_bias, jnp.asarray(c0), jnp.asarray(c1), jnp.asarray(c2), *c_args)


def _attn_body(*refs):
    xn_ref = refs[0]
    w_refs = refs[1:1 + 3 * N_GROUPS]
    gq_ref, bd_ref, b0_ref, b1_ref, b2_ref, o_ref, qs, ks, vs, m_st, l_st, acc_st = refs[1 + 3 * N_GROUPS:]
    seq = o_ref.shape[1]
    n_chunk = seq // 512
    zpad = jnp.zeros((PAD, QW), F32)
    qs[0:PAD, :] = zpad
    ks[0:PAD, :] = zpad
    vs[0:PAD, :] = zpad
    is_a = lax.broadcasted_iota(jnp.int32, (BLK, 128), 1) < HEAD_DIM

    def project(g):
        wg = jnp.concatenate([w_refs[t * N_GROUPS + g][...] for t in range(3)], axis=1)
        for c in range(n_chunk):
            y = _dot(xn_ref[0, 512 * c:512 * (c + 1), :], wg)
            q, k, v = y[:, :QW], y[:, QW:2 * QW], y[:, 2 * QW:]
            ssq = _dot((q * q).astype(BF16), bd_ref[...])
            ssk = _dot((k * k).astype(BF16), bd_ref[...])
            rows = slice(PAD + 512 * c, PAD + 512 * (c + 1))
            qs[rows, :] = q * lax.rsqrt(ssq * (1.0 / HEAD_DIM) + RMS_EPS) * gq_ref[...]
            ks[rows, :] = k * lax.rsqrt(ssk * (1.0 / HEAD_DIM) + RMS_EPS)
            vs[rows, :] = v

    def gather(ref, starts, n, pad=PAD):
        return jnp.concatenate([ref[pl.ds(pad + st, n), :] for st in starts], axis=0)

    def attend(q_starts, q_n, k_starts, k_n, bias, init):
        q_blk = gather(qs, q_starts, q_n).astype(BF16)
        k_blk = gather(ks, k_starts, k_n).astype(BF16)
        v_blk = gather(vs, k_starts, k_n).astype(BF16)
        for sl in range(2):
            cols = slice(128 * sl, 128 * (sl + 1))
            q = q_blk[:, cols]
            q2 = jnp.concatenate([jnp.where(is_a, q, 0), jnp.where(is_a, 0, q)], axis=0)
            s = lax.dot_general(q2, k_blk[:, cols], (((1,), (1,)), ((), ())), preferred_element_type=F32)
            s = s + jnp.concatenate([bias(2 * sl), bias(2 * sl + 1)], axis=0)
            m = jnp.max(s, axis=-1, keepdims=True)
            p = jnp.exp(s - m)
            l = jnp.sum(p, axis=-1, keepdims=True)
            o2 = _dot(p.astype(BF16), v_blk[:, cols])
            mb = jnp.where(is_a, m[:BLK], m[BLK:])
            lb = jnp.where(is_a, l[:BLK], l[BLK:])
            ob = jnp.where(is_a, o2[:BLK], o2[BLK:])
            if not init:
                mo = gather(m_st, q_starts, q_n, 0)[:, cols]
                lo = gather(l_st, q_starts, q_n, 0)[:, cols]
                ao = gather(acc_st, q_starts, q_n, 0)[:, cols]
                mn = jnp.maximum(mo, mb)
                al = jnp.exp(mo - mn)
                be = jnp.exp(mb - mn)
                lb = al * lo + be * lb
                ob = al * ao + be * ob
                mb = mn
            for i, st in enumerate(q_starts):
                piece = slice(i * q_n, (i + 1) * q_n)
                m_st[pl.ds(st, q_n), cols] = mb[piece]
                l_st[pl.ds(st, q_n), cols] = lb[piece]
                acc_st[pl.ds(st, q_n), cols] = ob[piece]


    project(0)
    for n in range(BLK // 8):
        q_starts = [n * 8 + r * BLK for r in range(RESIDUES)]
        k_starts = [n * 8 + r * BLK - 8 for r in range(RESIDUES)]
        attend(q_starts, 8, k_starts, 16, lambda h, var=min(n, 1): b0_ref[var, h], init=True)

    project(1)
    for r4 in range(4):
        for n in range(BLK // 32):
            base = r4 * BLK + n * 32
            q_starts = [base + 4 * j * BLK for j in range(4)]
            k_starts = [base + 4 * j * BLK - 32 for j in range(4)]
            attend(q_starts, 32, k_starts, 64, lambda h, var=min(n, 1): b1_ref[var, h], init=False)

    project(2)
    for r in range(seq // BLK):
        attend([r * BLK], BLK, [r * BLK], BLK, lambda h: b2_ref[h], init=False)

    o = (acc_st[...] / l_st[...]).reshape(16, seq // 16, QW)
    o_ref[0] = pltpu.einshape("rlc->lrc", o).reshape(seq, QW).astype(BF16)


def _attention(xn, w_q, w_k, w_v, gq, b0, b1, b2):
    b, s, d = xn.shape
    assert s == RESIDUES * BLK, "group 2 must be one block per residue"
    nq = N_HEADS // QUAD
    bd = np.kron(np.eye(QUAD), np.ones((HEAD_DIM, HEAD_DIM))).astype(np.float32)
    w_specs = [pl.BlockSpec((d, QW), lambda i, j, g=g: (0, g * nq + j)) for _ in range(3) for g in range(N_GROUPS)]
    w_args = [w for w in (w_q, w_k, w_v) for _ in range(N_GROUPS)]
    return pl.pallas_call(
        _attn_body,
        grid=(b, nq),
        in_specs=[pl.BlockSpec((1, s, d), lambda i, j: (i, 0, 0))] + w_specs + [
            pl.BlockSpec((1, QW), lambda i, j: (0, 0)),
            pl.BlockSpec((QW, QW), lambda i, j: (0, 0)),
            pl.BlockSpec((2, QUAD, BLK, 2 * BLK), lambda i, j: (0, j, 0, 0)),
            pl.BlockSpec((2, QUAD, BLK, 2 * BLK), lambda i, j: (0, j, 0, 0)),
            pl.BlockSpec((QUAD, BLK, BLK), lambda i, j: (j, 0, 0)),
        ],
        out_specs=pl.BlockSpec((1, s, QW), lambda i, j: (i, 0, j)),
        out_shape=jax.ShapeDtypeStruct((b, s, d), BF16),
        scratch_shapes=[pltpu.VMEM((PAD + s, QW), F32)] * 3 + [pltpu.VMEM((s, QW), F32)] * 3,
        compiler_params=pltpu.CompilerParams(
            dimension_semantics=("arbitrary", "arbitrary"), vmem_limit_bytes=VMEM_LIMIT_BYTES),
        name="dilated_attn",
    )(xn, *w_args, gq, jnp.asarray(bd, BF16), b0, b1, b2)


def kernel(x, norm_g, ffn_w_in, ffn_w_out, rnn_w_in, rnn_conv_w, rnn_conv_b, rnn_w_a, rnn_b_a, rnn_w_x,
           rnn_b_x, rnn_lambda, rnn_w_out, att_w_qkv, att_q_gain, att_k_gain, att_w_o, rel_bias):
    b, s, d = x.shape
    ffn_cast = lambda layer, slot: [(ffn_w_in, (layer, slot), 0, 2 * D_FF), (ffn_w_out, (layer, slot), 0, d)]
    qkv_cast = lambda third: [(att_w_qkv, (0,), third, att_w_qkv.shape[-1] // 3)]
    b0, b1, b2, w_in, w_out = _bias_tiles(rel_bias, cast=ffn_cast(0, 0))

    rnn_cast = [(rnn_w_in, (0,), 0, 2 * D_RNN), (rnn_w_out, (0,), 0, d)]
    h, hn, w_in_a, w_out_a, w_rnn_in, w_rnn_out = _ffn(x, norm_g[0, 0], w_in, w_out, post="rnn", g_next=norm_g[0, 1],
                                                       cast=ffn_cast(0, 1) + rnn_cast)
    wax = jnp.concatenate([rnn_w_a[0], rnn_w_x[0]], axis=-1).astype(BF16)
    delta, w_in_b, w_out_b, w_q = _rglru(hn, b, w_rnn_in, rnn_conv_w[0], rnn_conv_b[0], wax, rnn_b_a[0], rnn_b_x[0],
                                         rnn_lambda[0], w_rnn_out, cast=ffn_cast(1, 0) + qkv_cast(0))
    h, w_k = _ffn(h, norm_g[0, 2], w_in_a, w_out_a, pre="rnn", pre_args=(delta,), cast=qkv_cast(1))

    h, hn, w_in, w_out, w_v = _ffn(h.reshape(b * s, d), norm_g[1, 0], w_in_b, w_out_b, post="attn",
                                   g_next=norm_g[1, 1], cast=ffn_cast(1, 1) + qkv_cast(2))
    gq = jnp.tile(att_q_gain[0] * att_k_gain[0] * HEAD_DIM ** -0.5, QUAD).reshape(1, QW)
    o = _attention(hn.reshape(b, s, d), w_q, w_k, w_v, gq, b0, b1, b2)
    (h,) = _ffn(h, norm_g[1, 2], w_in, w_out, pre="attn", pre_args=(o.reshape(b * s, d), att_w_o[0].astype(BF16)))
    return h.reshape(b, s, d)
```

```python
import math

import numpy as np
import jax
import jax.numpy as jnp
from jax import lax
from jax.experimental import pallas as pl
from jax.experimental.pallas import tpu as pltpu

F32 = jnp.float32
BF16 = jnp.bfloat16

D_MODEL = 1024
RMS_EPS = 1e-6
D_FF = 2816
FFN_RES = 0.5
D_RNN = 1280
N_RNN_BLOCKS = 10
RNN_BLOCK = 128
CONV_WIDTH = 4
LRU_C = 8.0
GELU_C0 = math.sqrt(2.0 / math.pi)
GELU_C1 = 0.044715 * GELU_C0
HEAD_DIM = 64
N_HEADS = 16
DILATION_GROUPS = ((128, 1), (512, 4), (2048, 16))
N_GROUPS = 3
N_BUCKETS = 32
MAX_DISTANCE = 2048
BLK = 128
RESIDUES = 16

VMEM_LIMIT_BYTES = 56 * 1024 * 1024
FFN_TM = 1024
FFN_CHUNK = 256
RNN_TS = 128
NEG = -1e30
PAD = 32
LANES = 128
BF16_ROWS = 16
ATT_CHUNK = 512
CAST_SLABS = 16
QUAD = 4
QW = QUAD * HEAD_DIM


def _rms(x, g):
    return x * lax.rsqrt(jnp.mean(x * x, axis=-1, keepdims=True) + RMS_EPS) * g


def _dot(a, b):
    return jnp.dot(a, b, preferred_element_type=F32)


def _cast_specs(jobs, steps):
    in_specs, args, out_shape, out_specs = [], [], [], []
    assert CAST_SLABS <= steps
    clamp = lambda i: jnp.minimum(i, CAST_SLABS - 1)
    for arr, lead, col_block, n_cols in jobs:
        n_rows = arr.shape[-2]
        rows = n_rows // CAST_SLABS
        assert rows * CAST_SLABS == n_rows and rows % BF16_ROWS == 0 and arr.ndim == len(lead) + 2
        in_specs.append(pl.BlockSpec((None,) * len(lead) + (rows, n_cols),
                                     lambda i, lead=lead, cb=col_block: (*lead, clamp(i), cb)))
        args.append(arr)
        out_shape.append(jax.ShapeDtypeStruct((n_rows, n_cols), BF16))
        out_specs.append(pl.BlockSpec((rows, n_cols), lambda i: (clamp(i), 0)))
    return in_specs, args, out_shape, out_specs


def _cast_run(in_refs, out_refs):
    for src, dst in zip(in_refs, out_refs):
        dst[...] = src[...].astype(BF16)


def _ffn_body(*refs, pre, post, n_cast):
    refs = list(refs)
    x_ref = refs.pop(0)
    if pre == "rnn":
        d_ref = refs.pop(0)
    if pre == "attn":
        ao_ref, wo_ref = refs.pop(0), refs.pop(0)
    g_ref, win_ref, wout_ref = refs.pop(0), refs.pop(0), refs.pop(0)
    if post:
        g2_ref = refs.pop(0)
    cast_in = [refs.pop(0) for _ in range(n_cast)]
    o_ref = refs.pop(0)
    if post:
        xn_ref = refs.pop(0)
    _cast_run(cast_in, [refs.pop(0) for _ in range(n_cast)])
    (a_scr,) = refs
    nb = x_ref.shape[0]
    x = x_ref[...].reshape(FFN_TM, D_MODEL)
    if pre == "rnn":
        x = x + pltpu.einshape("tbd->btd", d_ref[0].reshape(RNN_TS, nb, D_MODEL)).reshape(FFN_TM, D_MODEL)
    if pre == "attn":
        x = x + _dot(ao_ref[...], wo_ref[...])
    xn = _rms(x, g_ref[...]).astype(BF16)
    for c in range(D_FF // FFN_CHUNK):
        lo = c * FFN_CHUNK
        gate = _dot(xn, win_ref[:, lo:lo + FFN_CHUNK])
        up = _dot(xn, win_ref[:, D_FF + lo:D_FF + lo + FFN_CHUNK])
        a_scr[:, lo:lo + FFN_CHUNK] = (gate * jax.nn.sigmoid(gate) * up).astype(BF16)
    y = x + FFN_RES * _dot(a_scr[...], wout_ref[...])
    o_ref[...] = y.reshape(o_ref.shape)
    if post == "rnn":
        yn = _rms(y, g2_ref[...]).astype(BF16).reshape(nb, RNN_TS, D_MODEL)
        xn_ref[0] = pltpu.einshape("btd->tbd", yn).reshape(FFN_TM, D_MODEL)
    if post == "attn":
        yn = _rms(y, g2_ref[...]).astype(BF16).reshape(FFN_TM // RESIDUES, RESIDUES, D_MODEL)
        xn_ref[0] = pltpu.einshape("lrd->rld", yn)


def _ffn(x, g, w_in, w_out, pre=None, pre_args=(), post=None, g_next=None, cast=()):
    const = lambda i: (0, 0)
    row = lambda i: (i, 0)
    if x.ndim == 3:
        nb, seq, _ = x.shape
        assert nb * RNN_TS == FFN_TM
        steps = seq // RNN_TS
        x_spec = pl.BlockSpec((nb, RNN_TS, D_MODEL), lambda i: (0, i, 0))
    else:
        assert pre != "rnn" and post != "rnn"
        steps = x.shape[0] // FFN_TM
        x_spec = pl.BlockSpec((FFN_TM, D_MODEL), row)
    tm_spec = pl.BlockSpec((1, FFN_TM, D_MODEL), lambda i: (i, 0, 0))
    in_specs = [x_spec]
    if pre == "rnn":
        in_specs.append(tm_spec)
    if pre == "attn":
        in_specs += [pl.BlockSpec((FFN_TM, D_MODEL), row),
                     pl.BlockSpec((D_MODEL, D_MODEL), const, pipeline_mode=pl.Buffered(1))]
    in_specs += [
        pl.BlockSpec((1, D_MODEL), const),
        pl.BlockSpec((D_MODEL, 2 * D_FF), const, pipeline_mode=pl.Buffered(1)),
        pl.BlockSpec((D_FF, D_MODEL), const, pipeline_mode=pl.Buffered(1)),
    ]
    args = [x, *pre_args, g.reshape(1, D_MODEL), w_in, w_out]
    out_shape = [jax.ShapeDtypeStruct(x.shape, F32)]
    out_specs = [x_spec]
    if post:
        in_specs.append(pl.BlockSpec((1, D_MODEL), const))
        args.append(g_next.reshape(1, D_MODEL))
    if post == "rnn":
        out_shape.append(jax.ShapeDtypeStruct((steps, FFN_TM, D_MODEL), BF16))
        out_specs.append(tm_spec)
    if post == "attn":
        seq = RESIDUES * BLK
        tiles = seq // FFN_TM
        out_shape.append(jax.ShapeDtypeStruct((x.shape[0] // seq, RESIDUES, BLK, D_MODEL), BF16))
        out_specs.append(pl.BlockSpec((1, RESIDUES, FFN_TM // RESIDUES, D_MODEL),
                                      lambda i: (i // tiles, 0, i % tiles, 0)))
    c_in, c_args, c_shape, c_out = _cast_specs(cast, steps)
    in_specs += c_in
    args += c_args
    out_shape += c_shape
    out_specs += c_out
    res = pl.pallas_call(
        lambda *refs: _ffn_body(*refs, pre=pre, post=post, n_cast=len(cast)),
        grid=(steps,),
        in_specs=in_specs,
        out_specs=out_specs,
        out_shape=out_shape,
        scratch_shapes=[pltpu.VMEM((FFN_TM, D_FF), BF16)],
        compiler_params=pltpu.CompilerParams(
            dimension_semantics=("arbitrary",), vmem_limit_bytes=VMEM_LIMIT_BYTES),
        name="ffn" + ("_pre_" + pre if pre else "") + ("_post_" + post if post else ""),
    )(*args)
    return res


def _rglru_body(xn_ref, win_ref, cw_ref, cb_ref, wax_ref, ba_ref, bx_ref, lam_ref, wout_ref, *rest, n_cast):
    cast_in, o_ref, cast_out = rest[:n_cast], rest[n_cast], rest[n_cast + 1:2 * n_cast + 1]
    u_scr, y_scr, h_scr = rest[2 * n_cast + 1:]
    _cast_run(cast_in, cast_out)
    nb = h_scr.shape[0]
    m = RNN_TS * nb
    hist = (CONV_WIDTH - 1) * nb

    @pl.when(pl.program_id(0) == 0)
    def _():
        u_scr[0:hist, :] = jnp.zeros((hist, D_RNN), F32)
        h_scr[...] = jnp.zeros_like(h_scr)

    xn = xn_ref[0]
    sp8 = LRU_C * jax.nn.softplus(-lam_ref[...])

    def project(n):
        w = jnp.concatenate([win_ref[:, n * RNN_BLOCK:(n + 1) * RNN_BLOCK],
                             win_ref[:, D_RNN + n * RNN_BLOCK:D_RNN + (n + 1) * RNN_BLOCK]], axis=1)
        return _dot(xn, w)

    def mix(n, gu):
        sl = slice(n * RNN_BLOCK, (n + 1) * RNN_BLOCK)
        u_scr[hist:hist + m, sl] = gu[:, RNN_BLOCK:]
        cn = cb_ref[:, sl]
        for k in range(CONV_WIDTH):
            cn = cn + u_scr[k * nb:k * nb + m, sl] * cw_ref[k:k + 1, sl]
        u_scr[0:hist, sl] = u_scr[m:m + hist, sl]
        ra = _dot(cn.astype(BF16), wax_ref[n])
        r = jax.nn.sigmoid(ra[:, :RNN_BLOCK] + ba_ref[:, sl])
        gi = jax.nn.sigmoid(ra[:, RNN_BLOCK:] + bx_ref[:, sl])
        nla = r * sp8[:, sl]
        a = jnp.exp(-nla)
        z = jnp.tanh(nla) * (1.0 + a * a)
        b = jnp.where(z > 0.0, z * lax.rsqrt(z), 0.0) * (gi * cn)
        h = h_scr[:, sl]
        hs = []
        for t in range(RNN_TS):
            h = a[t * nb:(t + 1) * nb] * h + b[t * nb:(t + 1) * nb]
            hs.append(h)
        h_scr[:, sl] = h
        gate = gu[:, :RNN_BLOCK]
        half = 0.5 * gate
        inner = gate * (GELU_C0 + GELU_C1 * (gate * gate))
        y_scr[:, sl] = (jnp.concatenate(hs, axis=0) * (half + half * jnp.tanh(inner))).astype(BF16)

    gu = project(0)
    for n in range(N_RNN_BLOCKS):
        gu_next = project(n + 1) if n + 1 < N_RNN_BLOCKS else None
        mix(n, gu)
        gu = gu_next
    o_ref[0] = _dot(y_scr[...], wout_ref[...])


def _rglru(xn, nb, w_in, conv_w, conv_b, wax, b_a, b_x, lam, w_out, cast=()):
    steps, m, d = xn.shape
    assert nb == 8 and m == nb * RNN_TS, "one f32 sublane group per time step"
    const2 = lambda i: (0, 0)
    tm_spec = pl.BlockSpec((1, m, d), lambda i: (i, 0, 0))
    c_in, c_args, c_shape, c_out = _cast_specs(cast, steps)
    return pl.pallas_call(
        lambda *refs: _rglru_body(*refs, n_cast=len(cast)),
        grid=(steps,),
        in_specs=[
            tm_spec,
            pl.BlockSpec((d, 2 * D_RNN), const2, pipeline_mode=pl.Buffered(1)),
            pl.BlockSpec((CONV_WIDTH, D_RNN), const2),
            pl.BlockSpec((1, D_RNN), const2),
            pl.BlockSpec((N_RNN_BLOCKS, RNN_BLOCK, 2 * RNN_BLOCK), lambda i: (0, 0, 0)),
            pl.BlockSpec((1, D_RNN), const2),
            pl.BlockSpec((1, D_RNN), const2),
            pl.BlockSpec((1, D_RNN), const2),
            pl.BlockSpec((D_RNN, d), const2, pipeline_mode=pl.Buffered(1)),
        ] + c_in,
        out_specs=[tm_spec] + c_out,
        out_shape=[jax.ShapeDtypeStruct((steps, m, d), F32)] + c_shape,
        scratch_shapes=[
            pltpu.VMEM(((CONV_WIDTH - 1) * nb + m, D_RNN), F32),
            pltpu.VMEM((m, D_RNN), BF16),
            pltpu.VMEM((nb, D_RNN), F32),
        ],
        compiler_params=pltpu.CompilerParams(
            dimension_semantics=("arbitrary",), vmem_limit_bytes=VMEM_LIMIT_BYTES),
        name="rglru",
    )(xn, w_in, conv_w, conv_b.reshape(1, D_RNN), wax,
      b_a.reshape(1, D_RNN), b_x.reshape(1, D_RNN), lam.reshape(1, D_RNN), w_out, *c_args)


def _t5_bucket(n):
    max_exact = N_BUCKETS // 2
    nf = np.maximum(n, 1).astype(np.float64)
    large = max_exact + (np.log(nf / max_exact) / math.log(MAX_DISTANCE / max_exact)
                         * (N_BUCKETS - max_exact)).astype(np.int64)
    large = np.minimum(large, N_BUCKETS - 1)
    return np.where(n < max_exact, n, large)


def _bias_codes():
    qi = np.arange(BLK)
    q_off = 16 * (qi % 8) + qi // 8
    kj = np.arange(2 * BLK)
    k_off = 16 * (kj % 16 - 8) + kj // 16
    dist = q_off[:, None] - k_off[None, :]
    band = (dist >= 0) & (dist <= BLK)
    code = np.where(band, _t5_bucket(np.maximum(dist, 0) * 1), -1)
    c0 = np.stack([np.where(k_off[None, :] >= 0, code, -1), code])
    q_off = 4 * (qi % 32) + qi // 32
    k_off = 4 * (kj % 64 - 32) + kj // 64
    dist = q_off[:, None] - k_off[None, :]
    band = (dist >= 0) & (dist <= BLK)
    code = np.where(band, _t5_bucket(np.maximum(dist, 0) * 4), -1)
    c1 = np.stack([np.where(k_off[None, :] >= 0, code, -1), code])
    dist = qi[:, None] - qi[None, :]
    c2 = np.where(dist >= 0, _t5_bucket(np.maximum(dist, 0) * 16), -1)
    return c0.astype(np.int32), c1.astype(np.int32), c2.astype(np.int32)


def _bias_body(tbl_ref, c0_ref, c1_ref, c2_ref, *rest, buckets, n_cast):
    cast_in, (b0_ref, b1_ref, b2_ref), cast_out = rest[:n_cast], rest[n_cast:n_cast + 3], rest[n_cast + 3:]
    h = pl.program_id(0)
    _cast_run(cast_in, cast_out)

    def tile(code, col, used):
        acc = jnp.full(code.shape, NEG, F32)
        for b in used:
            acc = jnp.where(code == b, tbl_ref[b, col], acc)
        return acc

    for c_ref, b_ref, g in ((c0_ref, b0_ref, 0), (c1_ref, b1_ref, 1)):
        general = tile(c_ref[1], g * N_HEADS + h, buckets[g])
        b_ref[1, 0] = general
        b_ref[0, 0] = jnp.where(c_ref[0] >= 0, general, NEG)
    b2_ref[0] = tile(c2_ref[...], 2 * N_HEADS + h, buckets[2])


def _bias_tiles(rel_bias, cast=()):
    c0, c1, c2 = _bias_codes()
    assert ((c0[0] < 0) | (c0[0] == c0[1])).all() and ((c1[0] < 0) | (c1[0] == c1[1])).all()
    buckets = tuple(tuple(int(b) for b in np.unique(c) if b >= 0) for c in (c0, c1, c2))
    full3 = lambda h: (0, 0, 0)
    c_in, c_args, c_shape, c_out = _cast_specs(cast, N_HEADS)
    return pl.pallas_call(
        lambda *refs: _bias_body(*refs, buckets=buckets, n_cast=len(cast)),
        grid=(N_HEADS,),
        in_specs=[
            pl.BlockSpec(memory_space=pltpu.SMEM),
            pl.BlockSpec((2, BLK, 2 * BLK), full3),
            pl.BlockSpec((2, BLK, 2 * BLK), full3),
            pl.BlockSpec((BLK, BLK), lambda h: (0, 0)),
        ] + c_in,
        out_specs=[
            pl.BlockSpec((2, 1, BLK, 2 * BLK), lambda h: (0, h, 0, 0)),
            pl.BlockSpec((2, 1, BLK, 2 * BLK), lambda h: (0, h, 0, 0)),
            pl.BlockSpec((1, BLK, BLK), lambda h: (h, 0, 0)),
        ] + c_out,
        out_shape=[
            jax.ShapeDtypeStruct((2, N_HEADS, BLK, 2 * BLK), F32),
            jax.ShapeDtypeStruct((2, N_HEADS, BLK, 2 * BLK), F32),
            jax.ShapeDtypeStruct((N_HEADS, BLK, BLK), F32),
        ] + c_shape,
        compiler_params=pltpu.CompilerParams(dimension_semantics=("arbitrary",)),
        name="attn_bias",
    )(rel_bias, jnp.asarray(c0), jnp.asarray(c1), jnp.asarray(c2), *c_args)


def _attn_body(*refs):
    xn_ref = refs[0]
    w_refs = refs[1:1 + 3 * N_GROUPS]
    gq_ref, bd_ref, b0_ref, b1_ref, b2_ref, o_ref, qs, ks, vs, m_st, l_st, acc_st = refs[1 + 3 * N_GROUPS:]
    seq = o_ref.shape[1]
    n_chunk = seq // ATT_CHUNK
    zpad = jnp.zeros((PAD, QW), F32)
    qs[0:PAD, :] = zpad
    ks[0:PAD, :] = zpad
    vs[0:PAD, :] = zpad
    is_a = lax.broadcasted_iota(jnp.int32, (BLK, LANES), 1) < HEAD_DIM

    def project(g):
        wg = jnp.concatenate([w_refs[t * N_GROUPS + g][...] for t in range(3)], axis=1)
        for c in range(n_chunk):
            y = _dot(xn_ref[0, ATT_CHUNK * c:ATT_CHUNK * (c + 1), :], wg)
            q, k, v = y[:, :QW], y[:, QW:2 * QW], y[:, 2 * QW:]
            ssq = _dot((q * q).astype(BF16), bd_ref[...])
            ssk = _dot((k * k).astype(BF16), bd_ref[...])
            rows = slice(PAD + ATT_CHUNK * c, PAD + ATT_CHUNK * (c + 1))
            qs[rows, :] = q * lax.rsqrt(ssq * (1.0 / HEAD_DIM) + RMS_EPS) * gq_ref[...]
            ks[rows, :] = k * lax.rsqrt(ssk * (1.0 / HEAD_DIM) + RMS_EPS)
            vs[rows, :] = v

    def gather(ref, starts, n, pad=PAD):
        return jnp.concatenate([ref[pl.ds(pad + st, n), :] for st in starts], axis=0)

    def attend(q_starts, q_n, k_starts, k_n, bias, init):
        q_blk = gather(qs, q_starts, q_n).astype(BF16)
        k_blk = gather(ks, k_starts, k_n).astype(BF16)
        v_blk = gather(vs, k_starts, k_n).astype(BF16)
        for sl in range(QW // LANES):
            cols = slice(LANES * sl, LANES * (sl + 1))
            q = q_blk[:, cols]
            q2 = jnp.concatenate([jnp.where(is_a, q, 0), jnp.where(is_a, 0, q)], axis=0)
            s = lax.dot_general(q2, k_blk[:, cols], (((1,), (1,)), ((), ())), preferred_element_type=F32)
            s = s + jnp.concatenate([bias(2 * sl), bias(2 * sl + 1)], axis=0)
            m = jnp.max(s, axis=-1, keepdims=True)
            p = jnp.exp(s - m)
            l = jnp.sum(p, axis=-1, keepdims=True)
            o2 = _dot(p.astype(BF16), v_blk[:, cols])
            mb = jnp.where(is_a, m[:BLK], m[BLK:])
            lb = jnp.where(is_a, l[:BLK], l[BLK:])
            ob = jnp.where(is_a, o2[:BLK], o2[BLK:])
            if not init:
                mo = gather(m_st, q_starts, q_n, 0)[:, cols]
                lo = gather(l_st, q_starts, q_n, 0)[:, cols]
                ao = gather(acc_st, q_starts, q_n, 0)[:, cols]
                mn = jnp.maximum(mo, mb)
                al = jnp.exp(mo - mn)
                be = jnp.exp(mb - mn)
                lb = al * lo + be * lb
                ob = al * ao + be * ob
                mb = mn
            for i, st in enumerate(q_starts):
                piece = slice(i * q_n, (i + 1) * q_n)
                m_st[pl.ds(st, q_n), cols] = mb[piece]
                l_st[pl.ds(st, q_n), cols] = lb[piece]
                acc_st[pl.ds(st, q_n), cols] = ob[piece]


    project(0)
    p0 = BLK // RESIDUES
    for n in range(BLK // p0):
        q_starts = [n * p0 + r * BLK for r in range(RESIDUES)]
        k_starts = [st - p0 for st in q_starts]
        attend(q_starts, p0, k_starts, 2 * p0, lambda h, var=min(n, 1): b0_ref[var, h], init=True)

    project(1)
    d1 = DILATION_GROUPS[1][1]
    p1 = BLK // d1
    for r4 in range(d1):
        for n in range(BLK // p1):
            q_starts = [r4 * BLK + n * p1 + d1 * j * BLK for j in range(RESIDUES // d1)]
            k_starts = [st - p1 for st in q_starts]
            attend(q_starts, p1, k_starts, 2 * p1, lambda h, var=min(n, 1): b1_ref[var, h], init=False)

    project(2)
    for r in range(seq // BLK):
        attend([r * BLK], BLK, [r * BLK], BLK, lambda h: b2_ref[h], init=False)

    o = (acc_st[...] / l_st[...]).reshape(RESIDUES, seq // RESIDUES, QW)
    o_ref[0] = pltpu.einshape("rlc->lrc", o).reshape(seq, QW).astype(BF16)


def _attention(xn, w_q, w_k, w_v, gq, b0, b1, b2):
    b, s, d = xn.shape
    assert s == RESIDUES * BLK, "group 2 must be one block per residue"
    nq = N_HEADS // QUAD
    bd = np.kron(np.eye(QUAD), np.ones((HEAD_DIM, HEAD_DIM))).astype(np.float32)
    w_specs = [pl.BlockSpec((d, QW), lambda i, j, g=g: (0, g * nq + j)) for _ in range(3) for g in range(N_GROUPS)]
    w_args = [w for w in (w_q, w_k, w_v) for _ in range(N_GROUPS)]
    return pl.pallas_call(
        _attn_body,
        grid=(b, nq),
        in_specs=[pl.BlockSpec((1, s, d), lambda i, j: (i, 0, 0))] + w_specs + [
            pl.BlockSpec((1, QW), lambda i, j: (0, 0)),
            pl.BlockSpec((QW, QW), lambda i, j: (0, 0)),
            pl.BlockSpec((2, QUAD, BLK, 2 * BLK), lambda i, j: (0, j, 0, 0)),
            pl.BlockSpec((2, QUAD, BLK, 2 * BLK), lambda i, j: (0, j, 0, 0)),
            pl.BlockSpec((QUAD, BLK, BLK), lambda i, j: (j, 0, 0)),
        ],
        out_specs=pl.BlockSpec((1, s, QW), lambda i, j: (i, 0, j)),
        out_shape=jax.ShapeDtypeStruct((b, s, d), BF16),
        scratch_shapes=[pltpu.VMEM((PAD + s, QW), F32)] * 3 + [pltpu.VMEM((s, QW), F32)] * 3,
        compiler_params=pltpu.CompilerParams(
            dimension_semantics=("arbitrary", "arbitrary"), vmem_limit_bytes=VMEM_LIMIT_BYTES),
        name="dilated_attn",
    )(xn, *w_args, gq, jnp.asarray(bd, BF16), b0, b1, b2)


def kernel(x, norm_g, ffn_w_in, ffn_w_out, rnn_w_in, rnn_conv_w, rnn_conv_b, rnn_w_a, rnn_b_a, rnn_w_x,
           rnn_b_x, rnn_lambda, rnn_w_out, att_w_qkv, att_q_gain, att_k_gain, att_w_o, rel_bias):
    b, s, d = x.shape
    ffn_cast = lambda layer, slot: [(ffn_w_in, (layer, slot), 0, 2 * D_FF), (ffn_w_out, (layer, slot), 0, d)]
    qkv_cast = lambda third: [(att_w_qkv, (0,), third, att_w_qkv.shape[-1] // 3)]
    b0, b1, b2, w_in, w_out = _bias_tiles(rel_bias, cast=ffn_cast(0, 0))

    rnn_cast = [(rnn_w_in, (0,), 0, 2 * D_RNN), (rnn_w_out, (0,), 0, d)]
    h, hn, w_in_a, w_out_a, w_rnn_in, w_rnn_out = _ffn(x, norm_g[0, 0], w_in, w_out, post="rnn", g_next=norm_g[0, 1],
                                                       cast=ffn_cast(0, 1) + rnn_cast)
    wax = jnp.concatenate([rnn_w_a[0], rnn_w_x[0]], axis=-1).astype(BF16)
    delta, w_in_b, w_out_b, w_q = _rglru(hn, b, w_rnn_in, rnn_conv_w[0], rnn_conv_b[0], wax, rnn_b_a[0], rnn_b_x[0],
                                         rnn_lambda[0], w_rnn_out, cast=ffn_cast(1, 0) + qkv_cast(0))
    h, w_k = _ffn(h, norm_g[0, 2], w_in_a, w_out_a, pre="rnn", pre_args=(delta,), cast=qkv_cast(1))

    h, hn, w_in, w_out, w_v = _ffn(h.reshape(b * s, d), norm_g[1, 0], w_in_b, w_out_b, post="attn",
                                   g_next=norm_g[1, 1], cast=ffn_cast(1, 1) + qkv_cast(2))
    gq = jnp.tile(att_q_gain[0] * att_k_gain[0] * HEAD_DIM ** -0.5, QUAD).reshape(1, QW)
    o = _attention(hn.reshape(b, s, d), w_q, w_k, w_v, gq, b0, b1, b2)
    (h,) = _ffn(h, norm_g[1, 2], w_in, w_out, pre="attn", pre_args=(o.reshape(b * s, d), att_w_o[0].astype(BF16)))
    return h.reshape(b, s, d)
```

```python
import math

import numpy as np
import jax
import jax.numpy as jnp
from jax import lax
from jax.experimental import pallas as pl
from jax.experimental.pallas import tpu as pltpu

F32 = jnp.float32
BF16 = jnp.bfloat16

D_MODEL = 1024
RMS_EPS = 1e-6
D_FF = 2816
FFN_RES = 0.5
D_RNN = 1280
N_RNN_BLOCKS = 10
RNN_BLOCK = 128
CONV_WIDTH = 4
LRU_C = 8.0
GELU_C0 = math.sqrt(2.0 / math.pi)
GELU_C1 = 0.044715 * GELU_C0
HEAD_DIM = 64
N_HEADS = 16
DILATION_GROUPS = ((128, 1), (512, 4), (2048, 16))
N_GROUPS = 3
N_BUCKETS = 32
MAX_DISTANCE = 2048
BLK = 128
RESIDUES = 16

VMEM_LIMIT_BYTES = 56 * 1024 * 1024
FFN_TM = 1024
FFN_CHUNK = 256
RNN_TS = 128
SCAN_SEGS = 2
NEG = -1e30
PAD = 32
LANES = 128
BF16_ROWS = 16
ATT_CHUNK = 512
CAST_SLABS = 16
QUAD = 4
QW = QUAD * HEAD_DIM


def _rms(x, g):
    return x * lax.rsqrt(jnp.mean(x * x, axis=-1, keepdims=True) + RMS_EPS) * g


def _dot(a, b):
    return jnp.dot(a, b, preferred_element_type=F32)


def _cast_specs(jobs, steps):
    in_specs, args, out_shape, out_specs = [], [], [], []
    assert CAST_SLABS <= steps
    clamp = lambda i: jnp.minimum(i, CAST_SLABS - 1)
    for arr, lead, col_block, n_cols in jobs:
        n_rows = arr.shape[-2]
        rows = n_rows // CAST_SLABS
        assert rows * CAST_SLABS == n_rows and rows % BF16_ROWS == 0 and arr.ndim == len(lead) + 2
        in_specs.append(pl.BlockSpec((None,) * len(lead) + (rows, n_cols),
                                     lambda i, lead=lead, cb=col_block: (*lead, clamp(i), cb)))
        args.append(arr)
        out_shape.append(jax.ShapeDtypeStruct((n_rows, n_cols), BF16))
        out_specs.append(pl.BlockSpec((rows, n_cols), lambda i: (clamp(i), 0)))
    return in_specs, args, out_shape, out_specs


def _cast_run(in_refs, out_refs):
    for src, dst in zip(in_refs, out_refs):
        dst[...] = src[...].astype(BF16)


def _ffn_body(*refs, pre, post, n_cast):
    refs = list(refs)
    x_ref = refs.pop(0)
    if pre == "rnn":
        d_ref = refs.pop(0)
    if pre == "attn":
        ao_ref, wo_ref = refs.pop(0), refs.pop(0)
    g_ref, win_ref, wout_ref = refs.pop(0), refs.pop(0), refs.pop(0)
    if post:
        g2_ref = refs.pop(0)
    cast_in = [refs.pop(0) for _ in range(n_cast)]
    o_ref = refs.pop(0)
    if post:
        xn_ref = refs.pop(0)
    _cast_run(cast_in, [refs.pop(0) for _ in range(n_cast)])
    (a_scr,) = refs
    nb = x_ref.shape[0]
    x = x_ref[...].reshape(FFN_TM, D_MODEL)
    if pre == "rnn":
        x = x + pltpu.einshape("tbd->btd", d_ref[0].reshape(RNN_TS, nb, D_MODEL)).reshape(FFN_TM, D_MODEL)
    if pre == "attn":
        x = x + _dot(ao_ref[...], wo_ref[...])
    xn = _rms(x, g_ref[...]).astype(BF16)
    for c in range(D_FF // FFN_CHUNK):
        lo = c * FFN_CHUNK
        gate = _dot(xn, win_ref[:, lo:lo + FFN_CHUNK])
        up = _dot(xn, win_ref[:, D_FF + lo:D_FF + lo + FFN_CHUNK])
        a_scr[:, lo:lo + FFN_CHUNK] = (gate * jax.nn.sigmoid(gate) * up).astype(BF16)
    y = x + FFN_RES * _dot(a_scr[...], wout_ref[...])
    o_ref[...] = y.reshape(o_ref.shape)
    if post == "rnn":
        yn = _rms(y, g2_ref[...]).astype(BF16).reshape(nb, RNN_TS, D_MODEL)
        xn_ref[0] = pltpu.einshape("btd->tbd", yn).reshape(FFN_TM, D_MODEL)
    if post == "attn":
        yn = _rms(y, g2_ref[...]).astype(BF16).reshape(FFN_TM // RESIDUES, RESIDUES, D_MODEL)
        xn_ref[0] = pltpu.einshape("lrd->rld", yn)


def _ffn(x, g, w_in, w_out, pre=None, pre_args=(), post=None, g_next=None, cast=()):
    const = lambda i: (0, 0)
    row = lambda i: (i, 0)
    if x.ndim == 3:
        nb, seq, _ = x.shape
        assert nb * RNN_TS == FFN_TM
        steps = seq // RNN_TS
        x_spec = pl.BlockSpec((nb, RNN_TS, D_MODEL), lambda i: (0, i, 0))
    else:
        assert pre != "rnn" and post != "rnn"
        steps = x.shape[0] // FFN_TM
        x_spec = pl.BlockSpec((FFN_TM, D_MODEL), row)
    tm_spec = pl.BlockSpec((1, FFN_TM, D_MODEL), lambda i: (i, 0, 0))
    in_specs = [x_spec]
    if pre == "rnn":
        in_specs.append(tm_spec)
    if pre == "attn":
        in_specs += [pl.BlockSpec((FFN_TM, D_MODEL), row),
                     pl.BlockSpec((D_MODEL, D_MODEL), const, pipeline_mode=pl.Buffered(1))]
    in_specs += [
        pl.BlockSpec((1, D_MODEL), const),
        pl.BlockSpec((D_MODEL, 2 * D_FF), const, pipeline_mode=pl.Buffered(1)),
        pl.BlockSpec((D_FF, D_MODEL), const, pipeline_mode=pl.Buffered(1)),
    ]
    args = [x, *pre_args, g.reshape(1, D_MODEL), w_in, w_out]
    out_shape = [jax.ShapeDtypeStruct(x.shape, F32)]
    out_specs = [x_spec]
    if post:
        in_specs.append(pl.BlockSpec((1, D_MODEL), const))
        args.append(g_next.reshape(1, D_MODEL))
    if post == "rnn":
        out_shape.append(jax.ShapeDtypeStruct((steps, FFN_TM, D_MODEL), BF16))
        out_specs.append(tm_spec)
    if post == "attn":
        seq = RESIDUES * BLK
        tiles = seq // FFN_TM
        out_shape.append(jax.ShapeDtypeStruct((x.shape[0] // seq, RESIDUES, BLK, D_MODEL), BF16))
        out_specs.append(pl.BlockSpec((1, RESIDUES, FFN_TM // RESIDUES, D_MODEL),
                                      lambda i: (i // tiles, 0, i % tiles, 0)))
    c_in, c_args, c_shape, c_out = _cast_specs(cast, steps)
    in_specs += c_in
    args += c_args
    out_shape += c_shape
    out_specs += c_out
    res = pl.pallas_call(
        lambda *refs: _ffn_body(*refs, pre=pre, post=post, n_cast=len(cast)),
        grid=(steps,),
        in_specs=in_specs,
        out_specs=out_specs,
        out_shape=out_shape,
        scratch_shapes=[pltpu.VMEM((FFN_TM, D_FF), BF16)],
        compiler_params=pltpu.CompilerParams(
            dimension_semantics=("arbitrary",), vmem_limit_bytes=VMEM_LIMIT_BYTES),
        name="ffn" + ("_pre_" + pre if pre else "") + ("_post_" + post if post else ""),
    )(*args)
    return res


def _rglru_body(xn_ref, win_ref, cw_ref, cb_ref, wax_ref, ba_ref, bx_ref, lam_ref, wout_ref, *rest, n_cast):
    cast_in, o_ref, cast_out = rest[:n_cast], rest[n_cast], rest[n_cast + 1:2 * n_cast + 1]
    u_scr, y_scr, h_scr = rest[2 * n_cast + 1:]
    _cast_run(cast_in, cast_out)
    nb = h_scr.shape[0]
    m = RNN_TS * nb
    hist = (CONV_WIDTH - 1) * nb

    @pl.when(pl.program_id(0) == 0)
    def _():
        u_scr[0:hist, :] = jnp.zeros((hist, D_RNN), F32)
        h_scr[...] = jnp.zeros_like(h_scr)

    xn = xn_ref[0]
    sp8 = LRU_C * jax.nn.softplus(-lam_ref[...])

    def project(n):
        w = jnp.concatenate([win_ref[:, n * RNN_BLOCK:(n + 1) * RNN_BLOCK],
                             win_ref[:, D_RNN + n * RNN_BLOCK:D_RNN + (n + 1) * RNN_BLOCK]], axis=1)
        return _dot(xn, w)

    def mix(n, gu):
        sl = slice(n * RNN_BLOCK, (n + 1) * RNN_BLOCK)
        u_scr[hist:hist + m, sl] = gu[:, RNN_BLOCK:]
        cn = cb_ref[:, sl]
        for k in range(CONV_WIDTH):
            cn = cn + u_scr[k * nb:k * nb + m, sl] * cw_ref[k:k + 1, sl]
        u_scr[0:hist, sl] = u_scr[m:m + hist, sl]
        ra = _dot(cn.astype(BF16), wax_ref[n])
        r = jax.nn.sigmoid(ra[:, :RNN_BLOCK] + ba_ref[:, sl])
        gi = jax.nn.sigmoid(ra[:, RNN_BLOCK:] + bx_ref[:, sl])
        nla = r * sp8[:, sl]
        a = jnp.exp(-nla)
        z = jnp.tanh(nla) * (1.0 + a * a)
        b = jnp.where(z > 0.0, z * lax.rsqrt(z), 0.0) * (gi * cn)
        seg_len = RNN_TS // SCAN_SEGS
        hz = [jnp.zeros((nb, RNN_BLOCK), F32) for _ in range(SCAN_SEGS)]
        pz = [jnp.ones((nb, RNN_BLOCK), F32) for _ in range(SCAN_SEGS)]
        hloc = [[] for _ in range(SCAN_SEGS)]
        ploc = [[] for _ in range(SCAN_SEGS)]
        for i in range(seg_len):
            for sg in range(SCAN_SEGS):
                t = sg * seg_len + i
                at = a[t * nb:(t + 1) * nb]
                hz[sg] = at * hz[sg] + b[t * nb:(t + 1) * nb]
                pz[sg] = at * pz[sg]
                hloc[sg].append(hz[sg])
                ploc[sg].append(pz[sg])
        h = h_scr[:, sl]
        hs = []
        for sg in range(SCAN_SEGS):
            hin = h
            for i in range(seg_len):
                hs.append(hloc[sg][i] + ploc[sg][i] * hin)
            h = hs[-1]
        h_scr[:, sl] = h
        gate = gu[:, :RNN_BLOCK]
        half = 0.5 * gate
        inner = gate * (GELU_C0 + GELU_C1 * (gate * gate))
        y_scr[:, sl] = (jnp.concatenate(hs, axis=0) * (half + half * jnp.tanh(inner))).astype(BF16)

    gu = project(0)
    for n in range(N_RNN_BLOCKS):
        gu_next = project(n + 1) if n + 1 < N_RNN_BLOCKS else None
        mix(n, gu)
        gu = gu_next
    o_ref[0] = _dot(y_scr[...], wout_ref[...])


def _rglru(xn, nb, w_in, conv_w, conv_b, wax, b_a, b_x, lam, w_out, cast=()):
    steps, m, d = xn.shape
    assert nb == 8 and m == nb * RNN_TS, "one f32 sublane group per time step"
    const2 = lambda i: (0, 0)
    tm_spec = pl.BlockSpec((1, m, d), lambda i: (i, 0, 0))
    c_in, c_args, c_shape, c_out = _cast_specs(cast, steps)
    return pl.pallas_call(
        lambda *refs: _rglru_body(*refs, n_cast=len(cast)),
        grid=(steps,),
        in_specs=[
            tm_spec,
            pl.BlockSpec((d, 2 * D_RNN), const2, pipeline_mode=pl.Buffered(1)),
            pl.BlockSpec((CONV_WIDTH, D_RNN), const2),
            pl.BlockSpec((1, D_RNN), const2),
            pl.BlockSpec((N_RNN_BLOCKS, RNN_BLOCK, 2 * RNN_BLOCK), lambda i: (0, 0, 0)),
            pl.BlockSpec((1, D_RNN), const2),
            pl.BlockSpec((1, D_RNN), const2),
            pl.BlockSpec((1, D_RNN), const2),
            pl.BlockSpec((D_RNN, d), const2, pipeline_mode=pl.Buffered(1)),
        ] + c_in,
        out_specs=[tm_spec] + c_out,
        out_shape=[jax.ShapeDtypeStruct((steps, m, d), F32)] + c_shape,
        scratch_shapes=[
            pltpu.VMEM(((CONV_WIDTH - 1) * nb + m, D_RNN), F32),
            pltpu.VMEM((m, D_RNN), BF16),
            pltpu.VMEM((nb, D_RNN), F32),
        ],
        compiler_params=pltpu.CompilerParams(
            dimension_semantics=("arbitrary",), vmem_limit_bytes=VMEM_LIMIT_BYTES),
        name="rglru",
    )(xn, w_in, conv_w, conv_b.reshape(1, D_RNN), wax,
      b_a.reshape(1, D_RNN), b_x.reshape(1, D_RNN), lam.reshape(1, D_RNN), w_out, *c_args)


def _t5_bucket(n):
    max_exact = N_BUCKETS // 2
    nf = np.maximum(n, 1).astype(np.float64)
    large = max_exact + (np.log(nf / max_exact) / math.log(MAX_DISTANCE / max_exact)
                         * (N_BUCKETS - max_exact)).astype(np.int64)
    large = np.minimum(large, N_BUCKETS - 1)
    return np.where(n < max_exact, n, large)


def _bias_codes():
    qi = np.arange(BLK)
    q_off = 16 * (qi % 8) + qi // 8
    kj = np.arange(2 * BLK)
    k_off = 16 * (kj % 16 - 8) + kj // 16
    dist = q_off[:, None] - k_off[None, :]
    band = (dist >= 0) & (dist <= BLK)
    code = np.where(band, _t5_bucket(np.maximum(dist, 0) * 1), -1)
    c0 = np.stack([np.where(k_off[None, :] >= 0, code, -1), code])
    q_off = 4 * (qi % 32) + qi // 32
    k_off = 4 * (kj % 64 - 32) + kj // 64
    dist = q_off[:, None] - k_off[None, :]
    band = (dist >= 0) & (dist <= BLK)
    code = np.where(band, _t5_bucket(np.maximum(dist, 0) * 4), -1)
    c1 = np.stack([np.where(k_off[None, :] >= 0, code, -1), code])
    dist = qi[:, None] - qi[None, :]
    c2 = np.where(dist >= 0, _t5_bucket(np.maximum(dist, 0) * 16), -1)
    return c0.astype(np.int32), c1.astype(np.int32), c2.astype(np.int32)


def _bias_body(tbl_ref, c0_ref, c1_ref, c2_ref, *rest, buckets, n_cast):
    cast_in, (b0_ref, b1_ref, b2_ref), cast_out = rest[:n_cast], rest[n_cast:n_cast + 3], rest[n_cast + 3:]
    h = pl.program_id(0)
    _cast_run(cast_in, cast_out)

    def tile(code, col, used):
        acc = jnp.full(code.shape, NEG, F32)
        for b in used:
            acc = jnp.where(code == b, tbl_ref[b, col], acc)
        return acc

    for c_ref, b_ref, g in ((c0_ref, b0_ref, 0), (c1_ref, b1_ref, 1)):
        general = tile(c_ref[1], g * N_HEADS + h, buckets[g])
        b_ref[1, 0] = general
        b_ref[0, 0] = jnp.where(c_ref[0] >= 0, general, NEG)
    b2_ref[0] = tile(c2_ref[...], 2 * N_HEADS + h, buckets[2])


def _bias_tiles(rel_bias, cast=()):
    c0, c1, c2 = _bias_codes()
    assert ((c0[0] < 0) | (c0[0] == c0[1])).all() and ((c1[0] < 0) | (c1[0] == c1[1])).all()
    buckets = tuple(tuple(int(b) for b in np.unique(c) if b >= 0) for c in (c0, c1, c2))
    full3 = lambda h: (0, 0, 0)
    c_in, c_args, c_shape, c_out = _cast_specs(cast, N_HEADS)
    return pl.pallas_call(
        lambda *refs: _bias_body(*refs, buckets=buckets, n_cast=len(cast)),
        grid=(N_HEADS,),
        in_specs=[
            pl.BlockSpec(memory_space=pltpu.SMEM),
            pl.BlockSpec((2, BLK, 2 * BLK), full3),
            pl.BlockSpec((2, BLK, 2 * BLK), full3),
            pl.BlockSpec((BLK, BLK), lambda h: (0, 0)),
        ] + c_in,
        out_specs=[
            pl.BlockSpec((2, 1, BLK, 2 * BLK), lambda h: (0, h, 0, 0)),
            pl.BlockSpec((2, 1, BLK, 2 * BLK), lambda h: (0, h, 0, 0)),
            pl.BlockSpec((1, BLK, BLK), lambda h: (h, 0, 0)),
        ] + c_out,
        out_shape=[
            jax.ShapeDtypeStruct((2, N_HEADS, BLK, 2 * BLK), F32),
            jax.ShapeDtypeStruct((2, N_HEADS, BLK, 2 * BLK), F32),
            jax.ShapeDtypeStruct((N_HEADS, BLK, BLK), F32),
        ] + c_shape,
        compiler_params=pltpu.CompilerParams(dimension_semantics=("arbitrary",)),
        name="attn_bias",
    )(rel_bias, jnp.asarray(c0), jnp.asarray(c1), jnp.asarray(c2), *c_args)


def _attn_body(*refs):
    xn_ref = refs[0]
    w_refs = refs[1:1 + 3 * N_GROUPS]
    gq_ref, bd_ref, b0_ref, b1_ref, b2_ref, o_ref, qs, ks, vs, m_st, l_st, acc_st = refs[1 + 3 * N_GROUPS:]
    seq = o_ref.shape[1]
    n_chunk = seq // ATT_CHUNK
    zpad = jnp.zeros((PAD, QW), F32)
    qs[0:PAD, :] = zpad
    ks[0:PAD, :] = zpad
    vs[0:PAD, :] = zpad
    is_a = lax.broadcasted_iota(jnp.int32, (BLK, LANES), 1) < HEAD_DIM

    def project(g):
        wg = jnp.concatenate([w_refs[t * N_GROUPS + g][...] for t in range(3)], axis=1)
        for c in range(n_chunk):
            y = _dot(xn_ref[0, ATT_CHUNK * c:ATT_CHUNK * (c + 1), :], wg)
            q, k, v = y[:, :QW], y[:, QW:2 * QW], y[:, 2 * QW:]
            ssq = _dot((q * q).astype(BF16), bd_ref[...])
            ssk = _dot((k * k).astype(BF16), bd_ref[...])
            rows = slice(PAD + ATT_CHUNK * c, PAD + ATT_CHUNK * (c + 1))
            qs[rows, :] = q * lax.rsqrt(ssq * (1.0 / HEAD_DIM) + RMS_EPS) * gq_ref[...]
            ks[rows, :] = k * lax.rsqrt(ssk * (1.0 / HEAD_DIM) + RMS_EPS)
            vs[rows, :] = v

    def gather(ref, starts, n, pad=PAD):
        return jnp.concatenate([ref[pl.ds(pad + st, n), :] for st in starts], axis=0)

    def attend(q_starts, q_n, k_starts, k_n, bias, init):
        q_blk = gather(qs, q_starts, q_n).astype(BF16)
        k_blk = gather(ks, k_starts, k_n).astype(BF16)
        v_blk = gather(vs, k_starts, k_n).astype(BF16)
        for sl in range(QW // LANES):
            cols = slice(LANES * sl, LANES * (sl + 1))
            q = q_blk[:, cols]
            q2 = jnp.concatenate([jnp.where(is_a, q, 0), jnp.where(is_a, 0, q)], axis=0)
            s = lax.dot_general(q2, k_blk[:, cols], (((1,), (1,)), ((), ())), preferred_element_type=F32)
            s = s + jnp.concatenate([bias(2 * sl), bias(2 * sl + 1)], axis=0)
            m = jnp.max(s, axis=-1, keepdims=True)
            p = jnp.exp(s - m)
            l = jnp.sum(p, axis=-1, keepdims=True)
            o2 = _dot(p.astype(BF16), v_blk[:, cols])
            mb = jnp.where(is_a, m[:BLK], m[BLK:])
            lb = jnp.where(is_a, l[:BLK], l[BLK:])
            ob = jnp.where(is_a, o2[:BLK], o2[BLK:])
            if not init:
                mo = gather(m_st, q_starts, q_n, 0)[:, cols]
                lo = gather(l_st, q_starts, q_n, 0)[:, cols]
                ao = gather(acc_st, q_starts, q_n, 0)[:, cols]
                mn = jnp.maximum(mo, mb)
                al = jnp.exp(mo - mn)
                be = jnp.exp(mb - mn)
                lb = al * lo + be * lb
                ob = al * ao + be * ob
                mb = mn
            for i, st in enumerate(q_starts):
                piece = slice(i * q_n, (i + 1) * q_n)
                m_st[pl.ds(st, q_n), cols] = mb[piece]
                l_st[pl.ds(st, q_n), cols] = lb[piece]
                acc_st[pl.ds(st, q_n), cols] = ob[piece]


    project(0)
    p0 = BLK // RESIDUES
    for n in range(BLK // p0):
        q_starts = [n * p0 + r * BLK for r in range(RESIDUES)]
        k_starts = [st - p0 for st in q_starts]
        attend(q_starts, p0, k_starts, 2 * p0, lambda h, var=min(n, 1): b0_ref[var, h], init=True)

    project(1)
    d1 = DILATION_GROUPS[1][1]
    p1 = BLK // d1
    for r4 in range(d1):
        for n in range(BLK // p1):
            q_starts = [r4 * BLK + n * p1 + d1 * j * BLK for j in range(RESIDUES // d1)]
            k_starts = [st - p1 for st in q_starts]
            attend(q_starts, p1, k_starts, 2 * p1, lambda h, var=min(n, 1): b1_ref[var, h], init=False)

    project(2)
    for r in range(seq // BLK):
        attend([r * BLK], BLK, [r * BLK], BLK, lambda h: b2_ref[h], init=False)

    o = (acc_st[...] / l_st[...]).reshape(RESIDUES, seq // RESIDUES, QW)
    o_ref[0] = pltpu.einshape("rlc->lrc", o).reshape(seq, QW).astype(BF16)


def _attention(xn, w_q, w_k, w_v, gq, b0, b1, b2):
    b, s, d = xn.shape
    assert s == RESIDUES * BLK, "group 2 must be one block per residue"
    nq = N_HEADS // QUAD
    bd = np.kron(np.eye(QUAD), np.ones((HEAD_DIM, HEAD_DIM))).astype(np.float32)
    w_specs = [pl.BlockSpec((d, QW), lambda i, j, g=g: (0, g * nq + j)) for _ in range(3) for g in range(N_GROUPS)]
    w_args = [w for w in (w_q, w_k, w_v) for _ in range(N_GROUPS)]
    return pl.pallas_call(
        _attn_body,
        grid=(b, nq),
        in_specs=[pl.BlockSpec((1, s, d), lambda i, j: (i, 0, 0))] + w_specs + [
            pl.BlockSpec((1, QW), lambda i, j: (0, 0)),
            pl.BlockSpec((QW, QW), lambda i, j: (0, 0)),
            pl.BlockSpec((2, QUAD, BLK, 2 * BLK), lambda i, j: (0, j, 0, 0)),
            pl.BlockSpec((2, QUAD, BLK, 2 * BLK), lambda i, j: (0, j, 0, 0)),
            pl.BlockSpec((QUAD, BLK, BLK), lambda i, j: (j, 0, 0)),
        ],
        out_specs=pl.BlockSpec((1, s, QW), lambda i, j: (i, 0, j)),
        out_shape=jax.ShapeDtypeStruct((b, s, d), BF16),
        scratch_shapes=[pltpu.VMEM((PAD + s, QW), F32)] * 3 + [pltpu.VMEM((s, QW), F32)] * 3,
        compiler_params=pltpu.CompilerParams(
            dimension_semantics=("arbitrary", "arbitrary"), vmem_limit_bytes=VMEM_LIMIT_BYTES),
        name="dilated_attn",
    )(xn, *w_args, gq, jnp.asarray(bd, BF16), b0, b1, b2)


def kernel(x, norm_g, ffn_w_in, ffn_w_out, rnn_w_in, rnn_conv_w, rnn_conv_b, rnn_w_a, rnn_b_a, rnn_w_x,
           rnn_b_x, rnn_lambda, rnn_w_out, att_w_qkv, att_q_gain, att_k_gain, att_w_o, rel_bias):
    b, s, d = x.shape
    ffn_cast = lambda layer, slot: [(ffn_w_in, (layer, slot), 0, 2 * D_FF), (ffn_w_out, (layer, slot), 0, d)]
    qkv_cast = lambda third: [(att_w_qkv, (0,), third, att_w_qkv.shape[-1] // 3)]
    b0, b1, b2, w_in, w_out = _bias_tiles(rel_bias, cast=ffn_cast(0, 0))

    rnn_cast = [(rnn_w_in, (0,), 0, 2 * D_RNN), (rnn_w_out, (0,), 0, d)]
    h, hn, w_in_a, w_out_a, w_rnn_in, w_rnn_out = _ffn(x, norm_g[0, 0], w_in, w_out, post="rnn", g_next=norm_g[0, 1],
                                                       cast=ffn_cast(0, 1) + rnn_cast)
    wax = jnp.concatenate([rnn_w_a[0], rnn_w_x[0]], axis=-1).astype(BF16)
    delta, w_in_b, w_out_b, w_q = _rglru(hn, b, w_rnn_in, rnn_conv_w[0], rnn_conv_b[0], wax, rnn_b_a[0], rnn_b_x[0],
                                         rnn_lambda[0], w_rnn_out, cast=ffn_cast(1, 0) + qkv_cast(0))
    h, w_k = _ffn(h, norm_g[0, 2], w_in_a, w_out_a, pre="rnn", pre_args=(delta,), cast=qkv_cast(1))

    h, hn, w_in, w_out, w_v = _ffn(h.reshape(b * s, d), norm_g[1, 0], w_in_b, w_out_b, post="attn",
                                   g_next=norm_g[1, 1], cast=ffn_cast(1, 1) + qkv_cast(2))
    gq = jnp.tile(att_q_gain[0] * att_k_gain[0] * HEAD_DIM ** -0.5, QUAD).reshape(1, QW)
    o = _attention(hn.reshape(b, s, d), w_q, w_k, w_v, gq, b0, b1, b2)
    (h,) = _ffn(h, norm_g[1, 2], w_in, w_out, pre="attn", pre_args=(o.reshape(b * s, d), att_w_o[0].astype(BF16)))
    return h.reshape(b, s, d)
```

```python
import math

import numpy as np
import jax
import jax.numpy as jnp
from jax import lax
from jax.experimental import pallas as pl
from jax.experimental.pallas import tpu as pltpu

F32 = jnp.float32
BF16 = jnp.bfloat16

D_MODEL = 1024
RMS_EPS = 1e-6
D_FF = 2816
FFN_RES = 0.5
D_RNN = 1280
N_RNN_BLOCKS = 10
RNN_BLOCK = 128
CONV_WIDTH = 4
LRU_C = 8.0
GELU_C0 = math.sqrt(2.0 / math.pi)
GELU_C1 = 0.044715 * GELU_C0
HEAD_DIM = 64
N_HEADS = 16
DILATION_GROUPS = ((128, 1), (512, 4), (2048, 16))
N_GROUPS = 3
N_BUCKETS = 32
MAX_DISTANCE = 2048
BLK = 128
RESIDUES = 16

VMEM_LIMIT_BYTES = 56 * 1024 * 1024
FFN_TM = 1024
FFN_CHUNK = 256
EPI_SPLIT = 4
RNN_TS = 128
NEG = -1e30
PAD = 32
LANES = 128
BF16_ROWS = 16
ATT_CHUNK = 512
CAST_SLABS = 16
QUAD = 4
QW = QUAD * HEAD_DIM


def _rms(x, g):
    return x * lax.rsqrt(jnp.mean(x * x, axis=-1, keepdims=True) + RMS_EPS) * g


def _dot(a, b):
    return jnp.dot(a, b, preferred_element_type=F32)


def _cast_specs(jobs, steps):
    in_specs, args, out_shape, out_specs = [], [], [], []
    assert CAST_SLABS <= steps
    clamp = lambda i: jnp.minimum(i, CAST_SLABS - 1)
    for arr, lead, col_block, n_cols in jobs:
        n_rows = arr.shape[-2]
        rows = n_rows // CAST_SLABS
        assert rows * CAST_SLABS == n_rows and rows % BF16_ROWS == 0 and arr.ndim == len(lead) + 2
        in_specs.append(pl.BlockSpec((None,) * len(lead) + (rows, n_cols),
                                     lambda i, lead=lead, cb=col_block: (*lead, clamp(i), cb)))
        args.append(arr)
        out_shape.append(jax.ShapeDtypeStruct((n_rows, n_cols), BF16))
        out_specs.append(pl.BlockSpec((rows, n_cols), lambda i: (clamp(i), 0)))
    return in_specs, args, out_shape, out_specs


def _cast_run(in_refs, out_refs):
    for src, dst in zip(in_refs, out_refs):
        dst[...] = src[...].astype(BF16)


def _ffn_body(*refs, pre, post, n_cast):
    refs = list(refs)
    x_ref = refs.pop(0)
    if pre == "rnn":
        d_ref = refs.pop(0)
    if pre == "attn":
        ao_ref, wo_ref = refs.pop(0), refs.pop(0)
    g_ref, win_ref, wout_ref = refs.pop(0), refs.pop(0), refs.pop(0)
    if post:
        g2_ref = refs.pop(0)
    cast_in = [refs.pop(0) for _ in range(n_cast)]
    o_ref = refs.pop(0)
    if post:
        xn_ref = refs.pop(0)
    _cast_run(cast_in, [refs.pop(0) for _ in range(n_cast)])
    (a_scr,) = refs
    nb = x_ref.shape[0]
    x = x_ref[...].reshape(FFN_TM, D_MODEL)
    if pre == "rnn":
        x = x + pltpu.einshape("tbd->btd", d_ref[0].reshape(RNN_TS, nb, D_MODEL)).reshape(FFN_TM, D_MODEL)
    if pre == "attn":
        x = x + _dot(ao_ref[...], wo_ref[...])
    xn = _rms(x, g_ref[...]).astype(BF16)
    for c in range(D_FF // FFN_CHUNK):
        lo = c * FFN_CHUNK
        gate = _dot(xn, win_ref[:, lo:lo + FFN_CHUNK])
        up = _dot(xn, win_ref[:, D_FF + lo:D_FF + lo + FFN_CHUNK])
        a_scr[:, lo:lo + FFN_CHUNK] = (gate * jax.nn.sigmoid(gate) * up).astype(BF16)
    if post == "attn":
        rg = FFN_TM // EPI_SPLIT
        for q in range(EPI_SPLIT):
            rows = slice(q * rg, (q + 1) * rg)
            y = x[rows] + FFN_RES * _dot(a_scr[rows, :], wout_ref[...])
            o_ref[rows, :] = y
            yn = _rms(y, g2_ref[...]).astype(BF16).reshape(rg // RESIDUES, RESIDUES, D_MODEL)
            xn_ref[0, :, q * (rg // RESIDUES):(q + 1) * (rg // RESIDUES), :] = pltpu.einshape("lrd->rld", yn)
        return
    y = x + FFN_RES * _dot(a_scr[...], wout_ref[...])
    o_ref[...] = y.reshape(o_ref.shape)
    if post == "rnn":
        yn = _rms(y, g2_ref[...]).astype(BF16).reshape(nb, RNN_TS, D_MODEL)
        xn_ref[0] = pltpu.einshape("btd->tbd", yn).reshape(FFN_TM, D_MODEL)


def _ffn(x, g, w_in, w_out, pre=None, pre_args=(), post=None, g_next=None, cast=()):
    const = lambda i: (0, 0)
    row = lambda i: (i, 0)
    if x.ndim == 3:
        nb, seq, _ = x.shape
        assert nb * RNN_TS == FFN_TM
        steps = seq // RNN_TS
        x_spec = pl.BlockSpec((nb, RNN_TS, D_MODEL), lambda i: (0, i, 0))
    else:
        assert pre != "rnn" and post != "rnn"
        steps = x.shape[0] // FFN_TM
        x_spec = pl.BlockSpec((FFN_TM, D_MODEL), row)
    tm_spec = pl.BlockSpec((1, FFN_TM, D_MODEL), lambda i: (i, 0, 0))
    in_specs = [x_spec]
    if pre == "rnn":
        in_specs.append(tm_spec)
    if pre == "attn":
        in_specs += [pl.BlockSpec((FFN_TM, D_MODEL), row),
                     pl.BlockSpec((D_MODEL, D_MODEL), const, pipeline_mode=pl.Buffered(1))]
    in_specs += [
        pl.BlockSpec((1, D_MODEL), const),
        pl.BlockSpec((D_MODEL, 2 * D_FF), const, pipeline_mode=pl.Buffered(1)),
        pl.BlockSpec((D_FF, D_MODEL), const, pipeline_mode=pl.Buffered(1)),
    ]
    args = [x, *pre_args, g.reshape(1, D_MODEL), w_in, w_out]
    out_shape = [jax.ShapeDtypeStruct(x.shape, F32)]
    out_specs = [x_spec]
    if post:
        in_specs.append(pl.BlockSpec((1, D_MODEL), const))
        args.append(g_next.reshape(1, D_MODEL))
    if post == "rnn":
        out_shape.append(jax.ShapeDtypeStruct((steps, FFN_TM, D_MODEL), BF16))
        out_specs.append(tm_spec)
    if post == "attn":
        seq = RESIDUES * BLK
        tiles = seq // FFN_TM
        out_shape.append(jax.ShapeDtypeStruct((x.shape[0] // seq, RESIDUES, BLK, D_MODEL), BF16))
        out_specs.append(pl.BlockSpec((1, RESIDUES, FFN_TM // RESIDUES, D_MODEL),
                                      lambda i: (i // tiles, 0, i % tiles, 0)))
    c_in, c_args, c_shape, c_out = _cast_specs(cast, steps)
    in_specs += c_in
    args += c_args
    out_shape += c_shape
    out_specs += c_out
    res = pl.pallas_call(
        lambda *refs: _ffn_body(*refs, pre=pre, post=post, n_cast=len(cast)),
        grid=(steps,),
        in_specs=in_specs,
        out_specs=out_specs,
        out_shape=out_shape,
        scratch_shapes=[pltpu.VMEM((FFN_TM, D_FF), BF16)],
        compiler_params=pltpu.CompilerParams(
            dimension_semantics=("arbitrary",), vmem_limit_bytes=VMEM_LIMIT_BYTES),
        name="ffn" + ("_pre_" + pre if pre else "") + ("_post_" + post if post else ""),
    )(*args)
    return res


def _rglru_body(xn_ref, win_ref, cw_ref, cb_ref, wax_ref, ba_ref, bx_ref, lam_ref, wout_ref, *rest, n_cast):
    cast_in, o_ref, cast_out = rest[:n_cast], rest[n_cast], rest[n_cast + 1:2 * n_cast + 1]
    u_scr, y_scr, h_scr = rest[2 * n_cast + 1:]
    _cast_run(cast_in, cast_out)
    nb = h_scr.shape[0]
    m = RNN_TS * nb
    hist = (CONV_WIDTH - 1) * nb

    @pl.when(pl.program_id(0) == 0)
    def _():
        u_scr[0:hist, :] = jnp.zeros((hist, D_RNN), F32)
        h_scr[...] = jnp.zeros_like(h_scr)

    xn = xn_ref[0]
    sp8 = LRU_C * jax.nn.softplus(-lam_ref[...])

    def project(n):
        w = jnp.concatenate([win_ref[:, n * RNN_BLOCK:(n + 1) * RNN_BLOCK],
                             win_ref[:, D_RNN + n * RNN_BLOCK:D_RNN + (n + 1) * RNN_BLOCK]], axis=1)
        return _dot(xn, w)

    def mix(n, gu):
        sl = slice(n * RNN_BLOCK, (n + 1) * RNN_BLOCK)
        u_scr[hist:hist + m, sl] = gu[:, RNN_BLOCK:]
        cn = cb_ref[:, sl]
        for k in range(CONV_WIDTH):
            cn = cn + u_scr[k * nb:k * nb + m, sl] * cw_ref[k:k + 1, sl]
        u_scr[0:hist, sl] = u_scr[m:m + hist, sl]
        ra = _dot(cn.astype(BF16), wax_ref[n])
        r = jax.nn.sigmoid(ra[:, :RNN_BLOCK] + ba_ref[:, sl])
        gi = jax.nn.sigmoid(ra[:, RNN_BLOCK:] + bx_ref[:, sl])
        nla = r * sp8[:, sl]
        a = jnp.exp(-nla)
        z = jnp.tanh(nla) * (1.0 + a * a)
        b = jnp.where(z > 0.0, z * lax.rsqrt(z), 0.0) * (gi * cn)
        h = h_scr[:, sl]
        hs = []
        for t in range(RNN_TS):
            h = a[t * nb:(t + 1) * nb] * h + b[t * nb:(t + 1) * nb]
            hs.append(h)
        h_scr[:, sl] = h
        gate = gu[:, :RNN_BLOCK]
        half = 0.5 * gate
        inner = gate * (GELU_C0 + GELU_C1 * (gate * gate))
        y_scr[:, sl] = (jnp.concatenate(hs, axis=0) * (half + half * jnp.tanh(inner))).astype(BF16)

    gu = project(0)
    for n in range(N_RNN_BLOCKS):
        gu_next = project(n + 1) if n + 1 < N_RNN_BLOCKS else None
        mix(n, gu)
        gu = gu_next
    o_ref[0] = _dot(y_scr[...], wout_ref[...])


def _rglru(xn, nb, w_in, conv_w, conv_b, wax, b_a, b_x, lam, w_out, cast=()):
    steps, m, d = xn.shape
    assert nb == 8 and m == nb * RNN_TS, "one f32 sublane group per time step"
    const2 = lambda i: (0, 0)
    tm_spec = pl.BlockSpec((1, m, d), lambda i: (i, 0, 0))
    c_in, c_args, c_shape, c_out = _cast_specs(cast, steps)
    return pl.pallas_call(
        lambda *refs: _rglru_body(*refs, n_cast=len(cast)),
        grid=(steps,),
        in_specs=[
            tm_spec,
            pl.BlockSpec((d, 2 * D_RNN), const2, pipeline_mode=pl.Buffered(1)),
            pl.BlockSpec((CONV_WIDTH, D_RNN), const2),
            pl.BlockSpec((1, D_RNN), const2),
            pl.BlockSpec((N_RNN_BLOCKS, RNN_BLOCK, 2 * RNN_BLOCK), lambda i: (0, 0, 0)),
            pl.BlockSpec((1, D_RNN), const2),
            pl.BlockSpec((1, D_RNN), const2),
            pl.BlockSpec((1, D_RNN), const2),
            pl.BlockSpec((D_RNN, d), const2, pipeline_mode=pl.Buffered(1)),
        ] + c_in,
        out_specs=[tm_spec] + c_out,
        out_shape=[jax.ShapeDtypeStruct((steps, m, d), F32)] + c_shape,
        scratch_shapes=[
            pltpu.VMEM(((CONV_WIDTH - 1) * nb + m, D_RNN), F32),
            pltpu.VMEM((m, D_RNN), BF16),
            pltpu.VMEM((nb, D_RNN), F32),
        ],
        compiler_params=pltpu.CompilerParams(
            dimension_semantics=("arbitrary",), vmem_limit_bytes=VMEM_LIMIT_BYTES),
        name="rglru",
    )(xn, w_in, conv_w, conv_b.reshape(1, D_RNN), wax,
      b_a.reshape(1, D_RNN), b_x.reshape(1, D_RNN), lam.reshape(1, D_RNN), w_out, *c_args)


def _t5_bucket(n):
    max_exact = N_BUCKETS // 2
    nf = np.maximum(n, 1).astype(np.float64)
    large = max_exact + (np.log(nf / max_exact) / math.log(MAX_DISTANCE / max_exact)
                         * (N_BUCKETS - max_exact)).astype(np.int64)
    large = np.minimum(large, N_BUCKETS - 1)
    return np.where(n < max_exact, n, large)


def _bias_codes():
    qi = np.arange(BLK)
    q_off = 16 * (qi % 8) + qi // 8
    kj = np.arange(2 * BLK)
    k_off = 16 * (kj % 16 - 8) + kj // 16
    dist = q_off[:, None] - k_off[None, :]
    band = (dist >= 0) & (dist <= BLK)
    code = np.where(band, _t5_bucket(np.maximum(dist, 0) * 1), -1)
    c0 = np.stack([np.where(k_off[None, :] >= 0, code, -1), code])
    q_off = 4 * (qi % 32) + qi // 32
    k_off = 4 * (kj % 64 - 32) + kj // 64
    dist = q_off[:, None] - k_off[None, :]
    band = (dist >= 0) & (dist <= BLK)
    code = np.where(band, _t5_bucket(np.maximum(dist, 0) * 4), -1)
    c1 = np.stack([np.where(k_off[None, :] >= 0, code, -1), code])
    dist = qi[:, None] - qi[None, :]
    c2 = np.where(dist >= 0, _t5_bucket(np.maximum(dist, 0) * 16), -1)
    return c0.astype(np.int32), c1.astype(np.int32), c2.astype(np.int32)


def _bias_body(tbl_ref, c0_ref, c1_ref, c2_ref, *rest, buckets, n_cast):
    cast_in, (b0_ref, b1_ref, b2_ref), cast_out = rest[:n_cast], rest[n_cast:n_cast + 3], rest[n_cast + 3:]
    h = pl.program_id(0)
    _cast_run(cast_in, cast_out)

    def tile(code, col, used):
        acc = jnp.full(code.shape, NEG, F32)
        for b in used:
            acc = jnp.where(code == b, tbl_ref[b, col], acc)
        return acc

    for c_ref, b_ref, g in ((c0_ref, b0_ref, 0), (c1_ref, b1_ref, 1)):
        general = tile(c_ref[1], g * N_HEADS + h, buckets[g])
        b_ref[1, 0] = general
        b_ref[0, 0] = jnp.where(c_ref[0] >= 0, general, NEG)
    b2_ref[0] = tile(c2_ref[...], 2 * N_HEADS + h, buckets[2])


def _bias_tiles(rel_bias, cast=()):
    c0, c1, c2 = _bias_codes()
    assert ((c0[0] < 0) | (c0[0] == c0[1])).all() and ((c1[0] < 0) | (c1[0] == c1[1])).all()
    buckets = tuple(tuple(int(b) for b in np.unique(c) if b >= 0) for c in (c0, c1, c2))
    full3 = lambda h: (0, 0, 0)
    c_in, c_args, c_shape, c_out = _cast_specs(cast, N_HEADS)
    return pl.pallas_call(
        lambda *refs: _bias_body(*refs, buckets=buckets, n_cast=len(cast)),
        grid=(N_HEADS,),
        in_specs=[
            pl.BlockSpec(memory_space=pltpu.SMEM),
            pl.BlockSpec((2, BLK, 2 * BLK), full3),
            pl.BlockSpec((2, BLK, 2 * BLK), full3),
            pl.BlockSpec((BLK, BLK), lambda h: (0, 0)),
        ] + c_in,
        out_specs=[
            pl.BlockSpec((2, 1, BLK, 2 * BLK), lambda h: (0, h, 0, 0)),
            pl.BlockSpec((2, 1, BLK, 2 * BLK), lambda h: (0, h, 0, 0)),
            pl.BlockSpec((1, BLK, BLK), lambda h: (h, 0, 0)),
        ] + c_out,
        out_shape=[
            jax.ShapeDtypeStruct((2, N_HEADS, BLK, 2 * BLK), F32),
            jax.ShapeDtypeStruct((2, N_HEADS, BLK, 2 * BLK), F32),
            jax.ShapeDtypeStruct((N_HEADS, BLK, BLK), F32),
        ] + c_shape,
        compiler_params=pltpu.CompilerParams(dimension_semantics=("arbitrary",)),
        name="attn_bias",
    )(rel_bias, jnp.asarray(c0), jnp.asarray(c1), jnp.asarray(c2), *c_args)


def _attn_body(*refs):
    xn_ref = refs[0]
    w_refs = refs[1:1 + 3 * N_GROUPS]
    gq_ref, bd_ref, b0_ref, b1_ref, b2_ref, o_ref, qs, ks, vs, m_st, l_st, acc_st = refs[1 + 3 * N_GROUPS:]
    seq = o_ref.shape[1]
    n_chunk = seq // ATT_CHUNK
    zpad = jnp.zeros((PAD, QW), F32)
    qs[0:PAD, :] = zpad
    ks[0:PAD, :] = zpad
    vs[0:PAD, :] = zpad
    is_a = lax.broadcasted_iota(jnp.int32, (BLK, LANES), 1) < HEAD_DIM

    def project(g):
        wg = jnp.concatenate([w_refs[t * N_GROUPS + g][...] for t in range(3)], axis=1)
        for c in range(n_chunk):
            y = _dot(xn_ref[0, ATT_CHUNK * c:ATT_CHUNK * (c + 1), :], wg)
            q, k, v = y[:, :QW], y[:, QW:2 * QW], y[:, 2 * QW:]
            ssq = _dot((q * q).astype(BF16), bd_ref[...])
            ssk = _dot((k * k).astype(BF16), bd_ref[...])
            rows = slice(PAD + ATT_CHUNK * c, PAD + ATT_CHUNK * (c + 1))
            qs[rows, :] = q * lax.rsqrt(ssq * (1.0 / HEAD_DIM) + RMS_EPS) * gq_ref[...]
            ks[rows, :] = k * lax.rsqrt(ssk * (1.0 / HEAD_DIM) + RMS_EPS)
            vs[rows, :] = v

    def gather(ref, starts, n, pad=PAD):
        return jnp.concatenate([ref[pl.ds(pad + st, n), :] for st in starts], axis=0)

    def attend(q_starts, q_n, k_starts, k_n, bias, init):
        q_blk = gather(qs, q_starts, q_n).astype(BF16)
        k_blk = gather(ks, k_starts, k_n).astype(BF16)
        v_blk = gather(vs, k_starts, k_n).astype(BF16)
        scores = []
        for sl in range(QW // LANES):
            cols = slice(LANES * sl, LANES * (sl + 1))
            q = q_blk[:, cols]
            q2 = jnp.concatenate([jnp.where(is_a, q, 0), jnp.where(is_a, 0, q)], axis=0)
            scores.append(lax.dot_general(q2, k_blk[:, cols], (((1,), (1,)), ((), ())),
                                          preferred_element_type=F32))
        for sl in range(QW // LANES):
            cols = slice(LANES * sl, LANES * (sl + 1))
            s = scores[sl] + jnp.concatenate([bias(2 * sl), bias(2 * sl + 1)], axis=0)
            m = jnp.max(s, axis=-1, keepdims=True)
            p = jnp.exp(s - m)
            l = jnp.sum(p, axis=-1, keepdims=True)
            o2 = _dot(p.astype(BF16), v_blk[:, cols])
            mb = jnp.where(is_a, m[:BLK], m[BLK:])
            lb = jnp.where(is_a, l[:BLK], l[BLK:])
            ob = jnp.where(is_a, o2[:BLK], o2[BLK:])
            if not init:
                mo = gather(m_st, q_starts, q_n, 0)[:, cols]
                lo = gather(l_st, q_starts, q_n, 0)[:, cols]
                ao = gather(acc_st, q_starts, q_n, 0)[:, cols]
                mn = jnp.maximum(mo, mb)
                al = jnp.exp(mo - mn)
                be = jnp.exp(mb - mn)
                lb = al * lo + be * lb
                ob = al * ao + be * ob
                mb = mn
            for i, st in enumerate(q_starts):
                piece = slice(i * q_n, (i + 1) * q_n)
                m_st[pl.ds(st, q_n), cols] = mb[piece]
                l_st[pl.ds(st, q_n), cols] = lb[piece]
                acc_st[pl.ds(st, q_n), cols] = ob[piece]


    project(0)
    p0 = BLK // RESIDUES
    for n in range(BLK // p0):
        q_starts = [n * p0 + r * BLK for r in range(RESIDUES)]
        k_starts = [st - p0 for st in q_starts]
        attend(q_starts, p0, k_starts, 2 * p0, lambda h, var=min(n, 1): b0_ref[var, h], init=True)

    project(1)
    d1 = DILATION_GROUPS[1][1]
    p1 = BLK // d1
    for r4 in range(d1):
        for n in range(BLK // p1):
            q_starts = [r4 * BLK + n * p1 + d1 * j * BLK for j in range(RESIDUES // d1)]
            k_starts = [st - p1 for st in q_starts]
            attend(q_starts, p1, k_starts, 2 * p1, lambda h, var=min(n, 1): b1_ref[var, h], init=False)

    project(2)
    for r in range(seq // BLK):
        attend([r * BLK], BLK, [r * BLK], BLK, lambda h: b2_ref[h], init=False)

    o = (acc_st[...] / l_st[...]).astype(BF16).reshape(RESIDUES, seq // RESIDUES, QW)
    o_ref[0] = pltpu.einshape("rlc->lrc", o).reshape(seq, QW)


def _attention(xn, w_q, w_k, w_v, gq, b0, b1, b2):
    b, s, d = xn.shape
    assert s == RESIDUES * BLK, "group 2 must be one block per residue"
    nq = N_HEADS // QUAD
    bd = np.kron(np.eye(QUAD), np.ones((HEAD_DIM, HEAD_DIM))).astype(np.float32)
    w_specs = [pl.BlockSpec((d, QW), lambda i, j, g=g: (0, g * nq + j)) for _ in range(3) for g in range(N_GROUPS)]
    w_args = [w for w in (w_q, w_k, w_v) for _ in range(N_GROUPS)]
    return pl.pallas_call(
        _attn_body,
        grid=(b, nq),
        in_specs=[pl.BlockSpec((1, s, d), lambda i, j: (i, 0, 0))] + w_specs + [
            pl.BlockSpec((1, QW), lambda i, j: (0, 0)),
            pl.BlockSpec((QW, QW), lambda i, j: (0, 0)),
            pl.BlockSpec((2, QUAD, BLK, 2 * BLK), lambda i, j: (0, j, 0, 0)),
            pl.BlockSpec((2, QUAD, BLK, 2 * BLK), lambda i, j: (0, j, 0, 0)),
            pl.BlockSpec((QUAD, BLK, BLK), lambda i, j: (j, 0, 0)),
        ],
        out_specs=pl.BlockSpec((1, s, QW), lambda i, j: (i, 0, j)),
        out_shape=jax.ShapeDtypeStruct((b, s, d), BF16),
        scratch_shapes=[pltpu.VMEM((PAD + s, QW), F32)] * 3 + [pltpu.VMEM((s, QW), F32)] * 3,
        compiler_params=pltpu.CompilerParams(
            dimension_semantics=("arbitrary", "arbitrary"), vmem_limit_bytes=VMEM_LIMIT_BYTES),
        name="dilated_attn",
    )(xn, *w_args, gq, jnp.asarray(bd, BF16), b0, b1, b2)


def kernel(x, norm_g, ffn_w_in, ffn_w_out, rnn_w_in, rnn_conv_w, rnn_conv_b, rnn_w_a, rnn_b_a, rnn_w_x,
           rnn_b_x, rnn_lambda, rnn_w_out, att_w_qkv, att_q_gain, att_k_gain, att_w_o, rel_bias):
    b, s, d = x.shape
    ffn_cast = lambda layer, slot: [(ffn_w_in, (layer, slot), 0, 2 * D_FF), (ffn_w_out, (layer, slot), 0, d)]
    qkv_cast = lambda third: [(att_w_qkv, (0,), third, att_w_qkv.shape[-1] // 3)]
    b0, b1, b2, w_in, w_out = _bias_tiles(rel_bias, cast=ffn_cast(0, 0))

    rnn_cast = [(rnn_w_in, (0,), 0, 2 * D_RNN), (rnn_w_out, (0,), 0, d)]
    h, hn, w_in_a, w_out_a, w_rnn_in, w_rnn_out = _ffn(x, norm_g[0, 0], w_in, w_out, post="rnn", g_next=norm_g[0, 1],
                                                       cast=ffn_cast(0, 1) + rnn_cast)
    wax = jnp.concatenate([rnn_w_a[0], rnn_w_x[0]], axis=-1).astype(BF16)
    delta, w_in_b, w_out_b, w_q = _rglru(hn, b, w_rnn_in, rnn_conv_w[0], rnn_conv_b[0], wax, rnn_b_a[0], rnn_b_x[0],
                                         rnn_lambda[0], w_rnn_out, cast=ffn_cast(1, 0) + qkv_cast(0))
    h, w_k = _ffn(h, norm_g[0, 2], w_in_a, w_out_a, pre="rnn", pre_args=(delta,), cast=qkv_cast(1))

    h, hn, w_in, w_out, w_v = _ffn(h.reshape(b * s, d), norm_g[1, 0], w_in_b, w_out_b, post="attn",
                                   g_next=norm_g[1, 1], cast=ffn_cast(1, 1) + qkv_cast(2))
    gq = jnp.tile(att_q_gain[0] * att_k_gain[0] * HEAD_DIM ** -0.5, QUAD).reshape(1, QW)
    o = _attention(hn.reshape(b, s, d), w_q, w_k, w_v, gq, b0, b1, b2)
    (h,) = _ffn(h, norm_g[1, 2], w_in, w_out, pre="attn", pre_args=(o.reshape(b * s, d), att_w_o[0].astype(BF16)))
    return h.reshape(b, s, d)
```

```python
import math

import numpy as np
import jax
import jax.numpy as jnp
from jax import lax
from jax.experimental import pallas as pl
from jax.experimental.pallas import tpu as pltpu

F32 = jnp.float32
BF16 = jnp.bfloat16

D_MODEL = 1024
RMS_EPS = 1e-6
D_FF = 2816
FFN_RES = 0.5
D_RNN = 1280
N_RNN_BLOCKS = 10
RNN_BLOCK = 128
CONV_WIDTH = 4
LRU_C = 8.0
GELU_C0 = math.sqrt(2.0 / math.pi)
GELU_C1 = 0.044715 * GELU_C0
HEAD_DIM = 64
N_HEADS = 16
DILATION_GROUPS = ((128, 1), (512, 4), (2048, 16))
N_GROUPS = 3
N_BUCKETS = 32
MAX_DISTANCE = 2048
BLK = 128
RESIDUES = 16

VMEM_LIMIT_BYTES = 56 * 1024 * 1024
ATT_VMEM_LIMIT_BYTES = 63 * 1024 * 1024
FFN_TM = 1024
FFN_CHUNK = 256
RNN_TS = 128
NEG = -1e30
PAD = 32
LANES = 128
BF16_ROWS = 16
ATT_CHUNK = 512
CAST_SLABS = 16
QUAD = 4
QW = QUAD * HEAD_DIM


def _rms(x, g):
    return x * lax.rsqrt(jnp.mean(x * x, axis=-1, keepdims=True) + RMS_EPS) * g


def _dot(a, b):
    return jnp.dot(a, b, preferred_element_type=F32)


def _cast_specs(jobs, steps):
    in_specs, args, out_shape, out_specs = [], [], [], []
    assert CAST_SLABS <= steps
    clamp = lambda i: jnp.minimum(i, CAST_SLABS - 1)
    for arr, lead, col_block, n_cols in jobs:
        n_rows = arr.shape[-2]
        rows = n_rows // CAST_SLABS
        assert rows * CAST_SLABS == n_rows and rows % BF16_ROWS == 0 and arr.ndim == len(lead) + 2
        in_specs.append(pl.BlockSpec((None,) * len(lead) + (rows, n_cols),
                                     lambda i, lead=lead, cb=col_block: (*lead, clamp(i), cb)))
        args.append(arr)
        out_shape.append(jax.ShapeDtypeStruct((n_rows, n_cols), BF16))
        out_specs.append(pl.BlockSpec((rows, n_cols), lambda i: (clamp(i), 0)))
    return in_specs, args, out_shape, out_specs


def _cast_run(in_refs, out_refs):
    for src, dst in zip(in_refs, out_refs):
        dst[...] = src[...].astype(BF16)


def _ffn_body(*refs, pre, post, n_cast):
    refs = list(refs)
    x_ref = refs.pop(0)
    if pre == "rnn":
        d_ref = refs.pop(0)
    if pre == "attn":
        ao_ref, wo_ref = refs.pop(0), refs.pop(0)
    g_ref, win_ref, wout_ref = refs.pop(0), refs.pop(0), refs.pop(0)
    if post:
        g2_ref = refs.pop(0)
    cast_in = [refs.pop(0) for _ in range(n_cast)]
    o_ref = refs.pop(0)
    if post:
        xn_ref = refs.pop(0)
    _cast_run(cast_in, [refs.pop(0) for _ in range(n_cast)])
    (a_scr,) = refs
    nb = x_ref.shape[0]
    x = x_ref[...].reshape(FFN_TM, D_MODEL)
    if pre == "rnn":
        x = x + pltpu.einshape("tbd->btd", d_ref[0].reshape(RNN_TS, nb, D_MODEL)).reshape(FFN_TM, D_MODEL)
    if pre == "attn":
        x = x + _dot(ao_ref[...], wo_ref[...])
    xn = _rms(x, g_ref[...]).astype(BF16)
    for c in range(D_FF // FFN_CHUNK):
        lo = c * FFN_CHUNK
        gate = _dot(xn, win_ref[:, lo:lo + FFN_CHUNK])
        up = _dot(xn, win_ref[:, D_FF + lo:D_FF + lo + FFN_CHUNK])
        a_scr[:, lo:lo + FFN_CHUNK] = (gate * jax.nn.sigmoid(gate) * up).astype(BF16)
    y = x + FFN_RES * _dot(a_scr[...], wout_ref[...])
    o_ref[...] = y.reshape(o_ref.shape)
    if post == "rnn":
        yn = _rms(y, g2_ref[...]).astype(BF16).reshape(nb, RNN_TS, D_MODEL)
        xn_ref[0] = pltpu.einshape("btd->tbd", yn).reshape(FFN_TM, D_MODEL)
    if post == "attn":
        yn = _rms(y, g2_ref[...]).astype(BF16).reshape(FFN_TM // RESIDUES, RESIDUES, D_MODEL)
        xn_ref[0] = pltpu.einshape("lrd->rld", yn)


def _ffn(x, g, w_in, w_out, pre=None, pre_args=(), post=None, g_next=None, cast=()):
    const = lambda i: (0, 0)
    row = lambda i: (i, 0)
    if x.ndim == 3:
        nb, seq, _ = x.shape
        assert nb * RNN_TS == FFN_TM
        steps = seq // RNN_TS
        x_spec = pl.BlockSpec((nb, RNN_TS, D_MODEL), lambda i: (0, i, 0))
    else:
        assert pre != "rnn" and post != "rnn"
        steps = x.shape[0] // FFN_TM
        x_spec = pl.BlockSpec((FFN_TM, D_MODEL), row)
    tm_spec = pl.BlockSpec((1, FFN_TM, D_MODEL), lambda i: (i, 0, 0))
    in_specs = [x_spec]
    if pre == "rnn":
        in_specs.append(tm_spec)
    if pre == "attn":
        in_specs += [pl.BlockSpec((FFN_TM, D_MODEL), row),
                     pl.BlockSpec((D_MODEL, D_MODEL), const, pipeline_mode=pl.Buffered(1))]
    in_specs += [
        pl.BlockSpec((1, D_MODEL), const),
        pl.BlockSpec((D_MODEL, 2 * D_FF), const, pipeline_mode=pl.Buffered(1)),
        pl.BlockSpec((D_FF, D_MODEL), const, pipeline_mode=pl.Buffered(1)),
    ]
    args = [x, *pre_args, g.reshape(1, D_MODEL), w_in, w_out]
    out_shape = [jax.ShapeDtypeStruct(x.shape, F32)]
    out_specs = [x_spec]
    if post:
        in_specs.append(pl.BlockSpec((1, D_MODEL), const))
        args.append(g_next.reshape(1, D_MODEL))
    if post == "rnn":
        out_shape.append(jax.ShapeDtypeStruct((steps, FFN_TM, D_MODEL), BF16))
        out_specs.append(tm_spec)
    if post == "attn":
        seq = RESIDUES * BLK
        tiles = seq // FFN_TM
        out_shape.append(jax.ShapeDtypeStruct((x.shape[0] // seq, RESIDUES, BLK, D_MODEL), BF16))
        out_specs.append(pl.BlockSpec((1, RESIDUES, FFN_TM // RESIDUES, D_MODEL),
                                      lambda i: (i // tiles, 0, i % tiles, 0)))
    c_in, c_args, c_shape, c_out = _cast_specs(cast, steps)
    in_specs += c_in
    args += c_args
    out_shape += c_shape
    out_specs += c_out
    res = pl.pallas_call(
        lambda *refs: _ffn_body(*refs, pre=pre, post=post, n_cast=len(cast)),
        grid=(steps,),
        in_specs=in_specs,
        out_specs=out_specs,
        out_shape=out_shape,
        scratch_shapes=[pltpu.VMEM((FFN_TM, D_FF), BF16)],
        compiler_params=pltpu.CompilerParams(
            dimension_semantics=("arbitrary",), vmem_limit_bytes=VMEM_LIMIT_BYTES),
        name="ffn" + ("_pre_" + pre if pre else "") + ("_post_" + post if post else ""),
    )(*args)
    return res


def _rglru_body(xn_ref, win_ref, cw_ref, cb_ref, wax_ref, ba_ref, bx_ref, lam_ref, wout_ref, *rest, n_cast):
    cast_in, o_ref, cast_out = rest[:n_cast], rest[n_cast], rest[n_cast + 1:2 * n_cast + 1]
    u_scr, y_scr, h_scr = rest[2 * n_cast + 1:]
    _cast_run(cast_in, cast_out)
    nb = h_scr.shape[0]
    m = RNN_TS * nb
    hist = (CONV_WIDTH - 1) * nb

    @pl.when(pl.program_id(0) == 0)
    def _():
        u_scr[0:hist, :] = jnp.zeros((hist, D_RNN), F32)
        h_scr[...] = jnp.zeros_like(h_scr)

    xn = xn_ref[0]
    sp8 = LRU_C * jax.nn.softplus(-lam_ref[...])

    def project(n):
        w = jnp.concatenate([win_ref[:, n * RNN_BLOCK:(n + 1) * RNN_BLOCK],
                             win_ref[:, D_RNN + n * RNN_BLOCK:D_RNN + (n + 1) * RNN_BLOCK]], axis=1)
        return _dot(xn, w)

    def mix(n, gu):
        sl = slice(n * RNN_BLOCK, (n + 1) * RNN_BLOCK)
        u_scr[hist:hist + m, sl] = gu[:, RNN_BLOCK:]
        cn = cb_ref[:, sl]
        for k in range(CONV_WIDTH):
            cn = cn + u_scr[k * nb:k * nb + m, sl] * cw_ref[k:k + 1, sl]
        u_scr[0:hist, sl] = u_scr[m:m + hist, sl]
        ra = _dot(cn.astype(BF16), wax_ref[n])
        r = jax.nn.sigmoid(ra[:, :RNN_BLOCK] + ba_ref[:, sl])
        gi = jax.nn.sigmoid(ra[:, RNN_BLOCK:] + bx_ref[:, sl])
        nla = r * sp8[:, sl]
        a = jnp.exp(-nla)
        z = jnp.tanh(nla) * (1.0 + a * a)
        b = jnp.where(z > 0.0, z * lax.rsqrt(z), 0.0) * (gi * cn)
        h = h_scr[:, sl]
        hs = []
        for t in range(RNN_TS):
            h = a[t * nb:(t + 1) * nb] * h + b[t * nb:(t + 1) * nb]
            hs.append(h)
        h_scr[:, sl] = h
        gate = gu[:, :RNN_BLOCK]
        half = 0.5 * gate
        inner = gate * (GELU_C0 + GELU_C1 * (gate * gate))
        y_scr[:, sl] = (jnp.concatenate(hs, axis=0) * (half + half * jnp.tanh(inner))).astype(BF16)

    gu = project(0)
    for n in range(N_RNN_BLOCKS):
        gu_next = project(n + 1) if n + 1 < N_RNN_BLOCKS else None
        mix(n, gu)
        gu = gu_next
    o_ref[0] = _dot(y_scr[...], wout_ref[...])


def _rglru(xn, nb, w_in, conv_w, conv_b, wax, b_a, b_x, lam, w_out, cast=()):
    steps, m, d = xn.shape
    assert nb == 8 and m == nb * RNN_TS, "one f32 sublane group per time step"
    const2 = lambda i: (0, 0)
    tm_spec = pl.BlockSpec((1, m, d), lambda i: (i, 0, 0))
    c_in, c_args, c_shape, c_out = _cast_specs(cast, steps)
    return pl.pallas_call(
        lambda *refs: _rglru_body(*refs, n_cast=len(cast)),
        grid=(steps,),
        in_specs=[
            tm_spec,
            pl.BlockSpec((d, 2 * D_RNN), const2, pipeline_mode=pl.Buffered(1)),
            pl.BlockSpec((CONV_WIDTH, D_RNN), const2),
            pl.BlockSpec((1, D_RNN), const2),
            pl.BlockSpec((N_RNN_BLOCKS, RNN_BLOCK, 2 * RNN_BLOCK), lambda i: (0, 0, 0)),
            pl.BlockSpec((1, D_RNN), const2),
            pl.BlockSpec((1, D_RNN), const2),
            pl.BlockSpec((1, D_RNN), const2),
            pl.BlockSpec((D_RNN, d), const2, pipeline_mode=pl.Buffered(1)),
        ] + c_in,
        out_specs=[tm_spec] + c_out,
        out_shape=[jax.ShapeDtypeStruct((steps, m, d), F32)] + c_shape,
        scratch_shapes=[
            pltpu.VMEM(((CONV_WIDTH - 1) * nb + m, D_RNN), F32),
            pltpu.VMEM((m, D_RNN), BF16),
            pltpu.VMEM((nb, D_RNN), F32),
        ],
        compiler_params=pltpu.CompilerParams(
            dimension_semantics=("arbitrary",), vmem_limit_bytes=VMEM_LIMIT_BYTES),
        name="rglru",
    )(xn, w_in, conv_w, conv_b.reshape(1, D_RNN), wax,
      b_a.reshape(1, D_RNN), b_x.reshape(1, D_RNN), lam.reshape(1, D_RNN), w_out, *c_args)


def _t5_bucket(n):
    max_exact = N_BUCKETS // 2
    nf = np.maximum(n, 1).astype(np.float64)
    large = max_exact + (np.log(nf / max_exact) / math.log(MAX_DISTANCE / max_exact)
                         * (N_BUCKETS - max_exact)).astype(np.int64)
    large = np.minimum(large, N_BUCKETS - 1)
    return np.where(n < max_exact, n, large)


def _bias_codes():
    qi = np.arange(BLK)
    q_off = 16 * (qi % 8) + qi // 8
    kj = np.arange(2 * BLK)
    k_off = 16 * (kj % 16 - 8) + kj // 16
    dist = q_off[:, None] - k_off[None, :]
    band = (dist >= 0) & (dist <= BLK)
    code = np.where(band, _t5_bucket(np.maximum(dist, 0) * 1), -1)
    c0 = np.stack([np.where(k_off[None, :] >= 0, code, -1), code])
    q_off = 4 * (qi % 32) + qi // 32
    k_off = 4 * (kj % 64 - 32) + kj // 64
    dist = q_off[:, None] - k_off[None, :]
    band = (dist >= 0) & (dist <= BLK)
    code = np.where(band, _t5_bucket(np.maximum(dist, 0) * 4), -1)
    c1 = np.stack([np.where(k_off[None, :] >= 0, code, -1), code])
    dist = qi[:, None] - qi[None, :]
    c2 = np.where(dist >= 0, _t5_bucket(np.maximum(dist, 0) * 16), -1)
    return c0.astype(np.int32), c1.astype(np.int32), c2.astype(np.int32)


def _bias_body(tbl_ref, c0_ref, c1_ref, c2_ref, *rest, buckets, n_cast):
    cast_in, (b0_ref, b1_ref, b2_ref), cast_out = rest[:n_cast], rest[n_cast:n_cast + 3], rest[n_cast + 3:]
    h = pl.program_id(0)
    _cast_run(cast_in, cast_out)

    def tile(code, col, used):
        acc = jnp.full(code.shape, NEG, F32)
        for b in used:
            acc = jnp.where(code == b, tbl_ref[b, col], acc)
        return acc

    for c_ref, b_ref, g in ((c0_ref, b0_ref, 0), (c1_ref, b1_ref, 1)):
        general = tile(c_ref[1], g * N_HEADS + h, buckets[g])
        b_ref[1, 0] = general
        b_ref[0, 0] = jnp.where(c_ref[0] >= 0, general, NEG)
    b2_ref[0] = tile(c2_ref[...], 2 * N_HEADS + h, buckets[2])


def _bias_tiles(rel_bias, cast=()):
    c0, c1, c2 = _bias_codes()
    assert ((c0[0] < 0) | (c0[0] == c0[1])).all() and ((c1[0] < 0) | (c1[0] == c1[1])).all()
    buckets = tuple(tuple(int(b) for b in np.unique(c) if b >= 0) for c in (c0, c1, c2))
    full3 = lambda h: (0, 0, 0)
    c_in, c_args, c_shape, c_out = _cast_specs(cast, N_HEADS)
    return pl.pallas_call(
        lambda *refs: _bias_body(*refs, buckets=buckets, n_cast=len(cast)),
        grid=(N_HEADS,),
        in_specs=[
            pl.BlockSpec(memory_space=pltpu.SMEM),
            pl.BlockSpec((2, BLK, 2 * BLK), full3),
            pl.BlockSpec((2, BLK, 2 * BLK), full3),
            pl.BlockSpec((BLK, BLK), lambda h: (0, 0)),
        ] + c_in,
        out_specs=[
            pl.BlockSpec((2, 1, BLK, 2 * BLK), lambda h: (0, h, 0, 0)),
            pl.BlockSpec((2, 1, BLK, 2 * BLK), lambda h: (0, h, 0, 0)),
            pl.BlockSpec((1, BLK, BLK), lambda h: (h, 0, 0)),
        ] + c_out,
        out_shape=[
            jax.ShapeDtypeStruct((2, N_HEADS, BLK, 2 * BLK), F32),
            jax.ShapeDtypeStruct((2, N_HEADS, BLK, 2 * BLK), F32),
            jax.ShapeDtypeStruct((N_HEADS, BLK, BLK), F32),
        ] + c_shape,
        compiler_params=pltpu.CompilerParams(dimension_semantics=("arbitrary",)),
        name="attn_bias",
    )(rel_bias, jnp.asarray(c0), jnp.asarray(c1), jnp.asarray(c2), *c_args)


def _attn_body(*refs):
    xn_ref = refs[0]
    w_refs = refs[1:1 + 3 * N_GROUPS]
    wn_refs = refs[1 + 3 * N_GROUPS:4 + 3 * N_GROUPS]
    gq_ref, bd_ref, b0_ref, b1_ref, b2_ref, o_ref = refs[4 + 3 * N_GROUPS:10 + 3 * N_GROUPS]
    set0 = refs[10 + 3 * N_GROUPS:13 + 3 * N_GROUPS]
    set12 = refs[13 + 3 * N_GROUPS:16 + 3 * N_GROUPS]
    m_st, l_st, acc_st = refs[16 + 3 * N_GROUPS:]
    seq = o_ref.shape[1]
    n_chunk = seq // ATT_CHUNK
    zpad = jnp.zeros((PAD, QW), F32)
    for ref in (*set0, *set12):
        ref[0:PAD, :] = zpad.astype(ref.dtype)
    is_a = lax.broadcasted_iota(jnp.int32, (BLK, LANES), 1) < HEAD_DIM

    def project(w_qkv, dst):
        qs, ks, vs = dst
        wg = jnp.concatenate([w[...] for w in w_qkv], axis=1)
        for c in range(n_chunk):
            y = _dot(xn_ref[0, ATT_CHUNK * c:ATT_CHUNK * (c + 1), :], wg)
            q, k, v = y[:, :QW], y[:, QW:2 * QW], y[:, 2 * QW:]
            ssq = _dot((q * q).astype(BF16), bd_ref[...])
            ssk = _dot((k * k).astype(BF16), bd_ref[...])
            rows = slice(PAD + ATT_CHUNK * c, PAD + ATT_CHUNK * (c + 1))
            qs[rows, :] = (q * lax.rsqrt(ssq * (1.0 / HEAD_DIM) + RMS_EPS) * gq_ref[...]).astype(qs.dtype)
            ks[rows, :] = (k * lax.rsqrt(ssk * (1.0 / HEAD_DIM) + RMS_EPS)).astype(ks.dtype)
            vs[rows, :] = v.astype(vs.dtype)

    def gather(ref, starts, n, pad=PAD):
        return jnp.concatenate([ref[pl.ds(pad + st, n), :] for st in starts], axis=0)

    def attend(src, q_starts, q_n, k_starts, k_n, bias, init):
        qs, ks, vs = src
        q_blk = gather(qs, q_starts, q_n).astype(BF16)
        k_blk = gather(ks, k_starts, k_n).astype(BF16)
        v_blk = gather(vs, k_starts, k_n).astype(BF16)
        scores = []
        for sl in range(QW // LANES):
            cols = slice(LANES * sl, LANES * (sl + 1))
            q = q_blk[:, cols]
            q2 = jnp.concatenate([jnp.where(is_a, q, 0), jnp.where(is_a, 0, q)], axis=0)
            scores.append(lax.dot_general(q2, k_blk[:, cols], (((1,), (1,)), ((), ())),
                                          preferred_element_type=F32))
        for sl in range(QW // LANES):
            cols = slice(LANES * sl, LANES * (sl + 1))
            s = scores[sl] + jnp.concatenate([bias(2 * sl), bias(2 * sl + 1)], axis=0)
            m = jnp.max(s, axis=-1, keepdims=True)
            p = jnp.exp(s - m)
            l = jnp.sum(p, axis=-1, keepdims=True)
            o2 = _dot(p.astype(BF16), v_blk[:, cols])
            mb = jnp.where(is_a, m[:BLK], m[BLK:])
            lb = jnp.where(is_a, l[:BLK], l[BLK:])
            ob = jnp.where(is_a, o2[:BLK], o2[BLK:])
            if not init:
                mo = gather(m_st, q_starts, q_n, 0)[:, cols]
                lo = gather(l_st, q_starts, q_n, 0)[:, cols]
                ao = gather(acc_st, q_starts, q_n, 0)[:, cols]
                mn = jnp.maximum(mo, mb)
                al = jnp.exp(mo - mn)
                be = jnp.exp(mb - mn)
                lb = al * lo + be * lb
                ob = al * ao + be * ob
                mb = mn
            for i, st in enumerate(q_starts):
                piece = slice(i * q_n, (i + 1) * q_n)
                m_st[pl.ds(st, q_n), cols] = mb[piece]
                l_st[pl.ds(st, q_n), cols] = lb[piece]
                acc_st[pl.ds(st, q_n), cols] = ob[piece]

    group_w = lambda g: [w_refs[t * N_GROUPS + g] for t in range(3)]

    @pl.when(pl.program_id(1) == 0)
    def _():
        project(group_w(0), set0)

    p0 = BLK // RESIDUES
    for n in range(BLK // p0):
        q_starts = [n * p0 + r * BLK for r in range(RESIDUES)]
        k_starts = [st - p0 for st in q_starts]
        attend(set0, q_starts, p0, k_starts, 2 * p0, lambda h, var=min(n, 1): b0_ref[var, h], init=True)

    project(group_w(1), set12)
    d1 = DILATION_GROUPS[1][1]
    p1 = BLK // d1
    for r4 in range(d1):
        for n in range(BLK // p1):
            q_starts = [r4 * BLK + n * p1 + d1 * j * BLK for j in range(RESIDUES // d1)]
            k_starts = [st - p1 for st in q_starts]
            attend(set12, q_starts, p1, k_starts, 2 * p1, lambda h, var=min(n, 1): b1_ref[var, h], init=False)

    project(group_w(2), set12)
    for r in range(seq // BLK):
        attend(set12, [r * BLK], BLK, [r * BLK], BLK, lambda h: b2_ref[h], init=False)

    project(wn_refs, set0)

    o = (acc_st[...] / l_st[...]).astype(BF16).reshape(RESIDUES, seq // RESIDUES, QW)
    o_ref[0] = pltpu.einshape("rlc->lrc", o).reshape(seq, QW)


def _attention(xn, w_q, w_k, w_v, gq, b0, b1, b2):
    b, s, d = xn.shape
    assert s == RESIDUES * BLK, "group 2 must be one block per residue"
    nq = N_HEADS // QUAD
    bd = np.kron(np.eye(QUAD), np.ones((HEAD_DIM, HEAD_DIM))).astype(np.float32)
    w_specs = [pl.BlockSpec((d, QW), lambda i, j, g=g: (0, g * nq + j)) for _ in range(3) for g in range(N_GROUPS)]
    w_args = [w for w in (w_q, w_k, w_v) for _ in range(N_GROUPS)]
    w_specs += [pl.BlockSpec((d, QW), lambda i, j: (0, (j + 1) % nq), pipeline_mode=pl.Buffered(1))
                for _ in range(3)]
    w_args += [w_q, w_k, w_v]
    return pl.pallas_call(
        _attn_body,
        grid=(b, nq),
        in_specs=[pl.BlockSpec((1, s, d), lambda i, j: (i, 0, 0), pipeline_mode=pl.Buffered(1))] + w_specs + [
            pl.BlockSpec((1, QW), lambda i, j: (0, 0)),
            pl.BlockSpec((QW, QW), lambda i, j: (0, 0)),
            pl.BlockSpec((2, QUAD, BLK, 2 * BLK), lambda i, j: (0, j, 0, 0)),
            pl.BlockSpec((2, QUAD, BLK, 2 * BLK), lambda i, j: (0, j, 0, 0)),
            pl.BlockSpec((QUAD, BLK, BLK), lambda i, j: (j, 0, 0)),
        ],
        out_specs=pl.BlockSpec((1, s, QW), lambda i, j: (i, 0, j)),
        out_shape=jax.ShapeDtypeStruct((b, s, d), BF16),
        scratch_shapes=([pltpu.VMEM((PAD + s, QW), F32)] * 3 + [pltpu.VMEM((PAD + s, QW), BF16)] * 3
                        + [pltpu.VMEM((s, QW), F32)] * 3),
        compiler_params=pltpu.CompilerParams(
            dimension_semantics=("arbitrary", "arbitrary"), vmem_limit_bytes=ATT_VMEM_LIMIT_BYTES),
        name="dilated_attn",
    )(xn, *w_args, gq, jnp.asarray(bd, BF16), b0, b1, b2)


def kernel(x, norm_g, ffn_w_in, ffn_w_out, rnn_w_in, rnn_conv_w, rnn_conv_b, rnn_w_a, rnn_b_a, rnn_w_x,
           rnn_b_x, rnn_lambda, rnn_w_out, att_w_qkv, att_q_gain, att_k_gain, att_w_o, rel_bias):
    b, s, d = x.shape
    ffn_cast = lambda layer, slot: [(ffn_w_in, (layer, slot), 0, 2 * D_FF), (ffn_w_out, (layer, slot), 0, d)]
    qkv_cast = lambda third: [(att_w_qkv, (0,), third, att_w_qkv.shape[-1] // 3)]
    b0, b1, b2, w_in, w_out = _bias_tiles(rel_bias, cast=ffn_cast(0, 0))

    rnn_cast = [(rnn_w_in, (0,), 0, 2 * D_RNN), (rnn_w_out, (0,), 0, d)]
    h, hn, w_in_a, w_out_a, w_rnn_in, w_rnn_out = _ffn(x, norm_g[0, 0], w_in, w_out, post="rnn", g_next=norm_g[0, 1],
                                                       cast=ffn_cast(0, 1) + rnn_cast)
    wax = jnp.concatenate([rnn_w_a[0], rnn_w_x[0]], axis=-1).astype(BF16)
    delta, w_in_b, w_out_b, w_q = _rglru(hn, b, w_rnn_in, rnn_conv_w[0], rnn_conv_b[0], wax, rnn_b_a[0], rnn_b_x[0],
                                         rnn_lambda[0], w_rnn_out, cast=ffn_cast(1, 0) + qkv_cast(0))
    h, w_k = _ffn(h, norm_g[0, 2], w_in_a, w_out_a, pre="rnn", pre_args=(delta,), cast=qkv_cast(1))

    h, hn, w_in, w_out, w_v = _ffn(h.reshape(b * s, d), norm_g[1, 0], w_in_b, w_out_b, post="attn",
                                   g_next=norm_g[1, 1], cast=ffn_cast(1, 1) + qkv_cast(2))
    gq = jnp.tile(att_q_gain[0] * att_k_gain[0] * HEAD_DIM ** -0.5, QUAD).reshape(1, QW)
    o = _attention(hn.reshape(b, s, d), w_q, w_k, w_v, gq, b0, b1, b2)
    (h,) = _ffn(h, norm_g[1, 2], w_in, w_out, pre="attn", pre_args=(o.reshape(b * s, d), att_w_o[0].astype(BF16)))
    return h.reshape(b, s, d)
```

```python
import math

import numpy as np
import jax
import jax.numpy as jnp
from jax import lax
from jax.experimental import pallas as pl
from jax.experimental.pallas import tpu as pltpu

F32 = jnp.float32
BF16 = jnp.bfloat16

D_MODEL = 1024
RMS_EPS = 1e-6
D_FF = 2816
FFN_RES = 0.5
D_RNN = 1280
N_RNN_BLOCKS = 10
RNN_BLOCK = 128
CONV_WIDTH = 4
LRU_C = 8.0
GELU_C0 = math.sqrt(2.0 / math.pi)
GELU_C1 = 0.044715 * GELU_C0
HEAD_DIM = 64
N_HEADS = 16
DILATION_GROUPS = ((128, 1), (512, 4), (2048, 16))
N_GROUPS = 3
N_BUCKETS = 32
MAX_DISTANCE = 2048
BLK = 128
RESIDUES = 16

VMEM_LIMIT_BYTES = 56 * 1024 * 1024
FFN_TM = 1024
FFN_CHUNK = 256
RNN_TS = 128
NEG = -1e30
PAD = 32
LANES = 128
BF16_ROWS = 16
ATT_CHUNK = 512
CAST_SLABS = 16
QUAD = 4
QW = QUAD * HEAD_DIM


def _rms(x, g):
    return x * lax.rsqrt(jnp.mean(x * x, axis=-1, keepdims=True) + RMS_EPS) * g


def _dot(a, b):
    return jnp.dot(a, b, preferred_element_type=F32)


def _cast_specs(jobs, steps):
    in_specs, args, out_shape, out_specs = [], [], [], []
    assert CAST_SLABS <= steps
    clamp = lambda i: jnp.minimum(i, CAST_SLABS - 1)
    for arr, lead, col_block, n_cols in jobs:
        n_rows = arr.shape[-2]
        rows = n_rows // CAST_SLABS
        assert rows * CAST_SLABS == n_rows and rows % BF16_ROWS == 0 and arr.ndim == len(lead) + 2
        in_specs.append(pl.BlockSpec((None,) * len(lead) + (rows, n_cols),
                                     lambda i, lead=lead, cb=col_block: (*lead, clamp(i), cb)))
        args.append(arr)
        out_shape.append(jax.ShapeDtypeStruct((n_rows, n_cols), BF16))
        out_specs.append(pl.BlockSpec((rows, n_cols), lambda i: (clamp(i), 0)))
    return in_specs, args, out_shape, out_specs


def _cast_run(in_refs, out_refs):
    for src, dst in zip(in_refs, out_refs):
        dst[...] = src[...].astype(BF16)


def _ffn_body(*refs, pre, post, n_cast):
    refs = list(refs)
    x_ref = refs.pop(0)
    if pre == "rnn":
        d_ref = refs.pop(0)
    if pre == "attn":
        ao_ref, wo_ref = refs.pop(0), refs.pop(0)
    g_ref, win_ref, wout_ref = refs.pop(0), refs.pop(0), refs.pop(0)
    if post:
        g2_ref = refs.pop(0)
    cast_in = [refs.pop(0) for _ in range(n_cast)]
    o_ref = refs.pop(0)
    if post:
        xn_ref = refs.pop(0)
    _cast_run(cast_in, [refs.pop(0) for _ in range(n_cast)])
    (a_scr,) = refs
    nb = x_ref.shape[0]
    x = x_ref[...].reshape(FFN_TM, D_MODEL)
    if pre == "rnn":
        x = x + pltpu.einshape("tbd->btd", d_ref[0].reshape(RNN_TS, nb, D_MODEL)).reshape(FFN_TM, D_MODEL)
    if pre == "attn":
        x = x + _dot(ao_ref[...], wo_ref[...])
    xn = _rms(x, g_ref[...]).astype(BF16)
    for c in range(D_FF // FFN_CHUNK):
        lo = c * FFN_CHUNK
        gate = _dot(xn, win_ref[:, lo:lo + FFN_CHUNK])
        up = _dot(xn, win_ref[:, D_FF + lo:D_FF + lo + FFN_CHUNK])
        a_scr[:, lo:lo + FFN_CHUNK] = (gate * jax.nn.sigmoid(gate) * up).astype(BF16)
    y = x + FFN_RES * _dot(a_scr[...], wout_ref[...])
    o_ref[...] = y.reshape(o_ref.shape)
    if post == "rnn":
        yn = _rms(y, g2_ref[...]).astype(BF16).reshape(nb, RNN_TS, D_MODEL)
        xn_ref[0] = pltpu.einshape("btd->tbd", yn).reshape(FFN_TM, D_MODEL)
    if post == "attn":
        yn = _rms(y, g2_ref[...]).astype(BF16).reshape(FFN_TM // RESIDUES, RESIDUES, D_MODEL)
        xn_ref[0] = pltpu.einshape("lrd->rld", yn)


def _ffn(x, g, w_in, w_out, pre=None, pre_args=(), post=None, g_next=None, cast=()):
    const = lambda i: (0, 0)
    row = lambda i: (i, 0)
    if x.ndim == 3:
        nb, seq, _ = x.shape
        assert nb * RNN_TS == FFN_TM
        steps = seq // RNN_TS
        x_spec = pl.BlockSpec((nb, RNN_TS, D_MODEL), lambda i: (0, i, 0))
    else:
        assert pre != "rnn" and post != "rnn"
        steps = x.shape[0] // FFN_TM
        x_spec = pl.BlockSpec((FFN_TM, D_MODEL), row)
    tm_spec = pl.BlockSpec((1, FFN_TM, D_MODEL), lambda i: (i, 0, 0))
    in_specs = [x_spec]
    if pre == "rnn":
        in_specs.append(tm_spec)
    if pre == "attn":
        in_specs += [pl.BlockSpec((FFN_TM, D_MODEL), row),
                     pl.BlockSpec((D_MODEL, D_MODEL), const, pipeline_mode=pl.Buffered(1))]
    in_specs += [
        pl.BlockSpec((1, D_MODEL), const),
        pl.BlockSpec((D_MODEL, 2 * D_FF), const, pipeline_mode=pl.Buffered(1)),
        pl.BlockSpec((D_FF, D_MODEL), const, pipeline_mode=pl.Buffered(1)),
    ]
    args = [x, *pre_args, g.reshape(1, D_MODEL), w_in, w_out]
    out_shape = [jax.ShapeDtypeStruct(x.shape, F32)]
    out_specs = [x_spec]
    if post:
        in_specs.append(pl.BlockSpec((1, D_MODEL), const))
        args.append(g_next.reshape(1, D_MODEL))
    if post == "rnn":
        out_shape.append(jax.ShapeDtypeStruct((steps, FFN_TM, D_MODEL), BF16))
        out_specs.append(tm_spec)
    if post == "attn":
        seq = RESIDUES * BLK
        tiles = seq // FFN_TM
        out_shape.append(jax.ShapeDtypeStruct((x.shape[0] // seq, RESIDUES, BLK, D_MODEL), BF16))
        out_specs.append(pl.BlockSpec((1, RESIDUES, FFN_TM // RESIDUES, D_MODEL),
                                      lambda i: (i // tiles, 0, i % tiles, 0)))
    c_in, c_args, c_shape, c_out = _cast_specs(cast, steps)
    in_specs += c_in
    args += c_args
    out_shape += c_shape
    out_specs += c_out
    res = pl.pallas_call(
        lambda *refs: _ffn_body(*refs, pre=pre, post=post, n_cast=len(cast)),
        grid=(steps,),
        in_specs=in_specs,
        out_specs=out_specs,
        out_shape=out_shape,
        scratch_shapes=[pltpu.VMEM((FFN_TM, D_FF), BF16)],
        compiler_params=pltpu.CompilerParams(
            dimension_semantics=("arbitrary",), vmem_limit_bytes=VMEM_LIMIT_BYTES),
        name="ffn" + ("_pre_" + pre if pre else "") + ("_post_" + post if post else ""),
    )(*args)
    return res


def _rglru_body(xn_ref, win_ref, cw_ref, cb_ref, wax_ref, ba_ref, bx_ref, lam_ref, wout_ref, *rest, n_cast):
    cast_in, o_ref, cast_out = rest[:n_cast], rest[n_cast], rest[n_cast + 1:2 * n_cast + 1]
    u_scr, y_scr, h_scr = rest[2 * n_cast + 1:]
    _cast_run(cast_in, cast_out)
    nb = h_scr.shape[0]
    m = RNN_TS * nb
    hist = (CONV_WIDTH - 1) * nb

    @pl.when(pl.program_id(0) == 0)
    def _():
        u_scr[0:hist, :] = jnp.zeros((hist, D_RNN), F32)
        h_scr[...] = jnp.zeros_like(h_scr)

    xn = xn_ref[0]
    sp8 = LRU_C * jax.nn.softplus(-lam_ref[...])

    def project(n):
        w = jnp.concatenate([win_ref[:, n * RNN_BLOCK:(n + 1) * RNN_BLOCK],
                             win_ref[:, D_RNN + n * RNN_BLOCK:D_RNN + (n + 1) * RNN_BLOCK]], axis=1)
        return _dot(xn, w)

    def mix(n, gu):
        sl = slice(n * RNN_BLOCK, (n + 1) * RNN_BLOCK)
        u_scr[hist:hist + m, sl] = gu[:, RNN_BLOCK:]
        cn = cb_ref[:, sl]
        for k in range(CONV_WIDTH):
            cn = cn + u_scr[k * nb:k * nb + m, sl] * cw_ref[k:k + 1, sl]
        u_scr[0:hist, sl] = u_scr[m:m + hist, sl]
        ra = _dot(cn.astype(BF16), wax_ref[n])
        r = 0.5 + 0.5 * jnp.tanh(0.5 * (ra[:, :RNN_BLOCK] + ba_ref[:, sl]))
        gi = 0.5 + 0.5 * jnp.tanh(0.5 * (ra[:, RNN_BLOCK:] + bx_ref[:, sl]))
        nla = r * sp8[:, sl]
        a = jnp.exp(-nla)
        z = jnp.tanh(nla) * (1.0 + a * a)
        b = jnp.where(z > 0.0, z * lax.rsqrt(z), 0.0) * (gi * cn)
        h = h_scr[:, sl]
        hs = []
        for t in range(RNN_TS):
            h = a[t * nb:(t + 1) * nb] * h + b[t * nb:(t + 1) * nb]
            hs.append(h)
        h_scr[:, sl] = h
        gate = gu[:, :RNN_BLOCK]
        half = 0.5 * gate
        inner = gate * (GELU_C0 + GELU_C1 * (gate * gate))
        y_scr[:, sl] = (jnp.concatenate(hs, axis=0) * (half + half * jnp.tanh(inner))).astype(BF16)

    gu = project(0)
    for n in range(N_RNN_BLOCKS):
        gu_next = project(n + 1) if n + 1 < N_RNN_BLOCKS else None
        mix(n, gu)
        gu = gu_next
    o_ref[0] = _dot(y_scr[...], wout_ref[...])


def _rglru(xn, nb, w_in, conv_w, conv_b, wax, b_a, b_x, lam, w_out, cast=()):
    steps, m, d = xn.shape
    assert nb == 8 and m == nb * RNN_TS, "one f32 sublane group per time step"
    const2 = lambda i: (0, 0)
    tm_spec = pl.BlockSpec((1, m, d), lambda i: (i, 0, 0))
    c_in, c_args, c_shape, c_out = _cast_specs(cast, steps)
    return pl.pallas_call(
        lambda *refs: _rglru_body(*refs, n_cast=len(cast)),
        grid=(steps,),
        in_specs=[
            tm_spec,
            pl.BlockSpec((d, 2 * D_RNN), const2, pipeline_mode=pl.Buffered(1)),
            pl.BlockSpec((CONV_WIDTH, D_RNN), const2),
            pl.BlockSpec((1, D_RNN), const2),
            pl.BlockSpec((N_RNN_BLOCKS, RNN_BLOCK, 2 * RNN_BLOCK), lambda i: (0, 0, 0)),
            pl.BlockSpec((1, D_RNN), const2),
            pl.BlockSpec((1, D_RNN), const2),
            pl.BlockSpec((1, D_RNN), const2),
            pl.BlockSpec((D_RNN, d), const2, pipeline_mode=pl.Buffered(1)),
        ] + c_in,
        out_specs=[tm_spec] + c_out,
        out_shape=[jax.ShapeDtypeStruct((steps, m, d), F32)] + c_shape,
        scratch_shapes=[
            pltpu.VMEM(((CONV_WIDTH - 1) * nb + m, D_RNN), F32),
            pltpu.VMEM((m, D_RNN), BF16),
            pltpu.VMEM((nb, D_RNN), F32),
        ],
        compiler_params=pltpu.CompilerParams(
            dimension_semantics=("arbitrary",), vmem_limit_bytes=VMEM_LIMIT_BYTES),
        name="rglru",
    )(xn, w_in, conv_w, conv_b.reshape(1, D_RNN), wax,
      b_a.reshape(1, D_RNN), b_x.reshape(1, D_RNN), lam.reshape(1, D_RNN), w_out, *c_args)


def _t5_bucket(n):
    max_exact = N_BUCKETS // 2
    nf = np.maximum(n, 1).astype(np.float64)
    large = max_exact + (np.log(nf / max_exact) / math.log(MAX_DISTANCE / max_exact)
                         * (N_BUCKETS - max_exact)).astype(np.int64)
    large = np.minimum(large, N_BUCKETS - 1)
    return np.where(n < max_exact, n, large)


def _bias_codes():
    qi = np.arange(BLK)
    q_off = 16 * (qi % 8) + qi // 8
    kj = np.arange(2 * BLK)
    k_off = 16 * (kj % 16 - 8) + kj // 16
    dist = q_off[:, None] - k_off[None, :]
    band = (dist >= 0) & (dist <= BLK)
    code = np.where(band, _t5_bucket(np.maximum(dist, 0) * 1), -1)
    c0 = np.stack([np.where(k_off[None, :] >= 0, code, -1), code])
    q_off = 4 * (qi % 32) + qi // 32
    k_off = 4 * (kj % 64 - 32) + kj // 64
    dist = q_off[:, None] - k_off[None, :]
    band = (dist >= 0) & (dist <= BLK)
    code = np.where(band, _t5_bucket(np.maximum(dist, 0) * 4), -1)
    c1 = np.stack([np.where(k_off[None, :] >= 0, code, -1), code])
    dist = qi[:, None] - qi[None, :]
    c2 = np.where(dist >= 0, _t5_bucket(np.maximum(dist, 0) * 16), -1)
    return c0.astype(np.int32), c1.astype(np.int32), c2.astype(np.int32)


def _bias_body(tbl_ref, c0_ref, c1_ref, c2_ref, *rest, buckets, n_cast):
    cast_in, (b0_ref, b1_ref, b2_ref), cast_out = rest[:n_cast], rest[n_cast:n_cast + 3], rest[n_cast + 3:]
    h = pl.program_id(0)
    _cast_run(cast_in, cast_out)

    def tile(code, col, used):
        acc = jnp.full(code.shape, NEG, F32)
        for b in used:
            acc = jnp.where(code == b, tbl_ref[b, col], acc)
        return acc

    for c_ref, b_ref, g in ((c0_ref, b0_ref, 0), (c1_ref, b1_ref, 1)):
        general = tile(c_ref[1], g * N_HEADS + h, buckets[g])
        b_ref[1, 0] = general
        b_ref[0, 0] = jnp.where(c_ref[0] >= 0, general, NEG)
    b2_ref[0] = tile(c2_ref[...], 2 * N_HEADS + h, buckets[2])


def _bias_tiles(rel_bias, cast=()):
    c0, c1, c2 = _bias_codes()
    assert ((c0[0] < 0) | (c0[0] == c0[1])).all() and ((c1[0] < 0) | (c1[0] == c1[1])).all()
    buckets = tuple(tuple(int(b) for b in np.unique(c) if b >= 0) for c in (c0, c1, c2))
    full3 = lambda h: (0, 0, 0)
    c_in, c_args, c_shape, c_out = _cast_specs(cast, N_HEADS)
    return pl.pallas_call(
        lambda *refs: _bias_body(*refs, buckets=buckets, n_cast=len(cast)),
        grid=(N_HEADS,),
        in_specs=[
            pl.BlockSpec(memory_space=pltpu.SMEM),
            pl.BlockSpec((2, BLK, 2 * BLK), full3),
            pl.BlockSpec((2, BLK, 2 * BLK), full3),
            pl.BlockSpec((BLK, BLK), lambda h: (0, 0)),
        ] + c_in,
        out_specs=[
            pl.BlockSpec((2, 1, BLK, 2 * BLK), lambda h: (0, h, 0, 0)),
            pl.BlockSpec((2, 1, BLK, 2 * BLK), lambda h: (0, h, 0, 0)),
            pl.BlockSpec((1, BLK, BLK), lambda h: (h, 0, 0)),
        ] + c_out,
        out_shape=[
            jax.ShapeDtypeStruct((2, N_HEADS, BLK, 2 * BLK), F32),
            jax.ShapeDtypeStruct((2, N_HEADS, BLK, 2 * BLK), F32),
            jax.ShapeDtypeStruct((N_HEADS, BLK, BLK), F32),
        ] + c_shape,
        compiler_params=pltpu.CompilerParams(dimension_semantics=("arbitrary",)),
        name="attn_bias",
    )(rel_bias, jnp.asarray(c0), jnp.asarray(c1), jnp.asarray(c2), *c_args)


def _attn_body(*refs):
    xn_ref = refs[0]
    w_refs = refs[1:1 + 3 * N_GROUPS]
    gq_ref, bd_ref, b0_ref, b1_ref, b2_ref, o_ref, qs, ks, vs, m_st, l_st, acc_st = refs[1 + 3 * N_GROUPS:]
    seq = o_ref.shape[1]
    n_chunk = seq // ATT_CHUNK
    zpad = jnp.zeros((PAD, QW), F32)
    qs[0:PAD, :] = zpad
    ks[0:PAD, :] = zpad
    vs[0:PAD, :] = zpad
    is_a = lax.broadcasted_iota(jnp.int32, (BLK, LANES), 1) < HEAD_DIM

    def project(g):
        wg = jnp.concatenate([w_refs[t * N_GROUPS + g][...] for t in range(3)], axis=1)
        for c in range(n_chunk):
            y = _dot(xn_ref[0, ATT_CHUNK * c:ATT_CHUNK * (c + 1), :], wg)
            q, k, v = y[:, :QW], y[:, QW:2 * QW], y[:, 2 * QW:]
            ssq = _dot((q * q).astype(BF16), bd_ref[...])
            ssk = _dot((k * k).astype(BF16), bd_ref[...])
            rows = slice(PAD + ATT_CHUNK * c, PAD + ATT_CHUNK * (c + 1))
            qs[rows, :] = q * lax.rsqrt(ssq * (1.0 / HEAD_DIM) + RMS_EPS) * gq_ref[...]
            ks[rows, :] = k * lax.rsqrt(ssk * (1.0 / HEAD_DIM) + RMS_EPS)
            vs[rows, :] = v

    def gather(ref, starts, n, pad=PAD):
        return jnp.concatenate([ref[pl.ds(pad + st, n), :] for st in starts], axis=0)

    def attend(q_starts, q_n, k_starts, k_n, bias, init):
        q_blk = gather(qs, q_starts, q_n).astype(BF16)
        k_blk = gather(ks, k_starts, k_n).astype(BF16)
        v_blk = gather(vs, k_starts, k_n).astype(BF16)
        scores = []
        for sl in range(QW // LANES):
            cols = slice(LANES * sl, LANES * (sl + 1))
            q = q_blk[:, cols]
            q2 = jnp.concatenate([jnp.where(is_a, q, 0), jnp.where(is_a, 0, q)], axis=0)
            scores.append(lax.dot_general(q2, k_blk[:, cols], (((1,), (1,)), ((), ())),
                                          preferred_element_type=F32))
        for sl in range(QW // LANES):
            cols = slice(LANES * sl, LANES * (sl + 1))
            s = scores[sl] + jnp.concatenate([bias(2 * sl), bias(2 * sl + 1)], axis=0)
            m = jnp.max(s, axis=-1, keepdims=True)
            p = jnp.exp(s - m)
            l = jnp.sum(p, axis=-1, keepdims=True)
            o2 = _dot(p.astype(BF16), v_blk[:, cols])
            mb = jnp.where(is_a, m[:BLK], m[BLK:])
            lb = jnp.where(is_a, l[:BLK], l[BLK:])
            ob = jnp.where(is_a, o2[:BLK], o2[BLK:])
            if not init:
                mo = gather(m_st, q_starts, q_n, 0)[:, cols]
                lo = gather(l_st, q_starts, q_n, 0)[:, cols]
                ao = gather(acc_st, q_starts, q_n, 0)[:, cols]
                mn = jnp.maximum(mo, mb)
                al = jnp.exp(mo - mn)
                be = jnp.exp(mb - mn)
                lb = al * lo + be * lb
                ob = al * ao + be * ob
                mb = mn
            for i, st in enumerate(q_starts):
                piece = slice(i * q_n, (i + 1) * q_n)
                m_st[pl.ds(st, q_n), cols] = mb[piece]
                l_st[pl.ds(st, q_n), cols] = lb[piece]
                acc_st[pl.ds(st, q_n), cols] = ob[piece]


    project(0)
    p0 = BLK // RESIDUES
    for n in range(BLK // p0):
        q_starts = [n * p0 + r * BLK for r in range(RESIDUES)]
        k_starts = [st - p0 for st in q_starts]
        attend(q_starts, p0, k_starts, 2 * p0, lambda h, var=min(n, 1): b0_ref[var, h], init=True)

    project(1)
    d1 = DILATION_GROUPS[1][1]
    p1 = BLK // d1
    for r4 in range(d1):
        for n in range(BLK // p1):
            q_starts = [r4 * BLK + n * p1 + d1 * j * BLK for j in range(RESIDUES // d1)]
            k_starts = [st - p1 for st in q_starts]
            attend(q_starts, p1, k_starts, 2 * p1, lambda h, var=min(n, 1): b1_ref[var, h], init=False)

    project(2)
    for r in range(seq // BLK):
        attend([r * BLK], BLK, [r * BLK], BLK, lambda h: b2_ref[h], init=False)

    o = (acc_st[...] / l_st[...]).astype(BF16).reshape(RESIDUES, seq // RESIDUES, QW)
    o_ref[0] = pltpu.einshape("rlc->lrc", o).reshape(seq, QW)


def _attention(xn, w_q, w_k, w_v, gq, b0, b1, b2):
    b, s, d = xn.shape
    assert s == RESIDUES * BLK, "group 2 must be one block per residue"
    nq = N_HEADS // QUAD
    bd = np.kron(np.eye(QUAD), np.ones((HEAD_DIM, HEAD_DIM))).astype(np.float32)
    w_specs = [pl.BlockSpec((d, QW), lambda i, j, g=g: (0, g * nq + j)) for _ in range(3) for g in range(N_GROUPS)]
    w_args = [w for w in (w_q, w_k, w_v) for _ in range(N_GROUPS)]
    return pl.pallas_call(
        _attn_body,
        grid=(b, nq),
        in_specs=[pl.BlockSpec((1, s, d), lambda i, j: (i, 0, 0))] + w_specs + [
            pl.BlockSpec((1, QW), lambda i, j: (0, 0)),
            pl.BlockSpec((QW, QW), lambda i, j: (0, 0)),
            pl.BlockSpec((2, QUAD, BLK, 2 * BLK), lambda i, j: (0, j, 0, 0)),
            pl.BlockSpec((2, QUAD, BLK, 2 * BLK), lambda i, j: (0, j, 0, 0)),
            pl.BlockSpec((QUAD, BLK, BLK), lambda i, j: (j, 0, 0)),
        ],
        out_specs=pl.BlockSpec((1, s, QW), lambda i, j: (i, 0, j)),
        out_shape=jax.ShapeDtypeStruct((b, s, d), BF16),
        scratch_shapes=[pltpu.VMEM((PAD + s, QW), F32)] * 3 + [pltpu.VMEM((s, QW), F32)] * 3,
        compiler_params=pltpu.CompilerParams(
            dimension_semantics=("arbitrary", "arbitrary"), vmem_limit_bytes=VMEM_LIMIT_BYTES),
        name="dilated_attn",
    )(xn, *w_args, gq, jnp.asarray(bd, BF16), b0, b1, b2)


def kernel(x, norm_g, ffn_w_in, ffn_w_out, rnn_w_in, rnn_conv_w, rnn_conv_b, rnn_w_a, rnn_b_a, rnn_w_x,
           rnn_b_x, rnn_lambda, rnn_w_out, att_w_qkv, att_q_gain, att_k_gain, att_w_o, rel_bias):
    b, s, d = x.shape
    ffn_cast = lambda layer, slot: [(ffn_w_in, (layer, slot), 0, 2 * D_FF), (ffn_w_out, (layer, slot), 0, d)]
    qkv_cast = lambda third: [(att_w_qkv, (0,), third, att_w_qkv.shape[-1] // 3)]
    b0, b1, b2, w_in, w_out = _bias_tiles(rel_bias, cast=ffn_cast(0, 0))

    rnn_cast = [(rnn_w_in, (0,), 0, 2 * D_RNN), (rnn_w_out, (0,), 0, d)]
    h, hn, w_in_a, w_out_a, w_rnn_in, w_rnn_out = _ffn(x, norm_g[0, 0], w_in, w_out, post="rnn", g_next=norm_g[0, 1],
                                                       cast=ffn_cast(0, 1) + rnn_cast)
    wax = jnp.concatenate([rnn_w_a[0], rnn_w_x[0]], axis=-1).astype(BF16)
    delta, w_in_b, w_out_b, w_q = _rglru(hn, b, w_rnn_in, rnn_conv_w[0], rnn_conv_b[0], wax, rnn_b_a[0], rnn_b_x[0],
                                         rnn_lambda[0], w_rnn_out, cast=ffn_cast(1, 0) + qkv_cast(0))
    h, w_k = _ffn(h, norm_g[0, 2], w_in_a, w_out_a, pre="rnn", pre_args=(delta,), cast=qkv_cast(1))

    h, hn, w_in, w_out, w_v = _ffn(h.reshape(b * s, d), norm_g[1, 0], w_in_b, w_out_b, post="attn",
                                   g_next=norm_g[1, 1], cast=ffn_cast(1, 1) + qkv_cast(2))
    gq = jnp.tile(att_q_gain[0] * att_k_gain[0] * HEAD_DIM ** -0.5, QUAD).reshape(1, QW)
    o = _attention(hn.reshape(b, s, d), w_q, w_k, w_v, gq, b0, b1, b2)
    (h,) = _ffn(h, norm_g[1, 2], w_in, w_out, pre="attn", pre_args=(o.reshape(b * s, d), att_w_o[0].astype(BF16)))
    return h.reshape(b, s, d)
```

```python
import math

import numpy as np
import jax
import jax.numpy as jnp
from jax import lax
from jax.experimental import pallas as pl
from jax.experimental.pallas import tpu as pltpu

F32 = jnp.float32
BF16 = jnp.bfloat16

D_MODEL = 1024
RMS_EPS = 1e-6
D_FF = 2816
FFN_RES = 0.5
D_RNN = 1280
N_RNN_BLOCKS = 10
RNN_BLOCK = 128
CONV_WIDTH = 4
LRU_C = 8.0
GELU_C0 = math.sqrt(2.0 / math.pi)
GELU_C1 = 0.044715 * GELU_C0
HEAD_DIM = 64
N_HEADS = 16
DILATION_GROUPS = ((128, 1), (512, 4), (2048, 16))
N_GROUPS = 3
N_BUCKETS = 32
MAX_DISTANCE = 2048
BLK = 128
RESIDUES = 16

VMEM_LIMIT_BYTES = 56 * 1024 * 1024
FFN_TM = 1024
FFN_CHUNK = 256
RNN_TS = 128
NEG = -1e30
PAD = 32
LANES = 128
BF16_ROWS = 16
ATT_CHUNK = 512
CAST_SLABS = 16
QUAD = 4
QW = QUAD * HEAD_DIM


def _rms(x, g):
    return x * lax.rsqrt(jnp.mean(x * x, axis=-1, keepdims=True) + RMS_EPS) * g


def _dot(a, b):
    return jnp.dot(a, b, preferred_element_type=F32)


def _cast_specs(jobs, steps):
    in_specs, args, out_shape, out_specs = [], [], [], []
    assert CAST_SLABS <= steps
    clamp = lambda i: jnp.minimum(i, CAST_SLABS - 1)
    for arr, lead, col_block, n_cols in jobs:
        n_rows = arr.shape[-2]
        rows = n_rows // CAST_SLABS
        assert rows * CAST_SLABS == n_rows and rows % BF16_ROWS == 0 and arr.ndim == len(lead) + 2
        in_specs.append(pl.BlockSpec((None,) * len(lead) + (rows, n_cols),
                                     lambda i, lead=lead, cb=col_block: (*lead, clamp(i), cb)))
        args.append(arr)
        out_shape.append(jax.ShapeDtypeStruct((n_rows, n_cols), BF16))
        out_specs.append(pl.BlockSpec((rows, n_cols), lambda i: (clamp(i), 0)))
    return in_specs, args, out_shape, out_specs


def _cast_run(in_refs, out_refs):
    for src, dst in zip(in_refs, out_refs):
        dst[...] = src[...].astype(BF16)


def _ffn_body(*refs, pre, post, n_cast):
    refs = list(refs)
    x_ref = refs.pop(0)
    if pre == "rnn":
        d_ref = refs.pop(0)
    if pre == "attn":
        ao_ref, wo_ref = refs.pop(0), refs.pop(0)
    g_ref, win_ref, wout_ref = refs.pop(0), refs.pop(0), refs.pop(0)
    if post:
        g2_ref = refs.pop(0)
    cast_in = [refs.pop(0) for _ in range(n_cast)]
    o_ref = refs.pop(0)
    if post:
        xn_ref = refs.pop(0)
    _cast_run(cast_in, [refs.pop(0) for _ in range(n_cast)])
    (a_scr,) = refs
    nb = x_ref.shape[0]
    x = x_ref[...].reshape(FFN_TM, D_MODEL)
    if pre == "rnn":
        x = x + pltpu.einshape("tbd->btd", d_ref[0].reshape(RNN_TS, nb, D_MODEL)).reshape(FFN_TM, D_MODEL)
    if pre == "attn":
        x = x + _dot(ao_ref[...], wo_ref[...])
    xn = _rms(x, g_ref[...]).astype(BF16)
    for c in range(D_FF // FFN_CHUNK):
        lo = c * FFN_CHUNK
        gate = _dot(xn, win_ref[:, lo:lo + FFN_CHUNK])
        up = _dot(xn, win_ref[:, D_FF + lo:D_FF + lo + FFN_CHUNK])
        half = 0.5 * gate
        a_scr[:, lo:lo + FFN_CHUNK] = ((half + half * jnp.tanh(half)) * up).astype(BF16)
    y = x + FFN_RES * _dot(a_scr[...], wout_ref[...])
    o_ref[...] = y.reshape(o_ref.shape)
    if post == "rnn":
        yn = _rms(y, g2_ref[...]).astype(BF16).reshape(nb, RNN_TS, D_MODEL)
        xn_ref[0] = pltpu.einshape("btd->tbd", yn).reshape(FFN_TM, D_MODEL)
    if post == "attn":
        yn = _rms(y, g2_ref[...]).astype(BF16).reshape(FFN_TM // RESIDUES, RESIDUES, D_MODEL)
        xn_ref[0] = pltpu.einshape("lrd->rld", yn)


def _ffn(x, g, w_in, w_out, pre=None, pre_args=(), post=None, g_next=None, cast=()):
    const = lambda i: (0, 0)
    row = lambda i: (i, 0)
    if x.ndim == 3:
        nb, seq, _ = x.shape
        assert nb * RNN_TS == FFN_TM
        steps = seq // RNN_TS
        x_spec = pl.BlockSpec((nb, RNN_TS, D_MODEL), lambda i: (0, i, 0))
    else:
        assert pre != "rnn" and post != "rnn"
        steps = x.shape[0] // FFN_TM
        x_spec = pl.BlockSpec((FFN_TM, D_MODEL), row)
    tm_spec = pl.BlockSpec((1, FFN_TM, D_MODEL), lambda i: (i, 0, 0))
    in_specs = [x_spec]
    if pre == "rnn":
        in_specs.append(tm_spec)
    if pre == "attn":
        in_specs += [pl.BlockSpec((FFN_TM, D_MODEL), row),
                     pl.BlockSpec((D_MODEL, D_MODEL), const, pipeline_mode=pl.Buffered(1))]
    in_specs += [
        pl.BlockSpec((1, D_MODEL), const),
        pl.BlockSpec((D_MODEL, 2 * D_FF), const, pipeline_mode=pl.Buffered(1)),
        pl.BlockSpec((D_FF, D_MODEL), const, pipeline_mode=pl.Buffered(1)),
    ]
    args = [x, *pre_args, g.reshape(1, D_MODEL), w_in, w_out]
    out_shape = [jax.ShapeDtypeStruct(x.shape, F32)]
    out_specs = [x_spec]
    if post:
        in_specs.append(pl.BlockSpec((1, D_MODEL), const))
        args.append(g_next.reshape(1, D_MODEL))
    if post == "rnn":
        out_shape.append(jax.ShapeDtypeStruct((steps, FFN_TM, D_MODEL), BF16))
        out_specs.append(tm_spec)
    if post == "attn":
        seq = RESIDUES * BLK
        tiles = seq // FFN_TM
        out_shape.append(jax.ShapeDtypeStruct((x.shape[0] // seq, RESIDUES, BLK, D_MODEL), BF16))
        out_specs.append(pl.BlockSpec((1, RESIDUES, FFN_TM // RESIDUES, D_MODEL),
                                      lambda i: (i // tiles, 0, i % tiles, 0)))
    c_in, c_args, c_shape, c_out = _cast_specs(cast, steps)
    in_specs += c_in
    args += c_args
    out_shape += c_shape
    out_specs += c_out
    res = pl.pallas_call(
        lambda *refs: _ffn_body(*refs, pre=pre, post=post, n_cast=len(cast)),
        grid=(steps,),
        in_specs=in_specs,
        out_specs=out_specs,
        out_shape=out_shape,
        scratch_shapes=[pltpu.VMEM((FFN_TM, D_FF), BF16)],
        compiler_params=pltpu.CompilerParams(
            dimension_semantics=("arbitrary",), vmem_limit_bytes=VMEM_LIMIT_BYTES),
        name="ffn" + ("_pre_" + pre if pre else "") + ("_post_" + post if post else ""),
    )(*args)
    return res


def _rglru_body(xn_ref, win_ref, cw_ref, cb_ref, wax_ref, ba_ref, bx_ref, lam_ref, wout_ref, *rest, n_cast):
    cast_in, o_ref, cast_out = rest[:n_cast], rest[n_cast], rest[n_cast + 1:2 * n_cast + 1]
    u_scr, y_scr, h_scr = rest[2 * n_cast + 1:]
    _cast_run(cast_in, cast_out)
    nb = h_scr.shape[0]
    m = RNN_TS * nb
    hist = (CONV_WIDTH - 1) * nb

    @pl.when(pl.program_id(0) == 0)
    def _():
        u_scr[0:hist, :] = jnp.zeros((hist, D_RNN), F32)
        h_scr[...] = jnp.zeros_like(h_scr)

    xn = xn_ref[0]
    sp8 = LRU_C * jax.nn.softplus(-lam_ref[...])

    def project(n):
        w = jnp.concatenate([win_ref[:, n * RNN_BLOCK:(n + 1) * RNN_BLOCK],
                             win_ref[:, D_RNN + n * RNN_BLOCK:D_RNN + (n + 1) * RNN_BLOCK]], axis=1)
        return _dot(xn, w)

    def mix(n, gu):
        sl = slice(n * RNN_BLOCK, (n + 1) * RNN_BLOCK)
        u_scr[hist:hist + m, sl] = gu[:, RNN_BLOCK:]
        cn = cb_ref[:, sl]
        for k in range(CONV_WIDTH):
            cn = cn + u_scr[k * nb:k * nb + m, sl] * cw_ref[k:k + 1, sl]
        u_scr[0:hist, sl] = u_scr[m:m + hist, sl]
        ra = _dot(cn.astype(BF16), wax_ref[n])
        r = 0.5 + 0.5 * jnp.tanh(0.5 * (ra[:, :RNN_BLOCK] + ba_ref[:, sl]))
        gi = 0.5 + 0.5 * jnp.tanh(0.5 * (ra[:, RNN_BLOCK:] + bx_ref[:, sl]))
        nla = r * sp8[:, sl]
        a = jnp.exp(-nla)
        z = jnp.tanh(nla) * (1.0 + a * a)
        b = jnp.where(z > 0.0, z * lax.rsqrt(z), 0.0) * (gi * cn)
        h = h_scr[:, sl]
        hs = []
        for t in range(RNN_TS):
            h = a[t * nb:(t + 1) * nb] * h + b[t * nb:(t + 1) * nb]
            hs.append(h)
        h_scr[:, sl] = h
        gate = gu[:, :RNN_BLOCK]
        half = 0.5 * gate
        inner = gate * (GELU_C0 + GELU_C1 * (gate * gate))
        y_scr[:, sl] = (jnp.concatenate(hs, axis=0) * (half + half * jnp.tanh(inner))).astype(BF16)

    gu = project(0)
    for n in range(N_RNN_BLOCKS):
        gu_next = project(n + 1) if n + 1 < N_RNN_BLOCKS else None
        mix(n, gu)
        gu = gu_next
    o_ref[0] = _dot(y_scr[...], wout_ref[...])


def _rglru(xn, nb, w_in, conv_w, conv_b, wax, b_a, b_x, lam, w_out, cast=()):
    steps, m, d = xn.shape
    assert nb == 8 and m == nb * RNN_TS, "one f32 sublane group per time step"
    const2 = lambda i: (0, 0)
    tm_spec = pl.BlockSpec((1, m, d), lambda i: (i, 0, 0))
    c_in, c_args, c_shape, c_out = _cast_specs(cast, steps)
    return pl.pallas_call(
        lambda *refs: _rglru_body(*refs, n_cast=len(cast)),
        grid=(steps,),
        in_specs=[
            tm_spec,
            pl.BlockSpec((d, 2 * D_RNN), const2, pipeline_mode=pl.Buffered(1)),
            pl.BlockSpec((CONV_WIDTH, D_RNN), const2),
            pl.BlockSpec((1, D_RNN), const2),
            pl.BlockSpec((N_RNN_BLOCKS, RNN_BLOCK, 2 * RNN_BLOCK), lambda i: (0, 0, 0)),
            pl.BlockSpec((1, D_RNN), const2),
            pl.BlockSpec((1, D_RNN), const2),
            pl.BlockSpec((1, D_RNN), const2),
            pl.BlockSpec((D_RNN, d), const2, pipeline_mode=pl.Buffered(1)),
        ] + c_in,
        out_specs=[tm_spec] + c_out,
        out_shape=[jax.ShapeDtypeStruct((steps, m, d), F32)] + c_shape,
        scratch_shapes=[
            pltpu.VMEM(((CONV_WIDTH - 1) * nb + m, D_RNN), F32),
            pltpu.VMEM((m, D_RNN), BF16),
            pltpu.VMEM((nb, D_RNN), F32),
        ],
        compiler_params=pltpu.CompilerParams(
            dimension_semantics=("arbitrary",), vmem_limit_bytes=VMEM_LIMIT_BYTES),
        name="rglru",
    )(xn, w_in, conv_w, conv_b.reshape(1, D_RNN), wax,
      b_a.reshape(1, D_RNN), b_x.reshape(1, D_RNN), lam.reshape(1, D_RNN), w_out, *c_args)


def _t5_bucket(n):
    max_exact = N_BUCKETS // 2
    nf = np.maximum(n, 1).astype(np.float64)
    large = max_exact + (np.log(nf / max_exact) / math.log(MAX_DISTANCE / max_exact)
                         * (N_BUCKETS - max_exact)).astype(np.int64)
    large = np.minimum(large, N_BUCKETS - 1)
    return np.where(n < max_exact, n, large)


def _bias_codes():
    qi = np.arange(BLK)
    q_off = 16 * (qi % 8) + qi // 8
    kj = np.arange(2 * BLK)
    k_off = 16 * (kj % 16 - 8) + kj // 16
    dist = q_off[:, None] - k_off[None, :]
    band = (dist >= 0) & (dist <= BLK)
    code = np.where(band, _t5_bucket(np.maximum(dist, 0) * 1), -1)
    c0 = np.stack([np.where(k_off[None, :] >= 0, code, -1), code])
    q_off = 4 * (qi % 32) + qi // 32
    k_off = 4 * (kj % 64 - 32) + kj // 64
    dist = q_off[:, None] - k_off[None, :]
    band = (dist >= 0) & (dist <= BLK)
    code = np.where(band, _t5_bucket(np.maximum(dist, 0) * 4), -1)
    c1 = np.stack([np.where(k_off[None, :] >= 0, code, -1), code])
    dist = qi[:, None] - qi[None, :]
    c2 = np.where(dist >= 0, _t5_bucket(np.maximum(dist, 0) * 16), -1)
    return c0.astype(np.int32), c1.astype(np.int32), c2.astype(np.int32)


def _bias_body(tbl_ref, c0_ref, c1_ref, c2_ref, *rest, buckets, n_cast):
    cast_in, (b0_ref, b1_ref, b2_ref), cast_out = rest[:n_cast], rest[n_cast:n_cast + 3], rest[n_cast + 3:]
    h = pl.program_id(0)
    _cast_run(cast_in, cast_out)

    def tile(code, col, used):
        acc = jnp.full(code.shape, NEG, F32)
        for b in used:
            acc = jnp.where(code == b, tbl_ref[b, col], acc)
        return acc

    for c_ref, b_ref, g in ((c0_ref, b0_ref, 0), (c1_ref, b1_ref, 1)):
        general = tile(c_ref[1], g * N_HEADS + h, buckets[g])
        b_ref[1, 0] = general
        b_ref[0, 0] = jnp.where(c_ref[0] >= 0, general, NEG)
    b2_ref[0] = tile(c2_ref[...], 2 * N_HEADS + h, buckets[2])


def _bias_tiles(rel_bias, cast=()):
    c0, c1, c2 = _bias_codes()
    assert ((c0[0] < 0) | (c0[0] == c0[1])).all() and ((c1[0] < 0) | (c1[0] == c1[1])).all()
    buckets = tuple(tuple(int(b) for b in np.unique(c) if b >= 0) for c in (c0, c1, c2))
    full3 = lambda h: (0, 0, 0)
    c_in, c_args, c_shape, c_out = _cast_specs(cast, N_HEADS)
    return pl.pallas_call(
        lambda *refs: _bias_body(*refs, buckets=buckets, n_cast=len(cast)),
        grid=(N_HEADS,),
        in_specs=[
            pl.BlockSpec(memory_space=pltpu.SMEM),
            pl.BlockSpec((2, BLK, 2 * BLK), full3),
            pl.BlockSpec((2, BLK, 2 * BLK), full3),
            pl.BlockSpec((BLK, BLK), lambda h: (0, 0)),
        ] + c_in,
        out_specs=[
            pl.BlockSpec((2, 1, BLK, 2 * BLK), lambda h: (0, h, 0, 0)),
            pl.BlockSpec((2, 1, BLK, 2 * BLK), lambda h: (0, h, 0, 0)),
            pl.BlockSpec((1, BLK, BLK), lambda h: (h, 0, 0)),
        ] + c_out,
        out_shape=[
            jax.ShapeDtypeStruct((2, N_HEADS, BLK, 2 * BLK), F32),
            jax.ShapeDtypeStruct((2, N_HEADS, BLK, 2 * BLK), F32),
            jax.ShapeDtypeStruct((N_HEADS, BLK, BLK), F32),
        ] + c_shape,
        compiler_params=pltpu.CompilerParams(dimension_semantics=("arbitrary",)),
        name="attn_bias",
    )(rel_bias, jnp.asarray(c0), jnp.asarray(c1), jnp.asarray(c2), *c_args)


def _attn_body(*refs):
    xn_ref = refs[0]
    w_refs = refs[1:1 + 3 * N_GROUPS]
    gq_ref, bd_ref, b0_ref, b1_ref, b2_ref, o_ref, qs, ks, vs, m_st, l_st, acc_st = refs[1 + 3 * N_GROUPS:]
    seq = o_ref.shape[1]
    n_chunk = seq // ATT_CHUNK
    zpad = jnp.zeros((PAD, QW), F32)
    qs[0:PAD, :] = zpad
    ks[0:PAD, :] = zpad
    vs[0:PAD, :] = zpad
    is_a = lax.broadcasted_iota(jnp.int32, (BLK, LANES), 1) < HEAD_DIM

    def project(g):
        wg = jnp.concatenate([w_refs[t * N_GROUPS + g][...] for t in range(3)], axis=1)
        for c in range(n_chunk):
            y = _dot(xn_ref[0, ATT_CHUNK * c:ATT_CHUNK * (c + 1), :], wg)
            q, k, v = y[:, :QW], y[:, QW:2 * QW], y[:, 2 * QW:]
            ssq = _dot((q * q).astype(BF16), bd_ref[...])
            ssk = _dot((k * k).astype(BF16), bd_ref[...])
            rows = slice(PAD + ATT_CHUNK * c, PAD + ATT_CHUNK * (c + 1))
            qs[rows, :] = q * lax.rsqrt(ssq * (1.0 / HEAD_DIM) + RMS_EPS) * gq_ref[...]
            ks[rows, :] = k * lax.rsqrt(ssk * (1.0 / HEAD_DIM) + RMS_EPS)
            vs[rows, :] = v

    def gather(ref, starts, n, pad=PAD):
        return jnp.concatenate([ref[pl.ds(pad + st, n), :] for st in starts], axis=0)

    def attend(q_starts, q_n, k_starts, k_n, bias, init):
        q_blk = gather(qs, q_starts, q_n).astype(BF16)
        k_blk = gather(ks, k_starts, k_n).astype(BF16)
        v_blk = gather(vs, k_starts, k_n).astype(BF16)
        scores = []
        for sl in range(QW // LANES):
            cols = slice(LANES * sl, LANES * (sl + 1))
            q = q_blk[:, cols]
            q2 = jnp.concatenate([jnp.where(is_a, q, 0), jnp.where(is_a, 0, q)], axis=0)
            scores.append(lax.dot_general(q2, k_blk[:, cols], (((1,), (1,)), ((), ())),
                                          preferred_element_type=F32))
        for sl in range(QW // LANES):
            cols = slice(LANES * sl, LANES * (sl + 1))
            s = scores[sl] + jnp.concatenate([bias(2 * sl), bias(2 * sl + 1)], axis=0)
            m = jnp.max(s, axis=-1, keepdims=True)
            p = jnp.exp(s - m)
            l = jnp.sum(p, axis=-1, keepdims=True)
            o2 = _dot(p.astype(BF16), v_blk[:, cols])
            mb = jnp.where(is_a, m[:BLK], m[BLK:])
            lb = jnp.where(is_a, l[:BLK], l[BLK:])
            ob = jnp.where(is_a, o2[:BLK], o2[BLK:])
            if not init:
                mo = gather(m_st, q_starts, q_n, 0)[:, cols]
                lo = gather(l_st, q_starts, q_n, 0)[:, cols]
                ao = gather(acc_st, q_starts, q_n, 0)[:, cols]
                mn = jnp.maximum(mo, mb)
                al = jnp.exp(mo - mn)
                be = jnp.exp(mb - mn)
                lb = al * lo + be * lb
                ob = al * ao + be * ob
                mb = mn
            for i, st in enumerate(q_starts):
                piece = slice(i * q_n, (i + 1) * q_n)
                m_st[pl.ds(st, q_n), cols] = mb[piece]
                l_st[pl.ds(st, q_n), cols] = lb[piece]
                acc_st[pl.ds(st, q_n), cols] = ob[piece]


    project(0)
    p0 = BLK // RESIDUES
    for n in range(BLK // p0):
        q_starts = [n * p0 + r * BLK for r in range(RESIDUES)]
        k_starts = [st - p0 for st in q_starts]
        attend(q_starts, p0, k_starts, 2 * p0, lambda h, var=min(n, 1): b0_ref[var, h], init=True)

    project(1)
    d1 = DILATION_GROUPS[1][1]
    p1 = BLK // d1
    for r4 in range(d1):
        for n in range(BLK // p1):
            q_starts = [r4 * BLK + n * p1 + d1 * j * BLK for j in range(RESIDUES // d1)]
            k_starts = [st - p1 for st in q_starts]
            attend(q_starts, p1, k_starts, 2 * p1, lambda h, var=min(n, 1): b1_ref[var, h], init=False)

    project(2)
    for r in range(seq // BLK):
        attend([r * BLK], BLK, [r * BLK], BLK, lambda h: b2_ref[h], init=False)

    o = (acc_st[...] / l_st[...]).astype(BF16).reshape(RESIDUES, seq // RESIDUES, QW)
    o_ref[0] = pltpu.einshape("rlc->lrc", o).reshape(seq, QW)


def _attention(xn, w_q, w_k, w_v, gq, b0, b1, b2):
    b, s, d = xn.shape
    assert s == RESIDUES * BLK, "group 2 must be one block per residue"
    nq = N_HEADS // QUAD
    bd = np.kron(np.eye(QUAD), np.ones((HEAD_DIM, HEAD_DIM))).astype(np.float32)
    w_specs = [pl.BlockSpec((d, QW), lambda i, j, g=g: (0, g * nq + j)) for _ in range(3) for g in range(N_GROUPS)]
    w_args = [w for w in (w_q, w_k, w_v) for _ in range(N_GROUPS)]
    return pl.pallas_call(
        _attn_body,
        grid=(b, nq),
        in_specs=[pl.BlockSpec((1, s, d), lambda i, j: (i, 0, 0))] + w_specs + [
            pl.BlockSpec((1, QW), lambda i, j: (0, 0)),
            pl.BlockSpec((QW, QW), lambda i, j: (0, 0)),
            pl.BlockSpec((2, QUAD, BLK, 2 * BLK), lambda i, j: (0, j, 0, 0)),
            pl.BlockSpec((2, QUAD, BLK, 2 * BLK), lambda i, j: (0, j, 0, 0)),
            pl.BlockSpec((QUAD, BLK, BLK), lambda i, j: (j, 0, 0)),
        ],
        out_specs=pl.BlockSpec((1, s, QW), lambda i, j: (i, 0, j)),
        out_shape=jax.ShapeDtypeStruct((b, s, d), BF16),
        scratch_shapes=[pltpu.VMEM((PAD + s, QW), F32)] * 3 + [pltpu.VMEM((s, QW), F32)] * 3,
        compiler_params=pltpu.CompilerParams(
            dimension_semantics=("arbitrary", "arbitrary"), vmem_limit_bytes=VMEM_LIMIT_BYTES),
        name="dilated_attn",
    )(xn, *w_args, gq, jnp.asarray(bd, BF16), b0, b1, b2)


def kernel(x, norm_g, ffn_w_in, ffn_w_out, rnn_w_in, rnn_conv_w, rnn_conv_b, rnn_w_a, rnn_b_a, rnn_w_x,
           rnn_b_x, rnn_lambda, rnn_w_out, att_w_qkv, att_q_gain, att_k_gain, att_w_o, rel_bias):
    b, s, d = x.shape
    ffn_cast = lambda layer, slot: [(ffn_w_in, (layer, slot), 0, 2 * D_FF), (ffn_w_out, (layer, slot), 0, d)]
    qkv_cast = lambda third: [(att_w_qkv, (0,), third, att_w_qkv.shape[-1] // 3)]
    b0, b1, b2, w_in, w_out = _bias_tiles(rel_bias, cast=ffn_cast(0, 0))

    rnn_cast = [(rnn_w_in, (0,), 0, 2 * D_RNN), (rnn_w_out, (0,), 0, d)]
    h, hn, w_in_a, w_out_a, w_rnn_in, w_rnn_out = _ffn(x, norm_g[0, 0], w_in, w_out, post="rnn", g_next=norm_g[0, 1],
                                                       cast=ffn_cast(0, 1) + rnn_cast)
    wax = jnp.concatenate([rnn_w_a[0], rnn_w_x[0]], axis=-1).astype(BF16)
    delta, w_in_b, w_out_b, w_q = _rglru(hn, b, w_rnn_in, rnn_conv_w[0], rnn_conv_b[0], wax, rnn_b_a[0], rnn_b_x[0],
                                         rnn_lambda[0], w_rnn_out, cast=ffn_cast(1, 0) + qkv_cast(0))
    h, w_k = _ffn(h, norm_g[0, 2], w_in_a, w_out_a, pre="rnn", pre_args=(delta,), cast=qkv_cast(1))

    h, hn, w_in, w_out, w_v = _ffn(h.reshape(b * s, d), norm_g[1, 0], w_in_b, w_out_b, post="attn",
                                   g_next=norm_g[1, 1], cast=ffn_cast(1, 1) + qkv_cast(2))
    gq = jnp.tile(att_q_gain[0] * att_k_gain[0] * HEAD_DIM ** -0.5, QUAD).reshape(1, QW)
    o = _attention(hn.reshape(b, s, d), w_q, w_k, w_v, gq, b0, b1, b2)
    (h,) = _ffn(h, norm_g[1, 2], w_in, w_out, pre="attn", pre_args=(o.reshape(b * s, d), att_w_o[0].astype(BF16)))
    return h.reshape(b, s, d)
```
